```python
import math, functools
import jax, jax.numpy as jnp
from jax import lax
import numpy as np

D_MODEL = 2048
BATCH = 1
SEQ = 8192
DEPTH = 4
DEC_BATCH = 16
DEC_SEQ = 16
PAST_LEN = 4096

CHUNK = 64
HEAD_DIM = 128
A_HEADS = (3 * D_MODEL // 8) // HEAD_DIM
A_DK = 128
A_DV = HEAD_DIM
A_KDIM = A_HEADS * A_DK
A_WIDTH = A_HEADS * A_DV
C_HEADS = (3 * D_MODEL // 8) // HEAD_DIM
C_HD = HEAD_DIM
C_WIDTH = C_HEADS * C_HD
Q_BLOCK = 128
B_WIDTH = D_MODEL - A_WIDTH - C_WIDTH
POOL_WINDOWS = (2, 4, 8, 16)
POOL_GROUPS = 4
POOL_GDIM = B_WIDTH // POOL_GROUPS
POOL_BUF = 15
IN_WIDTH = 2 * A_KDIM + 2 * A_WIDTH + B_WIDTH + 3 * C_WIDTH
D_FF = 5632
N_EXPERTS = 8
TOP_K = 2
DN_ALPHA = (2 * DEPTH) ** 0.25
DN_BETA = (8 * DEPTH) ** -0.25
LN_EPS = 1e-5
RMS_EPS = 1e-6
LB_FLOOR = 1e-30

kernel_name = "hymba_style_hgrn2_pool_stickbreak_stream_step"


def layer_norm(x, g, b):
    xf = x.astype(jnp.float32)
    mu = jnp.mean(xf, axis=-1, keepdims=True)
    var = jnp.mean(jnp.square(xf - mu), axis=-1, keepdims=True)
    return ((xf - mu) * lax.rsqrt(var + LN_EPS) * g.astype(jnp.float32) + b.astype(jnp.float32)).astype(x.dtype)


def rms_norm(x, g):
    xf = x.astype(jnp.float32)
    return xf * lax.rsqrt(jnp.mean(jnp.square(xf), axis=-1, keepdims=True) + RMS_EPS) * g.astype(jnp.float32)


def hgrn_lower_bounds(lb_param):
    p = jax.nn.softmax(lb_param.astype(jnp.float32), axis=0)
    return jnp.cumsum(p, axis=0) - p[0]


def hgrn2_scan(q, logf, k, v, s0):
    B, T, H, DK = q.shape
    DV = v.shape[-1]
    L = CHUNK if T % CHUNK == 0 else T
    n = T // L

    def to_chunks(a):
        return jnp.moveaxis(a.reshape(B, n, L, *a.shape[2:]), 1, 0)

    causal = jnp.tril(jnp.ones((L, L), dtype=bool))[None, :, :, None, None]

    def step(S, inp):
        qc, gc, kc, vc = inp
        b = jnp.cumsum(gc, axis=1)
        diff = b[:, :, None] - b[:, None, :]
        decay = jnp.where(causal, jnp.exp(jnp.where(causal, diff, 0.0)), 0.0)
        att = jnp.einsum('bthk,bshk,btshk->bhts', qc, kc, decay)
        o = (jnp.einsum('bhts,bshv->bthv', att, vc)
             + jnp.einsum('bthk,bhkv->bthv', qc * jnp.exp(b), S))
        b_last = b[:, -1]
        k_dec = kc * jnp.exp(b_last[:, None] - b)
        S = jnp.exp(b_last)[..., None] * S + jnp.einsum('bshk,bshv->bhkv', k_dec, vc)
        return S, o

    S, o = lax.scan(step, s0, (to_chunks(q), to_chunks(logf), to_chunks(k), to_chunks(v)))
    return jnp.moveaxis(o, 0, 1).reshape(B, T, H, DV), S


def pool_mix(u, buf, start, w_pool, pool_scale):
    B, T, C = u.shape
    up = jnp.concatenate([buf.astype(u.dtype), u], axis=1).astype(jnp.float32)
    cs = jnp.concatenate([jnp.zeros((B, 1, C), jnp.float32), jnp.cumsum(up, axis=1)], axis=1)
    pos = start + jnp.arange(T)
    hi = cs[:, POOL_BUF + 1:]
    cur = up[:, POOL_BUF:]
    groups = []
    for g, w in enumerate(POOL_WINDOWS):
        sl = slice(g * POOL_GDIM, (g + 1) * POOL_GDIM)
        lo = cs[:, POOL_BUF + 1 - w:POOL_BUF + 1 - w + T, sl]
        cnt = jnp.minimum(w, pos + 1).astype(jnp.float32)[None, :, None]
        groups.append((hi[..., sl] - lo) / cnt - cur[..., sl])
    d = jnp.stack(groups, axis=2)
    y = jnp.einsum('btgc,gcd->btgd', d, w_pool.astype(jnp.float32)).reshape(B, T, C)
    y = y * pool_scale.astype(jnp.float32)
    return y.astype(u.dtype), up[:, -POOL_BUF:].astype(u.dtype)


def stick_breaking(q, k, v, q_start):
    B, Tq, H, Dh = q.shape
    Tk = k.shape[1]
    qb = Q_BLOCK if Tq % Q_BLOCK == 0 else Tq
    nb = Tq // qb
    kpos = jnp.arange(Tk)
    scale = Dh ** -0.5
    kf = k.astype(jnp.float32)
    vf = v.astype(jnp.float32)

    def block(args):
        qblk, qpos = args
        z = jnp.einsum('bqhd,bkhd->bhqk', qblk.astype(jnp.float32), kf) * scale
        mask = kpos[None, :] < qpos[:, None]
        log1m = jnp.where(mask, jax.nn.log_sigmoid(-z), 0.0)
        between = lax.cumsum(log1m, axis=3, reverse=True) - log1m
        a = jnp.where(mask, jnp.exp(jnp.where(mask, jax.nn.log_sigmoid(z) + between, 0.0)), 0.0)
        return jnp.einsum('bhqk,bkhd->bqhd', a, vf)

    qblocks = jnp.moveaxis(q.reshape(B, nb, qb, H, Dh), 1, 0)
    qpos = (q_start + jnp.arange(Tq)).reshape(nb, qb)
    o = lax.map(block, (qblocks, qpos))
    return jnp.moveaxis(o, 0, 1).reshape(B, Tq, H, Dh).astype(q.dtype)


def token_mixers(x, s_hgrn, pool_buf, k_past, v_past, lb, w_in, hgrn_norm, w_pool, pool_scale, w_out):
    B, T, _ = x.shape
    start = k_past.shape[1]
    f32 = jnp.float32
    p = x @ w_in
    sizes = (A_KDIM, A_KDIM, A_WIDTH, A_WIDTH, B_WIDTH, C_WIDTH, C_WIDTH, C_WIDTH)
    idx = [int(i) for i in np.cumsum(sizes)[:-1]]
    aq, af, ai, ag, bu, cq, ck, cv = jnp.split(p, idx, axis=-1)

    lbh = lb.reshape(A_HEADS, A_DK).astype(f32)
    z = af.astype(f32).reshape(B, T, A_HEADS, A_DK)
    log_lb = jnp.log(jnp.maximum(lbh, LB_FLOOR))
    logf = jnp.logaddexp(log_lb, jnp.log1p(-lbh) + jax.nn.log_sigmoid(z))
    kin = (1.0 - lbh) * jax.nn.sigmoid(-z)
    qa = jax.nn.silu(aq.astype(f32)).reshape(B, T, A_HEADS, A_DK)
    va = ai.astype(f32).reshape(B, T, A_HEADS, A_DV)
    oa, s_new = hgrn2_scan(qa, logf, kin, va, s_hgrn.astype(f32))
    oa = rms_norm(oa, hgrn_norm).reshape(B, T, A_WIDTH) * jax.nn.silu(ag.astype(f32))

    ob, buf_new = pool_mix(bu, pool_buf, start, w_pool, pool_scale)

    qc = cq.reshape(B, T, C_HEADS, C_HD)
    kc = ck.reshape(B, T, C_HEADS, C_HD)
    vc = cv.reshape(B, T, C_HEADS, C_HD)
    k_all = jnp.concatenate([k_past.astype(x.dtype), kc], axis=1)
    v_all = jnp.concatenate([v_past.astype(x.dtype), vc], axis=1)
    oc = stick_breaking(qc, k_all, v_all, start).reshape(B, T, C_WIDTH)

    mix = jnp.concatenate([oa.astype(x.dtype), ob, oc], axis=-1) @ w_out
    return mix, s_new.astype(x.dtype), buf_new, kc, vc


def swiglu(h, w1, w2):
    g, u = jnp.split(h @ w1, 2, axis=-1)
    return (jax.nn.silu(g) * u) @ w2


def moe_swiglu(h, router, w1, w2):
    logits = jnp.einsum('btd,de->bte', h, router).astype(jnp.float32)
    top_v, top_i = lax.top_k(logits, TOP_K)
    gates = jax.nn.softmax(top_v, axis=-1)
    combine = jnp.sum(jax.nn.one_hot(top_i, N_EXPERTS, dtype=jnp.float32) * gates[..., None], axis=-2)
    y = jnp.zeros(h.shape, jnp.float32)
    for e in range(N_EXPERTS):
        y = y + combine[..., e:e + 1] * swiglu(h, w1[e], w2[e]).astype(jnp.float32)
    return y.astype(h.dtype)


def apply_layer(x, s0, buf0, k_past, v_past, mixer_w, ln_w, ffn):
    mix, s_new, buf_new, k_new, v_new = token_mixers(x, s0, buf0, k_past, v_past, *mixer_w)
    g1, b1, g2, b2 = ln_w
    h = layer_norm(DN_ALPHA * x + mix, g1, b1)
    y = layer_norm(DN_ALPHA * h + ffn(h), g2, b2)
    return y, s_new, buf_new, k_new, v_new


def setup_inputs(seed: int = 0) -> dict:
    key = jax.random.key(seed)
    ks = jax.random.split(key, 24)
    f32 = jnp.float32
    n_dense = (DEPTH + 1) // 2
    n_moe = DEPTH // 2

    def nrm(k, shape, s):
        return jax.random.normal(k, shape, f32) * s

    return {
        "x_prompt": nrm(ks[0], (BATCH, SEQ, D_MODEL), 1.0),
        "x_sample": nrm(ks[1], (DEC_BATCH, DEC_SEQ, D_MODEL), 1.0),
        "cache_sb_k": nrm(ks[2], (DEPTH, DEC_BATCH, PAST_LEN, C_HEADS, C_HD), 1.0),
        "cache_sb_v": nrm(ks[3], (DEPTH, DEC_BATCH, PAST_LEN, C_HEADS, C_HD), 1.0),
        "state_hgrn": nrm(ks[4], (DEPTH, DEC_BATCH, A_HEADS, A_DK, A_DV), 0.5),
        "state_pool": nrm(ks[5], (DEPTH, DEC_BATCH, POOL_BUF, B_WIDTH), 1.0),
        "w_in": nrm(ks[6], (DEPTH, D_MODEL, IN_WIDTH), D_MODEL ** -0.5),
        "lb_param": nrm(ks[7], (DEPTH, A_KDIM), 0.5),
        "hgrn_norm": 1.0 + nrm(ks[8], (DEPTH, A_HEADS, A_DV), 0.02),
        "w_pool": nrm(ks[9], (DEPTH, POOL_GROUPS, POOL_GDIM, POOL_GDIM), POOL_GDIM ** -0.5),
        "pool_scale": 1.0 + nrm(ks[10], (DEPTH, B_WIDTH), 0.02),
        "w_out": nrm(ks[11], (DEPTH, D_MODEL, D_MODEL), DN_BETA * D_MODEL ** -0.5),
        "ln1_g": 1.0 + nrm(ks[12], (DEPTH, D_MODEL), 0.02),
        "ln1_b": nrm(ks[13], (DEPTH, D_MODEL), 0.02),
        "ln2_g": 1.0 + nrm(ks[14], (DEPTH, D_MODEL), 0.02),
        "ln2_b": nrm(ks[15], (DEPTH, D_MODEL), 0.02),
        "ffn_w1": nrm(ks[16], (n_dense, D_MODEL, 2 * D_FF), D_MODEL ** -0.5),
        "ffn_w2": nrm(ks[17], (n_dense, D_FF, D_MODEL), DN_BETA * D_FF ** -0.5),
        "moe_router": nrm(ks[18], (n_moe, D_MODEL, N_EXPERTS), D_MODEL ** -0.5),
        "moe_w1": nrm(ks[19], (n_moe, N_EXPERTS, D_MODEL, 2 * D_FF), D_MODEL ** -0.5),
        "moe_w2": nrm(ks[20], (n_moe, N_EXPERTS, D_FF, D_MODEL), DN_BETA * D_FF ** -0.5),
    }


def reference(x_prompt, x_sample, cache_sb_k, cache_sb_v, state_hgrn, state_pool,
              w_in, lb_param, hgrn_norm, w_pool, pool_scale, w_out,
              ln1_g, ln1_b, ln2_g, ln2_b, ffn_w1, ffn_w2, moe_router, moe_w1, moe_w2):
    lb_all = hgrn_lower_bounds(lb_param)
    bp = x_prompt.shape[0]
    xp, xs = x_prompt, x_sample
    p_s0 = jnp.zeros((bp, A_HEADS, A_DK, A_DV), jnp.float32)
    p_buf0 = jnp.zeros((bp, POOL_BUF, B_WIDTH), x_prompt.dtype)
    p_kv0 = jnp.zeros((bp, 0, C_HEADS, C_HD), x_prompt.dtype)
    p_S, p_buf, p_k, p_v = [], [], [], []
    s_S, s_buf, s_k, s_v = [], [], [], []
    for l in range(DEPTH):
        mixer_w = (lb_all[l], w_in[l], hgrn_norm[l], w_pool[l], pool_scale[l], w_out[l])
        ln_w = (ln1_g[l], ln1_b[l], ln2_g[l], ln2_b[l])
        if l % 2 == 0:
            ffn = functools.partial(swiglu, w1=ffn_w1[l // 2], w2=ffn_w2[l // 2])
        else:
            ffn = functools.partial(moe_swiglu, router=moe_router[l // 2],
                                    w1=moe_w1[l // 2], w2=moe_w2[l // 2])
        xp, S, buf, k, v = apply_layer(xp, p_s0, p_buf0, p_kv0, p_kv0, mixer_w, ln_w, ffn)
        p_S.append(S); p_buf.append(buf); p_k.append(k); p_v.append(v)
        xs, S, buf, k, v = apply_layer(xs, state_hgrn[l], state_pool[l], cache_sb_k[l], cache_sb_v[l],
                                       mixer_w, ln_w, ffn)
        s_S.append(S); s_buf.append(buf); s_k.append(k); s_v.append(v)
    hgrn_prompt = jnp.stack(p_S)
    pool_prompt = jnp.stack(p_buf)
    k_prompt = jnp.stack(p_k)
    v_prompt = jnp.stack(p_v)
    hgrn_sample = jnp.stack(s_S)
    pool_sample = jnp.stack(s_buf)
    k_sample = jnp.stack(s_k)
    v_sample = jnp.stack(s_v)
    return (xp, xs, hgrn_prompt, pool_prompt, k_prompt, v_prompt,
            hgrn_sample, pool_sample, k_sample, v_sample)
```

```python
import functools

import jax
import jax.numpy as jnp
from jax import lax
from jax.experimental import pallas as pl
from jax.experimental.pallas import tpu as pltpu

F32 = jnp.float32
BF16 = jnp.bfloat16

HEAD_DIM = 128
HGRN_BLOCK_LOG2 = 4
HGRN_BLOCK = 1 << HGRN_BLOCK_LOG2
HGRN_TILE = 256
POOL_WINDOWS = (2, 4, 8, 16)
POOL_HALO = 16
TOP_K = 2
LN_EPS = 1e-5
RMS_EPS = 1e-6
LB_FLOOR = 1e-30
SB_DEAD_LOG = -104.0
VMEM_LIMIT_BYTES = 56 * 1024 * 1024


def _params(*sem):
    return pltpu.CompilerParams(dimension_semantics=sem, vmem_limit_bytes=VMEM_LIMIT_BYTES)


def _dot(a, b):
    return jnp.dot(a, b, preferred_element_type=F32)


def _dot_nt(a, b):
    return lax.dot_general(a, b, (((1,), (1,)), ((), ())), preferred_element_type=F32)


def _split3(x):
    h1 = x.astype(BF16)
    r1 = x - h1.astype(F32)
    h2 = r1.astype(BF16)
    h3 = (r1 - h2.astype(F32)).astype(BF16)
    return h1, h2, h3


def _sigmoid(x):
    return 1.0 / (1.0 + jnp.exp(-x))


def _log_sigmoid(x):
    return jnp.minimum(x, 0.0) - jnp.log1p(jnp.exp(-jnp.abs(x)))


def _mm_kernel(a_ref, w_ref, o_ref):
    o_ref[...] = _dot(a_ref[...], w_ref[...].astype(BF16)).astype(o_ref.dtype)


def matmul_layer(a, w, layer, *, tm, tn):
    m, k = a.shape
    n = w.shape[2]
    assert m % tm == 0 and n % tn == 0
    return pl.pallas_call(
        _mm_kernel,
        grid=(m // tm, n // tn),
        in_specs=[pl.BlockSpec((tm, k), lambda i, j: (i, 0)),
                  pl.BlockSpec((None, k, tn), lambda i, j: (layer, 0, j))],
        out_specs=pl.BlockSpec((tm, tn), lambda i, j: (i, j)),
        out_shape=jax.ShapeDtypeStruct((m, n), F32),
        compiler_params=_params("parallel", "arbitrary"),
        name="in_proj",
    )(a, w)


def _swiglu_kernel(x_ref, wg_ref, wu_ref, *rest, scaled):
    if scaled:
        s_ref, o_ref = rest
    else:
        (o_ref,) = rest
    x = x_ref[...]
    g = _dot(x, wg_ref[...].astype(BF16))
    u = _dot(x, wu_ref[...].astype(BF16))
    act = g * _sigmoid(g) * u
    if scaled:
        s = s_ref[...]
        act = act * jnp.concatenate([s] * (act.shape[1] // s.shape[1]), axis=1)
    o_ref[...] = act.astype(o_ref.dtype)


def swiglu_up(x, w1, layer, combine=None, *, tm, tn):
    m, k = x.shape
    f = w1.shape[-1] // 2
    assert m % tm == 0 and f % tn == 0
    nj = f // tn
    if combine is None:
        grid = (m // tm, nj)
        in_specs = [pl.BlockSpec((tm, k), lambda i, j: (i, 0)),
                    pl.BlockSpec((None, k, tn), lambda i, j: (layer, 0, j)),
                    pl.BlockSpec((None, k, tn), lambda i, j: (layer, 0, nj + j))]
        out_specs = pl.BlockSpec((tm, tn), lambda i, j: (i, j))
        out_cols = f
        args = (x, w1, w1)
        sem = ("parallel", "arbitrary")
    else:
        ne = w1.shape[1]
        grid = (m // tm, ne, nj)
        in_specs = [pl.BlockSpec((tm, k), lambda i, e, j: (i, 0)),
                    pl.BlockSpec((None, None, k, tn), lambda i, e, j: (layer, e, 0, j)),
                    pl.BlockSpec((None, None, k, tn), lambda i, e, j: (layer, e, 0, nj + j)),
                    pl.BlockSpec((None, tm, HEAD_DIM), lambda i, e, j: (e, i, 0))]
        out_specs = pl.BlockSpec((tm, tn), lambda i, e, j: (i, e * nj + j))
        out_cols = ne * f
        args = (x, w1, w1, combine)
        sem = ("parallel", "arbitrary", "arbitrary")
    return pl.pallas_call(
        functools.partial(_swiglu_kernel, scaled=combine is not None),
        grid=grid, in_specs=in_specs, out_specs=out_specs,
        out_shape=jax.ShapeDtypeStruct((m, out_cols), BF16),
        compiler_params=_params(*sem),
        name="swiglu_up",
    )(*args)


def _mm_ln_kernel(a_ref, w_ref, r_ref, g_ref, b_ref, of_ref, ob_ref, *, alpha, nk):
    kk = pl.program_id(1)

    @pl.when(kk == 0)
    def _():
        of_ref[...] = jnp.zeros_like(of_ref)

    of_ref[...] += _dot(a_ref[...], w_ref[...].astype(BF16))

    @pl.when(kk == nk - 1)
    def _():
        r = alpha * r_ref[...] + of_ref[...]
        mu = jnp.mean(r, axis=-1, keepdims=True)
        c = r - mu
        var = jnp.mean(c * c, axis=-1, keepdims=True)
        y = c * lax.rsqrt(var + LN_EPS) * g_ref[...] + b_ref[...]
        of_ref[...] = y
        ob_ref[...] = y.astype(BF16)


def matmul_residual_ln(a, w, w_row0, resid, g, b, layer, *, alpha, tm, tk):
    m, k = a.shape
    n = w.shape[1]
    assert m % tm == 0 and k % tk == 0 and w_row0 % tk == 0
    nk = k // tk
    kb0 = w_row0 // tk
    row = pl.BlockSpec((tm, n), lambda i, kk: (i, 0))
    vec = pl.BlockSpec((None, 1, n), lambda i, kk: (layer, 0, 0))
    return pl.pallas_call(
        functools.partial(_mm_ln_kernel, alpha=alpha, nk=nk),
        grid=(m // tm, nk),
        in_specs=[pl.BlockSpec((tm, tk), lambda i, kk: (i, kk)),
                  pl.BlockSpec((tk, n), lambda i, kk: (kb0 + kk, 0)),
                  row, vec, vec],
        out_specs=[row, row],
        out_shape=[jax.ShapeDtypeStruct((m, n), F32), jax.ShapeDtypeStruct((m, n), BF16)],
        compiler_params=_params("parallel", "arbitrary"),
        name="proj_ln",
    )(a, w, resid, g, b)


def _router_kernel(h_ref, r_ref, o_ref, *, ne):
    logits = jnp.dot(h_ref[...], r_ref[...], preferred_element_type=F32,
                     precision=lax.Precision.HIGHEST)
    lane = lax.broadcasted_iota(jnp.int32, logits.shape, 1)
    neg = jnp.float32(-jnp.inf)
    x = jnp.where(lane < ne, logits, neg)
    m1 = jnp.max(x, axis=-1, keepdims=True)
    i1 = jnp.min(jnp.where(x == m1, lane, ne), axis=-1, keepdims=True)
    x2 = jnp.where(lane == i1, neg, x)
    m2 = jnp.max(x2, axis=-1, keepdims=True)
    i2 = jnp.min(jnp.where(x2 == m2, lane, ne), axis=-1, keepdims=True)
    e2 = jnp.exp(m2 - m1)
    g1 = 1.0 / (1.0 + e2)
    g2 = e2 / (1.0 + e2)
    ones = jnp.ones(logits.shape, F32)
    for e in range(ne):
        o_ref[e] = jnp.where(i1 == e, g1, 0.0) * ones + jnp.where(i2 == e, g2, 0.0) * ones


def router_combine(h, router_padded, layer, ne, *, tm):
    m, k = h.shape
    return pl.pallas_call(
        functools.partial(_router_kernel, ne=ne),
        grid=(m // tm,),
        in_specs=[pl.BlockSpec((tm, k), lambda i: (i, 0)),
                  pl.BlockSpec((None, k, HEAD_DIM), lambda i: (layer, 0, 0))],
        out_specs=pl.BlockSpec((ne, tm, HEAD_DIM), lambda i: (0, i, 0)),
        out_shape=jax.ShapeDtypeStruct((ne, m, HEAD_DIM), F32),
        compiler_params=_params("parallel"),
        name="router",
    )(h, router_padded)


def _hgrn_kernel(q_ref, z_ref, v_ref, g_ref, lb_ref, gn_ref, s0_ref, o_ref, sout_ref,
                 st_s, qt_s, kd_s, vt_s, ebl_s, acc_s, *, heads, per_block_state):
    rows, width = q_ref.shape
    nblk = rows // HGRN_BLOCK
    step = pl.program_id(0)

    lb = lb_ref[...]
    log_lb = jnp.log(jnp.maximum(lb, LB_FLOOR))
    z = z_ref[...]
    lo = jnp.log1p(-lb) + _log_sigmoid(z)
    logf = jnp.maximum(log_lb, lo) + jnp.log1p(jnp.exp(-jnp.abs(log_lb - lo)))
    kin = (1.0 - lb) * _sigmoid(-z)
    qraw = q_ref[...]
    q = qraw * _sigmoid(qraw)
    v = v_ref[...]

    r_i = lax.broadcasted_iota(jnp.int32, (rows, rows), 0)
    c_i = lax.broadcasted_iota(jnp.int32, (rows, rows), 1)
    same = (r_i >> HGRN_BLOCK_LOG2) == (c_i >> HGRN_BLOCK_LOG2)
    tri = jnp.where(same & (c_i <= r_i), 1.0, 0.0).astype(BF16)
    blk = jnp.where(same, 1.0, 0.0).astype(BF16)
    parts = _split3(logf)
    b = _dot(tri, parts[0]) + _dot(tri, parts[1]) + _dot(tri, parts[2])
    btot = _dot(blk, parts[0]) + _dot(blk, parts[1]) + _dot(blk, parts[2])
    qt_s[...] = q * jnp.exp(b)
    kd_s[...] = kin * jnp.exp(btot - b)
    ebl_s[...] = jnp.exp(btot)
    vt_s[...] = v.T

    pos = lax.broadcasted_iota(jnp.int32, (rows, width), 0) & (HGRN_BLOCK - 1)
    acc = None
    for d in range(HGRN_BLOCK):
        if d == 0:
            w = q * kin
            vs = v
        else:
            decay = jnp.exp(b - pltpu.roll(b, d, 0))
            w = jnp.where(pos >= d, q * pltpu.roll(kin, d, 0) * decay, 0.0)
            vs = pltpu.roll(v, d, 0)
        att = jnp.concatenate(
            [jnp.broadcast_to(jnp.sum(w[:, h * HEAD_DIM:(h + 1) * HEAD_DIM], axis=1, keepdims=True),
                              (rows, HEAD_DIM)) for h in range(heads)], axis=1)
        acc = att * vs if acc is None else acc + att * vs
    acc_s[...] = acc

    if not per_block_state:
        @pl.when(step == 0)
        def _():
            st_s[...] = jnp.zeros_like(st_s)

    lane_blk = lax.broadcasted_iota(jnp.int32, (HEAD_DIM, rows), 1) >> HGRN_BLOCK_LOG2

    def block_step(c, carry):
        r0 = pl.multiple_of(c * HGRN_BLOCK, HGRN_BLOCK)
        for h in range(heads):
            hc = slice(h * HEAD_DIM, (h + 1) * HEAD_DIM)
            if per_block_state:
                st = s0_ref[c, h].T
            else:
                st = st_s[h]
            qc = qt_s[pl.ds(r0, HGRN_BLOCK), hc]
            acc_s[pl.ds(r0, HGRN_BLOCK), hc] += _dot_nt(qc.astype(BF16), st.astype(BF16))
            vmask = jnp.where(lane_blk == c, vt_s[hc, :], 0.0).astype(BF16)
            upd = _dot(vmask, kd_s[:, hc].astype(BF16))
            st_new = st * ebl_s[pl.ds(r0, 1), hc] + upd
            if per_block_state:
                sout_ref[c, h] = st_new.T
            else:
                st_s[h] = st_new
        return carry

    lax.fori_loop(0, nblk, block_step, 0)

    if not per_block_state:
        @pl.when(step == pl.num_programs(0) - 1)
        def _():
            for h in range(heads):
                sout_ref[0, h] = st_s[h].T

    o = acc_s[...]
    ms = jnp.concatenate(
        [jnp.broadcast_to(jnp.sum(jnp.square(o[:, h * HEAD_DIM:(h + 1) * HEAD_DIM]), axis=1, keepdims=True),
                          (rows, HEAD_DIM)) for h in range(heads)], axis=1) * (1.0 / HEAD_DIM)
    gate = g_ref[...]
    o_ref[...] = (o * lax.rsqrt(ms + RMS_EPS) * gn_ref[...] * (gate * _sigmoid(gate))).astype(o_ref.dtype)


def hgrn_mixer(p, row0, nrows, lb, gnorm, layer, s0, *, per_block_state, width):
    heads = width // HEAD_DIM
    tile = HGRN_TILE
    assert nrows % tile == 0 and row0 % tile == 0
    rb0 = row0 // tile
    nt = nrows // tile
    if per_block_state:
        assert nt == 1
        nstate = s0.shape[0]
    else:
        nstate = 1
        s0 = jnp.zeros((1, heads, HEAD_DIM, HEAD_DIM), F32)

    def col(cb):
        return pl.BlockSpec((tile, width), lambda i: (rb0 + i, cb))

    vec = pl.BlockSpec((None, 1, width), lambda i: (layer, 0, 0))
    state = pl.BlockSpec((nstate, heads, HEAD_DIM, HEAD_DIM), lambda i: (0, 0, 0, 0))
    return pl.pallas_call(
        functools.partial(_hgrn_kernel, heads=heads, per_block_state=per_block_state),
        grid=(nt,),
        in_specs=[col(0), col(1), col(2), col(3), vec, vec, state],
        out_specs=[pl.BlockSpec((tile, width), lambda i: (i, 0)), state],
        out_shape=[jax.ShapeDtypeStruct((nrows, width), BF16),
                   jax.ShapeDtypeStruct((nstate, heads, HEAD_DIM, HEAD_DIM), F32)],
        scratch_shapes=[pltpu.VMEM((heads, HEAD_DIM, HEAD_DIM), F32),
                        pltpu.VMEM((tile, width), F32),
                        pltpu.VMEM((tile, width), F32),
                        pltpu.VMEM((width, tile), F32),
                        pltpu.VMEM((tile, width), F32),
                        pltpu.VMEM((tile, width), F32)],
        compiler_params=_params("arbitrary"),
        name="hgrn2",
    )(p, p, p, p, lb, gnorm, s0)


def _pool_tile(xp, pos, w_ref, scale):
    sums = {1: xp}
    s = xp
    w = 1
    while w < POOL_WINDOWS[-1]:
        s = s + pltpu.roll(s, w, 0)
        w *= 2
        sums[w] = s
    gd = HEAD_DIM
    outs = []
    for g, win in enumerate(POOL_WINDOWS):
        sl = slice(g * gd, (g + 1) * gd)
        cnt = jnp.clip(pos + 1, 1, win).astype(F32)
        d = sums[win][:, sl] / cnt - xp[:, sl]
        outs.append(_dot(d.astype(BF16), w_ref[g].astype(BF16)))
    return jnp.concatenate(outs, axis=1) * scale


def _pool_stream_kernel(u_ref, w_ref, sc_ref, o_ref, xp_s):
    step = pl.program_id(0)
    rows = u_ref.shape[0]

    @pl.when(step == 0)
    def _():
        xp_s[0:POOL_HALO, :] = jnp.zeros((POOL_HALO, xp_s.shape[1]), F32)

    xp_s[POOL_HALO:, :] = u_ref[...]
    xp = xp_s[...]
    pos = step * rows - POOL_HALO + lax.broadcasted_iota(jnp.int32, (rows + POOL_HALO, 1), 0)
    y = _pool_tile(xp, pos, w_ref, sc_ref[...])
    o_ref[...] = y[POOL_HALO:, :].astype(o_ref.dtype)
    xp_s[0:POOL_HALO, :] = xp[rows:, :]


def _pool_batch_kernel(u_ref, buf_ref, w_ref, sc_ref, o_ref, xp_s, *, start):
    nb, t, c = u_ref.shape
    seg = POOL_HALO + t
    for bi in range(nb):
        xp_s[bi * seg:bi * seg + POOL_HALO, :] = buf_ref[bi]
        xp_s[bi * seg + POOL_HALO:(bi + 1) * seg, :] = u_ref[bi]
    r = lax.broadcasted_iota(jnp.int32, (nb * seg, 1), 0) & (seg - 1)
    y = _pool_tile(xp_s[...], start - POOL_HALO + r, w_ref, sc_ref[...])
    for bi in range(nb):
        o_ref[bi] = y[bi * seg + POOL_HALO:(bi + 1) * seg, :].astype(o_ref.dtype)


def pool_stream(p, nrows, col_block, w_pool, pool_scale, layer, *, width, tile):
    assert nrows % tile == 0
    return pl.pallas_call(
        _pool_stream_kernel,
        grid=(nrows // tile,),
        in_specs=[pl.BlockSpec((tile, width), lambda i: (i, col_block)),
                  pl.BlockSpec((None,) + w_pool.shape[1:], lambda i: (layer, 0, 0, 0)),
                  pl.BlockSpec((None, 1, width), lambda i: (layer, 0, 0))],
        out_specs=pl.BlockSpec((tile, width), lambda i: (i, 0)),
        out_shape=jax.ShapeDtypeStruct((nrows, width), BF16),
        scratch_shapes=[pltpu.VMEM((tile + POOL_HALO, width), F32)],
        compiler_params=_params("arbitrary"),
        name="pool_stream",
    )(p, w_pool, pool_scale)


def pool_batch(u, buf, w_pool, pool_scale, layer, *, start):
    nb, t, c = u.shape
    return pl.pallas_call(
        functools.partial(_pool_batch_kernel, start=start),
        grid=(1,),
        in_specs=[pl.BlockSpec((nb, t, c), lambda i: (0, 0, 0)),
                  pl.BlockSpec((nb, POOL_HALO, c), lambda i: (0, 0, 0)),
                  pl.BlockSpec((None,) + w_pool.shape[1:], lambda i: (layer, 0, 0, 0)),
                  pl.BlockSpec((None, 1, c), lambda i: (layer, 0, 0))],
        out_specs=pl.BlockSpec((nb, t, c), lambda i: (0, 0, 0)),
        out_shape=jax.ShapeDtypeStruct((nb, t, c), BF16),
        scratch_shapes=[pltpu.VMEM((nb * (POOL_HALO + t), c), F32)],
        compiler_params=_params("arbitrary"),
        name="pool_batch",
    )(u, buf, w_pool, pool_scale)


def _sb_block(qb, kblk, vblk, mask, r_run, upper, scale):
    z = _dot_nt(qb, kblk.astype(BF16)) * scale
    l1m = _log_sigmoid(-z)
    if mask is not None:
        l1m = jnp.where(mask, l1m, 0.0)
    parts = _split3(l1m)
    later = _dot(parts[0], upper) + _dot(parts[1], upper) + _dot(parts[2], upper)
    a = jnp.exp(z + l1m + later + r_run)
    if mask is not None:
        a = jnp.where(mask, a, 0.0)
    return _dot(a.astype(BF16), vblk.astype(BF16)), jnp.sum(l1m, axis=1, keepdims=True)


def _strict_upper(tk):
    j = lax.broadcasted_iota(jnp.int32, (tk, tk), 0)
    s = lax.broadcasted_iota(jnp.int32, (tk, tk), 1)
    return jnp.where(j > s, 1.0, 0.0).astype(BF16)


def _sb_stream_kernel(q_ref, k_ref, v_ref, o_ref, acc_s, run_s, *, tk, scale):
    tq = q_ref.shape[0]
    qi = pl.program_id(1)
    qb = q_ref[...].astype(BF16)
    upper = _strict_upper(tk)
    acc_s[...] = jnp.zeros_like(acc_s)
    run_s[...] = jnp.zeros_like(run_s)
    qpos = qi * tq + lax.broadcasted_iota(jnp.int32, (tq, tk), 0)
    col = lax.broadcasted_iota(jnp.int32, (tq, tk), 1)

    def cond(carry):
        kb, live = carry
        return jnp.logical_and(kb >= 0, live > SB_DEAD_LOG)

    def body(carry):
        kb, _ = carry
        k0 = pl.multiple_of(kb * tk, tk)
        mask = (k0 + col) < qpos
        r_run = run_s[...]
        contrib, rowsum = _sb_block(qb, k_ref[pl.ds(k0, tk), :], v_ref[pl.ds(k0, tk), :],
                                    mask, r_run, upper, scale)
        acc_s[...] += contrib
        r_new = r_run + rowsum
        run_s[...] = r_new
        return kb - 1, jnp.max(r_new)

    lax.while_loop(cond, body, ((qi + 1) * (tq // tk) - 1, jnp.float32(0.0)))
    o_ref[...] = acc_s[...].astype(o_ref.dtype)


def sb_stream(p, nrows, col_q, col_k, col_v, *, heads, tq, tk):
    assert nrows % tq == 0 and tq % tk == 0
    scale = HEAD_DIM ** -0.5
    return pl.pallas_call(
        functools.partial(_sb_stream_kernel, tk=tk, scale=scale),
        grid=(heads, nrows // tq),
        in_specs=[pl.BlockSpec((tq, HEAD_DIM), lambda h, i: (i, col_q + h)),
                  pl.BlockSpec((nrows, HEAD_DIM), lambda h, i: (0, col_k + h)),
                  pl.BlockSpec((nrows, HEAD_DIM), lambda h, i: (0, col_v + h))],
        out_specs=pl.BlockSpec((tq, HEAD_DIM), lambda h, i: (i, h)),
        out_shape=jax.ShapeDtypeStruct((nrows, heads * HEAD_DIM), BF16),
        scratch_shapes=[pltpu.VMEM((tq, HEAD_DIM), F32), pltpu.VMEM((tq, 1), F32)],
        compiler_params=_params("parallel", "arbitrary"),
        name="sb_stream",
    )(p, p, p)


def _sb_decode_kernel(q_ref, kn_ref, vn_ref, kp_ref, vp_ref, o_ref, kpad_s, vpad_s, acc_s, run_s,
                      *, tk, scale):
    t = q_ref.shape[0]
    npast = kp_ref.shape[0]
    qb = q_ref[...].astype(BF16)
    upper = _strict_upper(tk)

    kpad_s[...] = jnp.zeros_like(kpad_s)
    vpad_s[...] = jnp.zeros_like(vpad_s)
    kpad_s[0:t, :] = kn_ref[...]
    vpad_s[0:t, :] = vn_ref[...]
    row = lax.broadcasted_iota(jnp.int32, (t, tk), 0)
    col = lax.broadcasted_iota(jnp.int32, (t, tk), 1)
    contrib, rowsum = _sb_block(qb, kpad_s[...], vpad_s[...], col < row,
                                jnp.zeros((t, 1), F32), upper, scale)
    acc_s[...] = contrib
    run_s[...] = rowsum

    def cond(carry):
        kb, live = carry
        return jnp.logical_and(kb >= 0, live > SB_DEAD_LOG)

    def body(carry):
        kb, _ = carry
        k0 = pl.multiple_of(kb * tk, tk)
        r_run = run_s[...]
        contrib, rowsum = _sb_block(qb, kp_ref[pl.ds(k0, tk), :], vp_ref[pl.ds(k0, tk), :],
                                    None, r_run, upper, scale)
        acc_s[...] += contrib
        r_new = r_run + rowsum
        run_s[...] = r_new
        return kb - 1, jnp.max(r_new)

    lax.while_loop(cond, body, (npast // tk - 1, jnp.max(rowsum)))
    o_ref[...] = acc_s[...].astype(o_ref.dtype)


def sb_decode(p, row0, nb, t, col_q, col_k, col_v, k_past, v_past, layer, *, heads, tk):
    npast = k_past.shape[2]
    assert row0 % t == 0 and npast % tk == 0 and t <= tk
    rb0 = row0 // t
    scale = HEAD_DIM ** -0.5

    def new(cb):
        return pl.BlockSpec((t, HEAD_DIM), lambda b, h: (rb0 + b, cb + h))

    past = pl.BlockSpec((None, None, npast, HEAD_DIM), lambda b, h: (layer, b, 0, h))
    return pl.pallas_call(
        functools.partial(_sb_decode_kernel, tk=tk, scale=scale),
        grid=(nb, heads),
        in_specs=[new(col_q), new(col_k), new(col_v), past, past],
        out_specs=pl.BlockSpec((t, HEAD_DIM), lambda b, h: (b, h)),
        out_shape=jax.ShapeDtypeStruct((nb * t, heads * HEAD_DIM), BF16),
        scratch_shapes=[pltpu.VMEM((tk, HEAD_DIM), F32), pltpu.VMEM((tk, HEAD_DIM), F32),
                        pltpu.VMEM((t, HEAD_DIM), F32), pltpu.VMEM((t, 1), F32)],
        compiler_params=_params("parallel", "arbitrary"),
        name="sb_decode",
    )(p, p, p, k_past, v_past)


def kernel(x_prompt, x_sample, cache_sb_k, cache_sb_v, state_hgrn, state_pool, w_in, lb_param, hgrn_norm,
           w_pool, pool_scale, w_out, ln1_g, ln1_b, ln2_g, ln2_b, ffn_w1, ffn_w2, moe_router, moe_w1, moe_w2):
    bp, seq, d_model = x_prompt.shape
    nb, dseq, _ = x_sample.shape
    depth = w_in.shape[0]
    past_len = cache_sb_k.shape[2]
    a_heads = state_hgrn.shape[2]
    c_heads = cache_sb_k.shape[3]
    a_width = a_heads * HEAD_DIM
    c_width = c_heads * HEAD_DIM
    b_width = state_pool.shape[3]
    ne = moe_router.shape[2]
    d_ff = ffn_w2.shape[1]
    assert bp == 1 and a_width == c_width
    alpha = (2 * depth) ** 0.25
    np_rows = bp * seq
    ns_rows = nb * dseq
    m = np_rows + ns_rows

    pool_cb = 4 * a_width // b_width
    cq = (4 * a_width + b_width) // HEAD_DIM
    ck = cq + c_heads
    cv = ck + c_heads
    k_col0 = ck * HEAD_DIM
    v_col0 = cv * HEAD_DIM
    u_col0 = 4 * a_width

    pl_soft = jax.nn.softmax(lb_param.astype(F32), axis=0)
    lb_all = (jnp.cumsum(pl_soft, axis=0) - pl_soft[0])[:, None, :]
    gnorm = hgrn_norm.reshape(depth, 1, a_width)
    pscale = pool_scale.reshape(depth, 1, b_width)
    vec = lambda a: a.reshape(depth, 1, d_model)
    ln1g, ln1b, ln2g, ln2b = vec(ln1_g), vec(ln1_b), vec(ln2_g), vec(ln2_b)
    router_p = jnp.pad(moe_router, ((0, 0), (0, 0), (0, HEAD_DIM - ne)))
    k_cache = cache_sb_k.reshape(depth, nb, past_len, c_width)
    v_cache = cache_sb_v.reshape(depth, nb, past_len, c_width)
    pool_hist = jnp.pad(state_pool, ((0, 0), (0, 0), (POOL_HALO - state_pool.shape[2], 0), (0, 0)))
    w_out2 = w_out.reshape(depth * d_model, d_model)
    ffn_w2_2 = ffn_w2.reshape(-1, d_model)
    moe_w2_2 = moe_w2.reshape(-1, d_model)

    x = jnp.concatenate([x_prompt.reshape(np_rows, d_model), x_sample.reshape(ns_rows, d_model)], axis=0)
    xb = x.astype(BF16)

    outs = {k: [] for k in ("ps", "pb", "pk", "pv", "ss", "sb", "sk", "sv")}
    for l in range(depth):
        p = matmul_layer(xb, w_in, l, tm=2112, tn=256)

        oa_p, st_p = hgrn_mixer(p, 0, np_rows, lb_all, gnorm, l, None, per_block_state=False, width=a_width)
        oa_s, st_s = hgrn_mixer(p, np_rows, ns_rows, lb_all, gnorm, l, state_hgrn[l],
                                per_block_state=True, width=a_width)
        ob_p = pool_stream(p, np_rows, pool_cb, w_pool, pscale, l, width=b_width, tile=512)
        u_s = p[np_rows:, u_col0:u_col0 + b_width].reshape(nb, dseq, b_width)
        ob_s = pool_batch(u_s, pool_hist[l], w_pool, pscale, l, start=past_len).reshape(ns_rows, b_width)
        oc_p = sb_stream(p, np_rows, cq, ck, cv, heads=c_heads, tq=256, tk=128)
        oc_s = sb_decode(p, np_rows, nb, dseq, cq, ck, cv, k_cache, v_cache, l, heads=c_heads, tk=128)

        mix_in = jnp.concatenate([jnp.concatenate([oa_p, ob_p, oc_p], axis=1),
                                  jnp.concatenate([oa_s, ob_s, oc_s], axis=1)], axis=0)
        h, hb = matmul_residual_ln(mix_in, w_out2, l * d_model, x, ln1g, ln1b, l, alpha=alpha, tm=1056, tk=256)

        if l % 2 == 0:
            mid = swiglu_up(hb, ffn_w1, l // 2, tm=1056, tn=512)
            x, xb = matmul_residual_ln(mid, ffn_w2_2, (l // 2) * d_ff, h, ln2g, ln2b, l,
                                       alpha=alpha, tm=1056, tk=256)
        else:
            comb = router_combine(h, router_p, l // 2, ne, tm=1056)
            mid = swiglu_up(hb, moe_w1, l // 2, comb, tm=1056, tn=512)
            x, xb = matmul_residual_ln(mid, moe_w2_2, (l // 2) * ne * d_ff, h, ln2g, ln2b, l,
                                       alpha=alpha, tm=1056, tk=256)

        outs["ps"].append(st_p)
        outs["pb"].append(p[np_rows - (POOL_HALO - 1):np_rows, u_col0:u_col0 + b_width][None])
        outs["pk"].append(p[:np_rows, k_col0:k_col0 + c_width].reshape(bp, seq, c_heads, HEAD_DIM))
        outs["pv"].append(p[:np_rows, v_col0:v_col0 + c_width].reshape(bp, seq, c_heads, HEAD_DIM))
        outs["ss"].append(st_s)
        outs["sb"].append(u_s[:, dseq - (POOL_HALO - 1):, :])
        outs["sk"].append(p[np_rows:, k_col0:k_col0 + c_width].reshape(nb, dseq, c_heads, HEAD_DIM))
        outs["sv"].append(p[np_rows:, v_col0:v_col0 + c_width].reshape(nb, dseq, c_heads, HEAD_DIM))

    stack = lambda k: jnp.stack(outs[k])
    return (x[:np_rows].reshape(bp, seq, d_model), x[np_rows:].reshape(nb, dseq, d_model),
            stack("ps"), stack("pb"), stack("pk"), stack("pv"),
            stack("ss"), stack("sb"), stack("sk"), stack("sv"))
```

```python
import functools

import jax
import jax.numpy as jnp
from jax import lax
from jax.experimental import pallas as pl
from jax.experimental.pallas import tpu as pltpu

F32 = jnp.float32
BF16 = jnp.bfloat16

HEAD_DIM = 128
HGRN_BLOCK_LOG2 = 4
HGRN_BLOCK = 1 << HGRN_BLOCK_LOG2
HGRN_TILE = 256
POOL_WINDOWS = (2, 4, 8, 16)
POOL_HALO = 16
TOP_K = 2
LN_EPS = 1e-5
RMS_EPS = 1e-6
LB_FLOOR = 1e-30
SB_DEAD_LOG = -104.0
VMEM_LIMIT_BYTES = 56 * 1024 * 1024


def _params(*sem):
    return pltpu.CompilerParams(dimension_semantics=sem, vmem_limit_bytes=VMEM_LIMIT_BYTES)


def _dot(a, b):
    return jnp.dot(a, b, preferred_element_type=F32)


def _dot_nt(a, b):
    return lax.dot_general(a, b, (((1,), (1,)), ((), ())), preferred_element_type=F32)


def _split3(x):
    h1 = x.astype(BF16)
    r1 = x - h1.astype(F32)
    h2 = r1.astype(BF16)
    h3 = (r1 - h2.astype(F32)).astype(BF16)
    return h1, h2, h3


def _sigmoid(x):
    return 1.0 / (1.0 + jnp.exp(-x))


def _log_sigmoid(x):
    return jnp.minimum(x, 0.0) - jnp.log1p(jnp.exp(-jnp.abs(x)))


def _mm_kernel(a_ref, w_ref, o_ref):
    o_ref[...] = _dot(a_ref[...], w_ref[...].astype(BF16)).astype(o_ref.dtype)


def matmul_layer(a, w, layer, *, tm, tn):
    m, k = a.shape
    n = w.shape[2]
    assert m % tm == 0 and n % tn == 0
    return pl.pallas_call(
        _mm_kernel,
        grid=(m // tm, n // tn),
        in_specs=[pl.BlockSpec((tm, k), lambda i, j: (i, 0)),
                  pl.BlockSpec((None, k, tn), lambda i, j: (layer, 0, j))],
        out_specs=pl.BlockSpec((tm, tn), lambda i, j: (i, j)),
        out_shape=jax.ShapeDtypeStruct((m, n), F32),
        compiler_params=_params("parallel", "arbitrary"),
        name="in_proj",
    )(a, w)


def _swiglu_kernel(x_ref, wg_ref, wu_ref, o_ref):
    x = x_ref[...]
    g = _dot(x, wg_ref[...].astype(BF16))
    u = _dot(x, wu_ref[...].astype(BF16))
    o_ref[...] = (g * _sigmoid(g) * u).astype(o_ref.dtype)


def swiglu_up(x, w1, layer, *, tm, tn):
    m, k = x.shape
    f = w1.shape[-1] // 2
    assert m % tm == 0 and f % tn == 0
    nj = f // tn
    return pl.pallas_call(
        _swiglu_kernel,
        grid=(m // tm, nj),
        in_specs=[pl.BlockSpec((tm, k), lambda i, j: (i, 0)),
                  pl.BlockSpec((None, k, tn), lambda i, j: (layer, 0, j)),
                  pl.BlockSpec((None, k, tn), lambda i, j: (layer, 0, nj + j))],
        out_specs=pl.BlockSpec((tm, tn), lambda i, j: (i, j)),
        out_shape=jax.ShapeDtypeStruct((m, f), BF16),
        compiler_params=_params("parallel", "arbitrary"),
        name="swiglu_up",
    )(x, w1, w1)


def _mm_ln_kernel(a_ref, w_ref, r_ref, g_ref, b_ref, of_ref, ob_ref, *, alpha, nk):
    kk = pl.program_id(1)

    @pl.when(kk == 0)
    def _():
        of_ref[...] = jnp.zeros_like(of_ref)

    of_ref[...] += _dot(a_ref[...], w_ref[...].astype(BF16))

    @pl.when(kk == nk - 1)
    def _():
        r = alpha * r_ref[...] + of_ref[...]
        mu = jnp.mean(r, axis=-1, keepdims=True)
        c = r - mu
        var = jnp.mean(c * c, axis=-1, keepdims=True)
        y = c * lax.rsqrt(var + LN_EPS) * g_ref[...] + b_ref[...]
        of_ref[...] = y
        ob_ref[...] = y.astype(BF16)


def matmul_residual_ln(a, w, w_row0, resid, g, b, layer, *, alpha, tm, tk):
    m, k = a.shape
    n = w.shape[1]
    assert m % tm == 0 and k % tk == 0 and w_row0 % tk == 0
    nk = k // tk
    kb0 = w_row0 // tk
    row = pl.BlockSpec((tm, n), lambda i, kk: (i, 0))
    vec = pl.BlockSpec((None, 1, n), lambda i, kk: (layer, 0, 0))
    return pl.pallas_call(
        functools.partial(_mm_ln_kernel, alpha=alpha, nk=nk),
        grid=(m // tm, nk),
        in_specs=[pl.BlockSpec((tm, tk), lambda i, kk: (i, kk)),
                  pl.BlockSpec((tk, n), lambda i, kk: (kb0 + kk, 0)),
                  row, vec, vec],
        out_specs=[row, row],
        out_shape=[jax.ShapeDtypeStruct((m, n), F32), jax.ShapeDtypeStruct((m, n), BF16)],
        compiler_params=_params("parallel", "arbitrary"),
        name="proj_ln",
    )(a, w, resid, g, b)


ROUTE_E1, ROUTE_E2, ROUTE_G1, ROUTE_G2, ROUTE_R1, ROUTE_R2 = range(6)


def _router_kernel(h_ref, r_ref, o_ref, cnt_ref, *, ne):
    step = pl.program_id(0)

    @pl.when(step == 0)
    def _():
        cnt_ref[...] = jnp.zeros_like(cnt_ref)

    logits = jnp.dot(h_ref[...], r_ref[...], preferred_element_type=F32,
                     precision=lax.Precision.HIGHEST)
    tm = logits.shape[0]
    lane = lax.broadcasted_iota(jnp.int32, logits.shape, 1)
    neg = jnp.float32(-jnp.inf)
    x = jnp.where(lane < ne, logits, neg)
    m1 = jnp.max(x, axis=-1, keepdims=True)
    i1 = jnp.min(jnp.where(x == m1, lane, ne), axis=-1, keepdims=True)
    x2 = jnp.where(lane == i1, neg, x)
    m2 = jnp.max(x2, axis=-1, keepdims=True)
    i2 = jnp.min(jnp.where(x2 == m2, lane, ne), axis=-1, keepdims=True)
    e2 = jnp.exp(m2 - m1)
    g1 = 1.0 / (1.0 + e2)
    g2 = e2 / (1.0 + e2)

    hit1 = lane == i1
    hit2 = lane == i2
    onehot = jnp.where(hit1, 1.0, 0.0) + jnp.where(hit2, 1.0, 0.0)
    t_i = lax.broadcasted_iota(jnp.int32, (tm, tm), 0)
    s_i = lax.broadcasted_iota(jnp.int32, (tm, tm), 1)
    before = jnp.where(s_i < t_i, 1.0, 0.0).astype(BF16)
    prior = _dot(before, onehot.astype(BF16)) + cnt_ref[...]
    r1 = jnp.sum(jnp.where(hit1, prior, 0.0), axis=-1, keepdims=True)
    r2 = jnp.sum(jnp.where(hit2, prior, 0.0), axis=-1, keepdims=True)
    cnt_ref[...] += jnp.sum(onehot, axis=0, keepdims=True)

    rec = jnp.zeros(logits.shape, F32)
    for ln, val in ((ROUTE_E1, i1.astype(F32)), (ROUTE_E2, i2.astype(F32)), (ROUTE_G1, g1),
                    (ROUTE_G2, g2), (ROUTE_R1, r1), (ROUTE_R2, r2)):
        rec = jnp.where(lane == ln, val, rec)
    o_ref[...] = rec


def router_top2(h, router_padded, layer, ne, *, tm):
    m, k = h.shape
    return pl.pallas_call(
        functools.partial(_router_kernel, ne=ne),
        grid=(m // tm,),
        in_specs=[pl.BlockSpec((tm, k), lambda i: (i, 0)),
                  pl.BlockSpec((None, k, HEAD_DIM), lambda i: (layer, 0, 0))],
        out_specs=[pl.BlockSpec((tm, HEAD_DIM), lambda i: (i, 0)),
                   pl.BlockSpec((1, HEAD_DIM), lambda i: (0, 0))],
        out_shape=[jax.ShapeDtypeStruct((m, HEAD_DIM), F32), jax.ShapeDtypeStruct((1, HEAD_DIM), F32)],
        compiler_params=_params("arbitrary"),
        name="router",
    )(h, router_padded)


MOE_BLOCK = 512


def _gather_rows_kernel(src_ref, h_any, o_ref, buf, sem):
    nrow = buf.shape[0]

    def start(r, c):
        pltpu.make_async_copy(h_any.at[pl.ds(src_ref[0, r], 1)], buf.at[pl.ds(r, 1)], sem).start()
        return c

    lax.fori_loop(0, nrow, start, 0, unroll=8)

    def wait(r, c):
        pltpu.make_async_copy(h_any.at[pl.ds(0, 1)], buf.at[pl.ds(r, 1)], sem).wait()
        return c

    lax.fori_loop(0, nrow, wait, 0, unroll=8)
    o_ref[...] = buf[...].astype(o_ref.dtype)


def gather_rows(h, src):
    nblk = src.shape[0]
    d = h.shape[1]
    return pl.pallas_call(
        _gather_rows_kernel,
        grid=(nblk,),
        in_specs=[pl.BlockSpec((None, 1, MOE_BLOCK), lambda i: (i, 0, 0), memory_space=pltpu.SMEM),
                  pl.BlockSpec(memory_space=pl.ANY)],
        out_specs=pl.BlockSpec((MOE_BLOCK, d), lambda i: (i, 0)),
        out_shape=jax.ShapeDtypeStruct((nblk * MOE_BLOCK, d), BF16),
        scratch_shapes=[pltpu.VMEM((MOE_BLOCK, d), F32), pltpu.SemaphoreType.DMA(())],
        compiler_params=_params("arbitrary"),
        name="moe_gather",
    )(src, h)


def _grouped_swiglu_kernel(ie_ref, n_ref, x_ref, wg_ref, wu_ref, o_ref):
    it = pl.program_id(1)

    @pl.when(it < n_ref[0])
    def _():
        x = x_ref[...]
        g = _dot(x, wg_ref[...].astype(BF16))
        u = _dot(x, wu_ref[...].astype(BF16))
        o_ref[...] = (g * _sigmoid(g) * u).astype(o_ref.dtype)

    @pl.when(it >= n_ref[0])
    def _():
        o_ref[...] = jnp.zeros_like(o_ref)


def grouped_swiglu(xs, w1, layer, item_expert, n_items, *, tn):
    rows, k = xs.shape
    nblk = rows // MOE_BLOCK
    f = w1.shape[-1] // 2
    nj = f // tn
    grid_spec = pltpu.PrefetchScalarGridSpec(
        num_scalar_prefetch=2,
        grid=(nj, nblk),
        in_specs=[pl.BlockSpec((MOE_BLOCK, k), lambda j, it, ie, n: (it, 0)),
                  pl.BlockSpec((None, None, k, tn), lambda j, it, ie, n: (layer, ie[it], 0, j)),
                  pl.BlockSpec((None, None, k, tn), lambda j, it, ie, n: (layer, ie[it], 0, nj + j))],
        out_specs=pl.BlockSpec((MOE_BLOCK, tn), lambda j, it, ie, n: (it, j)),
    )
    return pl.pallas_call(
        _grouped_swiglu_kernel,
        grid_spec=grid_spec,
        out_shape=jax.ShapeDtypeStruct((rows, f), BF16),
        compiler_params=_params("arbitrary", "arbitrary"),
        name="moe_up",
    )(item_expert, n_items, xs, w1, w1)


def _grouped_down_kernel(ie_ref, n_ref, a_ref, w_ref, o_ref, *, nk):
    it = pl.program_id(0)
    kk = pl.program_id(1)

    @pl.when(kk == 0)
    def _():
        o_ref[...] = jnp.zeros_like(o_ref)

    @pl.when(it < n_ref[0])
    def _():
        o_ref[...] += _dot(a_ref[...], w_ref[...].astype(BF16))


def grouped_down(mid, w2, layer, item_expert, n_items, *, tk):
    rows, f = mid.shape
    nblk = rows // MOE_BLOCK
    d = w2.shape[-1]
    nk = f // tk

    def kidx(it, kk, n):
        return jnp.where(it < n[0], kk, nk - 1)

    grid_spec = pltpu.PrefetchScalarGridSpec(
        num_scalar_prefetch=2,
        grid=(nblk, nk),
        in_specs=[pl.BlockSpec((MOE_BLOCK, tk), lambda it, kk, ie, n: (it, kidx(it, kk, n))),
                  pl.BlockSpec((None, None, tk, d), lambda it, kk, ie, n: (layer, ie[it], kidx(it, kk, n), 0))],
        out_specs=pl.BlockSpec((MOE_BLOCK, d), lambda it, kk, ie, n: (it, 0)),
    )
    return pl.pallas_call(
        functools.partial(_grouped_down_kernel, nk=nk),
        grid_spec=grid_spec,
        out_shape=jax.ShapeDtypeStruct((rows, d), F32),
        compiler_params=_params("arbitrary", "arbitrary"),
        name="moe_down",
    )(item_expert, n_items, mid, w2)


def _combine_ln_kernel(dst_ref, y_any, route_ref, r_ref, g_ref, b_ref, of_ref, ob_ref, buf, sem, *, alpha):
    tm = r_ref.shape[0]

    def start(i, c):
        for slot in range(TOP_K):
            pltpu.make_async_copy(y_any.at[pl.ds(dst_ref[0, slot * tm + i], 1)],
                                  buf.at[slot, pl.ds(i, 1)], sem).start()
        return c

    lax.fori_loop(0, tm, start, 0, unroll=8)

    def wait(i, c):
        for slot in range(TOP_K):
            pltpu.make_async_copy(y_any.at[pl.ds(0, 1)], buf.at[slot, pl.ds(i, 1)], sem).wait()
        return c

    lax.fori_loop(0, tm, wait, 0, unroll=8)
    route = route_ref[...]
    g1 = route[:, ROUTE_G1:ROUTE_G1 + 1]
    g2 = route[:, ROUTE_G2:ROUTE_G2 + 1]
    r = alpha * r_ref[...] + (g1 * buf[0] + g2 * buf[1])
    mu = jnp.mean(r, axis=-1, keepdims=True)
    c = r - mu
    var = jnp.mean(c * c, axis=-1, keepdims=True)
    y = c * lax.rsqrt(var + LN_EPS) * g_ref[...] + b_ref[...]
    of_ref[...] = y
    ob_ref[...] = y.astype(BF16)


def combine_ln(ys, dst, route, resid, g, b, layer, *, alpha, tm):
    m, d = resid.shape
    row = pl.BlockSpec((tm, d), lambda i: (i, 0))
    vec = pl.BlockSpec((None, 1, d), lambda i: (layer, 0, 0))
    return pl.pallas_call(
        functools.partial(_combine_ln_kernel, alpha=alpha),
        grid=(m // tm,),
        in_specs=[pl.BlockSpec((None, 1, TOP_K * tm), lambda i: (i, 0, 0), memory_space=pltpu.SMEM),
                  pl.BlockSpec(memory_space=pl.ANY),
                  pl.BlockSpec((tm, HEAD_DIM), lambda i: (i, 0)),
                  row, vec, vec],
        out_specs=[row, row],
        out_shape=[jax.ShapeDtypeStruct((m, d), F32), jax.ShapeDtypeStruct((m, d), BF16)],
        scratch_shapes=[pltpu.VMEM((TOP_K, tm, d), F32), pltpu.SemaphoreType.DMA(())],
        compiler_params=_params("arbitrary"),
        name="moe_combine_ln",
    )(dst, ys, route, resid, g, b)


def routed_ffn(h, router_padded, w1, w2, layer, g, b, ln_layer, ne, *, alpha, tm_route, tm_comb, tf):
    m, d = h.shape
    route, counts = router_top2(h, router_padded, layer, ne, tm=tm_route)
    cnt = counts[0, :ne].astype(jnp.int32)
    nblk_e = (cnt + MOE_BLOCK - 1) // MOE_BLOCK
    blk_end = jnp.cumsum(nblk_e)
    row_off = (blk_end - nblk_e) * MOE_BLOCK
    n_items = blk_end[-1:]
    max_items = (TOP_K * m + MOE_BLOCK - 1) // MOE_BLOCK + ne
    item = jnp.arange(max_items, dtype=jnp.int32)
    item_expert = jnp.minimum(jnp.searchsorted(blk_end, item, side="right"), ne - 1).astype(jnp.int32)
    last_expert = item_expert[jnp.maximum(n_items[0] - 1, 0)]
    item_expert = jnp.where(item < n_items[0], item_expert, last_expert)
    e1 = route[:, ROUTE_E1].astype(jnp.int32)
    e2 = route[:, ROUTE_E2].astype(jnp.int32)
    dst1 = row_off[e1] + route[:, ROUTE_R1].astype(jnp.int32)
    dst2 = row_off[e2] + route[:, ROUTE_R2].astype(jnp.int32)
    tok = jnp.arange(m, dtype=jnp.int32)
    src = jnp.zeros((max_items * MOE_BLOCK,), jnp.int32).at[dst1].set(tok).at[dst2].set(tok)
    dst = jnp.concatenate([dst1.reshape(m // tm_comb, 1, tm_comb), dst2.reshape(m // tm_comb, 1, tm_comb)], axis=2)

    xs = gather_rows(h, src.reshape(max_items, 1, MOE_BLOCK))
    mid = grouped_swiglu(xs, w1, layer, item_expert, n_items, tn=tf)
    ys = grouped_down(mid, w2, layer, item_expert, n_items, tk=tf)
    return combine_ln(ys, dst, route, h, g, b, ln_layer, alpha=alpha, tm=tm_comb)


def _hgrn_kernel(q_ref, z_ref, v_ref, g_ref, lb_ref, gn_ref, s0_ref, o_ref, sout_ref,
                 st_s, qt_s, kd_s, vt_s, ebl_s, acc_s, *, heads, per_block_state):
    rows, width = q_ref.shape
    nblk = rows // HGRN_BLOCK
    step = pl.program_id(0)

    lb = lb_ref[...]
    log_lb = jnp.log(jnp.maximum(lb, LB_FLOOR))
    z = z_ref[...]
    lo = jnp.log1p(-lb) + _log_sigmoid(z)
    logf = jnp.maximum(log_lb, lo) + jnp.log1p(jnp.exp(-jnp.abs(log_lb - lo)))
    kin = (1.0 - lb) * _sigmoid(-z)
    qraw = q_ref[...]
    q = qraw * _sigmoid(qraw)
    v = v_ref[...]

    r_i = lax.broadcasted_iota(jnp.int32, (rows, rows), 0)
    c_i = lax.broadcasted_iota(jnp.int32, (rows, rows), 1)
    same = (r_i >> HGRN_BLOCK_LOG2) == (c_i >> HGRN_BLOCK_LOG2)
    tri = jnp.where(same & (c_i <= r_i), 1.0, 0.0).astype(BF16)
    blk = jnp.where(same, 1.0, 0.0).astype(BF16)
    parts = _split3(logf)
    b = _dot(tri, parts[0]) + _dot(tri, parts[1]) + _dot(tri, parts[2])
    btot = _dot(blk, parts[0]) + _dot(blk, parts[1]) + _dot(blk, parts[2])
    qt_s[...] = q * jnp.exp(b)
    kd_s[...] = kin * jnp.exp(btot - b)
    ebl_s[...] = jnp.exp(btot)
    vt_s[...] = v.T

    pos = lax.broadcasted_iota(jnp.int32, (rows, width), 0) & (HGRN_BLOCK - 1)
    acc = None
    for d in range(HGRN_BLOCK):
        if d == 0:
            w = q * kin
            vs = v
        else:
            decay = jnp.exp(b - pltpu.roll(b, d, 0))
            w = jnp.where(pos >= d, q * pltpu.roll(kin, d, 0) * decay, 0.0)
            vs = pltpu.roll(v, d, 0)
        att = jnp.concatenate(
            [jnp.broadcast_to(jnp.sum(w[:, h * HEAD_DIM:(h + 1) * HEAD_DIM], axis=1, keepdims=True),
                              (rows, HEAD_DIM)) for h in range(heads)], axis=1)
        acc = att * vs if acc is None else acc + att * vs
    acc_s[...] = acc

    if not per_block_state:
        @pl.when(step == 0)
        def _():
            st_s[...] = jnp.zeros_like(st_s)

    lane_blk = lax.broadcasted_iota(jnp.int32, (HEAD_DIM, rows), 1) >> HGRN_BLOCK_LOG2

    def block_step(c, carry):
        r0 = pl.multiple_of(c * HGRN_BLOCK, HGRN_BLOCK)
        for h in range(heads):
            hc = slice(h * HEAD_DIM, (h + 1) * HEAD_DIM)
            if per_block_state:
                st = s0_ref[c, h].T
            else:
                st = st_s[h]
            qc = qt_s[pl.ds(r0, HGRN_BLOCK), hc]
            acc_s[pl.ds(r0, HGRN_BLOCK), hc] += _dot_nt(qc.astype(BF16), st.astype(BF16))
            vmask = jnp.where(lane_blk == c, vt_s[hc, :], 0.0).astype(BF16)
            upd = _dot(vmask, kd_s[:, hc].astype(BF16))
            st_new = st * ebl_s[pl.ds(r0, 1), hc] + upd
            if per_block_state:
                sout_ref[c, h] = st_new.T
            else:
                st_s[h] = st_new
        return carry

    lax.fori_loop(0, nblk, block_step, 0)

    if not per_block_state:
        @pl.when(step == pl.num_programs(0) - 1)
        def _():
            for h in range(heads):
                sout_ref[0, h] = st_s[h].T

    o = acc_s[...]
    ms = jnp.concatenate(
        [jnp.broadcast_to(jnp.sum(jnp.square(o[:, h * HEAD_DIM:(h + 1) * HEAD_DIM]), axis=1, keepdims=True),
                          (rows, HEAD_DIM)) for h in range(heads)], axis=1) * (1.0 / HEAD_DIM)
    gate = g_ref[...]
    o_ref[...] = (o * lax.rsqrt(ms + RMS_EPS) * gn_ref[...] * (gate * _sigmoid(gate))).astype(o_ref.dtype)


def hgrn_mixer(p, row0, nrows, lb, gnorm, layer, s0, *, per_block_state, width):
    heads = width // HEAD_DIM
    tile = HGRN_TILE
    assert nrows % tile == 0 and row0 % tile == 0
    rb0 = row0 // tile
    nt = nrows // tile
    if per_block_state:
        assert nt == 1
        nstate = s0.shape[0]
    else:
        nstate = 1
        s0 = jnp.zeros((1, heads, HEAD_DIM, HEAD_DIM), F32)

    def col(cb):
        return pl.BlockSpec((tile, width), lambda i: (rb0 + i, cb))

    vec = pl.BlockSpec((None, 1, width), lambda i: (layer, 0, 0))
    state = pl.BlockSpec((nstate, heads, HEAD_DIM, HEAD_DIM), lambda i: (0, 0, 0, 0))
    return pl.pallas_call(
        functools.partial(_hgrn_kernel, heads=heads, per_block_state=per_block_state),
        grid=(nt,),
        in_specs=[col(0), col(1), col(2), col(3), vec, vec, state],
        out_specs=[pl.BlockSpec((tile, width), lambda i: (i, 0)), state],
        out_shape=[jax.ShapeDtypeStruct((nrows, width), BF16),
                   jax.ShapeDtypeStruct((nstate, heads, HEAD_DIM, HEAD_DIM), F32)],
        scratch_shapes=[pltpu.VMEM((heads, HEAD_DIM, HEAD_DIM), F32),
                        pltpu.VMEM((tile, width), F32),
                        pltpu.VMEM((tile, width), F32),
                        pltpu.VMEM((width, tile), F32),
                        pltpu.VMEM((tile, width), F32),
                        pltpu.VMEM((tile, width), F32)],
        compiler_params=_params("arbitrary"),
        name="hgrn2",
    )(p, p, p, p, lb, gnorm, s0)


def _pool_tile(xp, pos, w_ref, scale):
    sums = {1: xp}
    s = xp
    w = 1
    while w < POOL_WINDOWS[-1]:
        s = s + pltpu.roll(s, w, 0)
        w *= 2
        sums[w] = s
    gd = HEAD_DIM
    outs = []
    for g, win in enumerate(POOL_WINDOWS):
        sl = slice(g * gd, (g + 1) * gd)
        cnt = jnp.clip(pos + 1, 1, win).astype(F32)
        d = sums[win][:, sl] / cnt - xp[:, sl]
        outs.append(_dot(d.astype(BF16), w_ref[g].astype(BF16)))
    return jnp.concatenate(outs, axis=1) * scale


def _pool_stream_kernel(u_ref, w_ref, sc_ref, o_ref, xp_s):
    step = pl.program_id(0)
    rows = u_ref.shape[0]

    @pl.when(step == 0)
    def _():
        xp_s[0:POOL_HALO, :] = jnp.zeros((POOL_HALO, xp_s.shape[1]), F32)

    xp_s[POOL_HALO:, :] = u_ref[...]
    xp = xp_s[...]
    pos = step * rows - POOL_HALO + lax.broadcasted_iota(jnp.int32, (rows + POOL_HALO, 1), 0)
    y = _pool_tile(xp, pos, w_ref, sc_ref[...])
    o_ref[...] = y[POOL_HALO:, :].astype(o_ref.dtype)
    xp_s[0:POOL_HALO, :] = xp[rows:, :]


def _pool_batch_kernel(u_ref, buf_ref, w_ref, sc_ref, o_ref, xp_s, *, start):
    nb, t, c = u_ref.shape
    seg = POOL_HALO + t
    for bi in range(nb):
        xp_s[bi * seg:bi * seg + POOL_HALO, :] = buf_ref[bi]
        xp_s[bi * seg + POOL_HALO:(bi + 1) * seg, :] = u_ref[bi]
    r = lax.broadcasted_iota(jnp.int32, (nb * seg, 1), 0) & (seg - 1)
    y = _pool_tile(xp_s[...], start - POOL_HALO + r, w_ref, sc_ref[...])
    for bi in range(nb):
        o_ref[bi] = y[bi * seg + POOL_HALO:(bi + 1) * seg, :].astype(o_ref.dtype)


def pool_stream(p, nrows, col_block, w_pool, pool_scale, layer, *, width, tile):
    assert nrows % tile == 0
    return pl.pallas_call(
        _pool_stream_kernel,
        grid=(nrows // tile,),
        in_specs=[pl.BlockSpec((tile, width), lambda i: (i, col_block)),
                  pl.BlockSpec((None,) + w_pool.shape[1:], lambda i: (layer, 0, 0, 0)),
                  pl.BlockSpec((None, 1, width), lambda i: (layer, 0, 0))],
        out_specs=pl.BlockSpec((tile, width), lambda i: (i, 0)),
        out_shape=jax.ShapeDtypeStruct((nrows, width), BF16),
        scratch_shapes=[pltpu.VMEM((tile + POOL_HALO, width), F32)],
        compiler_params=_params("arbitrary"),
        name="pool_stream",
    )(p, w_pool, pool_scale)


def pool_batch(u, buf, w_pool, pool_scale, layer, *, start):
    nb, t, c = u.shape
    return pl.pallas_call(
        functools.partial(_pool_batch_kernel, start=start),
        grid=(1,),
        in_specs=[pl.BlockSpec((nb, t, c), lambda i: (0, 0, 0)),
                  pl.BlockSpec((nb, POOL_HALO, c), lambda i: (0, 0, 0)),
                  pl.BlockSpec((None,) + w_pool.shape[1:], lambda i: (layer, 0, 0, 0)),
                  pl.BlockSpec((None, 1, c), lambda i: (layer, 0, 0))],
        out_specs=pl.BlockSpec((nb, t, c), lambda i: (0, 0, 0)),
        out_shape=jax.ShapeDtypeStruct((nb, t, c), BF16),
        scratch_shapes=[pltpu.VMEM((nb * (POOL_HALO + t), c), F32)],
        compiler_params=_params("arbitrary"),
        name="pool_batch",
    )(u, buf, w_pool, pool_scale)


def _sb_block(qb, kblk, vblk, mask, r_run, upper, scale):
    z = _dot_nt(qb, kblk.astype(BF16)) * scale
    l1m = _log_sigmoid(-z)
    if mask is not None:
        l1m = jnp.where(mask, l1m, 0.0)
    parts = _split3(l1m)
    later = _dot(parts[0], upper) + _dot(parts[1], upper) + _dot(parts[2], upper)
    a = jnp.exp(z + l1m + later + r_run)
    if mask is not None:
        a = jnp.where(mask, a, 0.0)
    return _dot(a.astype(BF16), vblk.astype(BF16)), jnp.sum(l1m, axis=1, keepdims=True)


def _strict_upper(tk):
    j = lax.broadcasted_iota(jnp.int32, (tk, tk), 0)
    s = lax.broadcasted_iota(jnp.int32, (tk, tk), 1)
    return jnp.where(j > s, 1.0, 0.0).astype(BF16)


def _sb_stream_kernel(q_ref, k_ref, v_ref, o_ref, acc_s, run_s, *, tk, scale):
    tq = q_ref.shape[0]
    qi = pl.program_id(1)
    qb = q_ref[...].astype(BF16)
    upper = _strict_upper(tk)
    acc_s[...] = jnp.zeros_like(acc_s)
    run_s[...] = jnp.zeros_like(run_s)
    qpos = qi * tq + lax.broadcasted_iota(jnp.int32, (tq, tk), 0)
    col = lax.broadcasted_iota(jnp.int32, (tq, tk), 1)

    def cond(carry):
        kb, live = carry
        return jnp.logical_and(kb >= 0, live > SB_DEAD_LOG)

    def body(carry):
        kb, _ = carry
        k0 = pl.multiple_of(kb * tk, tk)
        mask = (k0 + col) < qpos
        r_run = run_s[...]
        contrib, rowsum = _sb_block(qb, k_ref[pl.ds(k0, tk), :], v_ref[pl.ds(k0, tk), :],
                                    mask, r_run, upper, scale)
        acc_s[...] += contrib
        r_new = r_run + rowsum
        run_s[...] = r_new
        return kb - 1, jnp.max(r_new)

    lax.while_loop(cond, body, ((qi + 1) * (tq // tk) - 1, jnp.float32(0.0)))
    o_ref[...] = acc_s[...].astype(o_ref.dtype)


def sb_stream(p, nrows, col_q, col_k, col_v, *, heads, tq, tk):
    assert nrows % tq == 0 and tq % tk == 0
    scale = HEAD_DIM ** -0.5
    return pl.pallas_call(
        functools.partial(_sb_stream_kernel, tk=tk, scale=scale),
        grid=(heads, nrows // tq),
        in_specs=[pl.BlockSpec((tq, HEAD_DIM), lambda h, i: (i, col_q + h)),
                  pl.BlockSpec((nrows, HEAD_DIM), lambda h, i: (0, col_k + h)),
                  pl.BlockSpec((nrows, HEAD_DIM), lambda h, i: (0, col_v + h))],
        out_specs=pl.BlockSpec((tq, HEAD_DIM), lambda h, i: (i, h)),
        out_shape=jax.ShapeDtypeStruct((nrows, heads * HEAD_DIM), BF16),
        scratch_shapes=[pltpu.VMEM((tq, HEAD_DIM), F32), pltpu.VMEM((tq, 1), F32)],
        compiler_params=_params("parallel", "arbitrary"),
        name="sb_stream",
    )(p, p, p)


def _sb_decode_kernel(*refs, heads, tk, scale):
    npair = heads // 2
    q_refs = refs[0:npair]
    kn_refs = refs[npair:2 * npair]
    vn_refs = refs[2 * npair:3 * npair]
    kp_ref, vp_ref, o_ref, kpad_s, vpad_s, acc_s, run_s, live_s = refs[3 * npair:]
    t = o_ref.shape[0]
    tp = kp_ref.shape[0]
    step = pl.program_id(1)
    upper = _strict_upper(tk)

    def head_cols(pair_refs, h):
        return pair_refs[h // 2][:, (h % 2) * HEAD_DIM:(h % 2 + 1) * HEAD_DIM]

    qbs = [head_cols(q_refs, h).astype(BF16) for h in range(heads)]

    @pl.when(step == 0)
    def _():
        kpad_s[...] = jnp.zeros_like(kpad_s)
        vpad_s[...] = jnp.zeros_like(vpad_s)
        row = lax.broadcasted_iota(jnp.int32, (t, tk), 0)
        col = lax.broadcasted_iota(jnp.int32, (t, tk), 1)
        live = None
        for h in range(heads):
            kpad_s[h, 0:t, :] = head_cols(kn_refs, h)
            vpad_s[h, 0:t, :] = head_cols(vn_refs, h)
            contrib, rowsum = _sb_block(qbs[h], kpad_s[h], vpad_s[h], col < row,
                                        jnp.zeros((t, 1), F32), upper, scale)
            acc_s[h] = contrib
            run_s[h] = rowsum
            m = jnp.max(rowsum)
            live = m if live is None else jnp.maximum(live, m)
        live_s[0] = live

    @pl.when(live_s[0] > SB_DEAD_LOG)
    def _():
        live = None
        for h in range(heads):
            r_run = run_s[h]
            acc = acc_s[h]
            for j in reversed(range(tp // tk)):
                contrib, rowsum = _sb_block(qbs[h], kp_ref[j * tk:(j + 1) * tk, h, :],
                                            vp_ref[j * tk:(j + 1) * tk, h, :], None, r_run, upper, scale)
                acc = acc + contrib
                r_run = r_run + rowsum
            acc_s[h] = acc
            run_s[h] = r_run
            m = jnp.max(r_run)
            live = m if live is None else jnp.maximum(live, m)
        live_s[0] = live

    @pl.when(step == pl.num_programs(1) - 1)
    def _():
        for h in range(heads):
            o_ref[:, h * HEAD_DIM:(h + 1) * HEAD_DIM] = acc_s[h].astype(o_ref.dtype)


def sb_decode(p, row0, nb, t, col_q, col_k, col_v, k_past, v_past, layer, *, heads, tk, tp):
    npast = k_past.shape[2]
    assert row0 % t == 0 and npast % tp == 0 and tp % tk == 0 and t <= tk and heads % 2 == 0
    assert col_q % 2 == 0 and col_k % 2 == 0 and col_v % 2 == 0
    rb0 = row0 // t
    nchunk = npast // tp
    scale = HEAD_DIM ** -0.5

    def new(cb):
        return [pl.BlockSpec((t, 2 * HEAD_DIM), lambda b, c, j=j: (rb0 + b, cb // 2 + j)) for j in range(heads // 2)]

    past = pl.BlockSpec((None, None, tp, heads, HEAD_DIM), lambda b, c: (layer, b, nchunk - 1 - c, 0, 0))
    npair = heads // 2
    return pl.pallas_call(
        functools.partial(_sb_decode_kernel, heads=heads, tk=tk, scale=scale),
        grid=(nb, nchunk),
        in_specs=new(col_q) + new(col_k) + new(col_v) + [past, past],
        out_specs=pl.BlockSpec((t, heads * HEAD_DIM), lambda b, c: (b, 0)),
        out_shape=jax.ShapeDtypeStruct((nb * t, heads * HEAD_DIM), BF16),
        scratch_shapes=[pltpu.VMEM((heads, tk, HEAD_DIM), F32), pltpu.VMEM((heads, tk, HEAD_DIM), F32),
                        pltpu.VMEM((heads, t, HEAD_DIM), F32), pltpu.VMEM((heads, t, 1), F32),
                        pltpu.SMEM((1,), F32)],
        compiler_params=_params("parallel", "arbitrary"),
        name="sb_decode",
    )(*([p] * (3 * npair)), k_past, v_past)


def kernel(x_prompt, x_sample, cache_sb_k, cache_sb_v, state_hgrn, state_pool, w_in, lb_param, hgrn_norm,
           w_pool, pool_scale, w_out, ln1_g, ln1_b, ln2_g, ln2_b, ffn_w1, ffn_w2, moe_router, moe_w1, moe_w2):
    bp, seq, d_model = x_prompt.shape
    nb, dseq, _ = x_sample.shape
    depth = w_in.shape[0]
    past_len = cache_sb_k.shape[2]
    a_heads = state_hgrn.shape[2]
    c_heads = cache_sb_k.shape[3]
    a_width = a_heads * HEAD_DIM
    c_width = c_heads * HEAD_DIM
    b_width = state_pool.shape[3]
    ne = moe_router.shape[2]
    d_ff = ffn_w2.shape[1]
    assert bp == 1 and a_width == c_width
    alpha = (2 * depth) ** 0.25
    np_rows = bp * seq
    ns_rows = nb * dseq
    m = np_rows + ns_rows

    pool_cb = 4 * a_width // b_width
    cq = (4 * a_width + b_width) // HEAD_DIM
    ck = cq + c_heads
    cv = ck + c_heads
    k_col0 = ck * HEAD_DIM
    v_col0 = cv * HEAD_DIM
    u_col0 = 4 * a_width

    pl_soft = jax.nn.softmax(lb_param.astype(F32), axis=0)
    lb_all = (jnp.cumsum(pl_soft, axis=0) - pl_soft[0])[:, None, :]
    gnorm = hgrn_norm.reshape(depth, 1, a_width)
    pscale = pool_scale.reshape(depth, 1, b_width)
    vec = lambda a: a.reshape(depth, 1, d_model)
    ln1g, ln1b, ln2g, ln2b = vec(ln1_g), vec(ln1_b), vec(ln2_g), vec(ln2_b)
    router_p = jnp.pad(moe_router, ((0, 0), (0, 0), (0, HEAD_DIM - ne)))
    pool_hist = jnp.pad(state_pool, ((0, 0), (0, 0), (POOL_HALO - state_pool.shape[2], 0), (0, 0)))
    w_out2 = w_out.reshape(depth * d_model, d_model)
    ffn_w2_2 = ffn_w2.reshape(-1, d_model)

    x = jnp.concatenate([x_prompt.reshape(np_rows, d_model), x_sample.reshape(ns_rows, d_model)], axis=0)
    xb = x.astype(BF16)

    outs = {k: [] for k in ("ps", "pb", "pk", "pv", "ss", "sb", "sk", "sv")}
    for l in range(depth):
        p = matmul_layer(xb, w_in, l, tm=2112, tn=256)

        oa_p, st_p = hgrn_mixer(p, 0, np_rows, lb_all, gnorm, l, None, per_block_state=False, width=a_width)
        oa_s, st_s = hgrn_mixer(p, np_rows, ns_rows, lb_all, gnorm, l, state_hgrn[l],
                                per_block_state=True, width=a_width)
        ob_p = pool_stream(p, np_rows, pool_cb, w_pool, pscale, l, width=b_width, tile=512)
        u_s = p[np_rows:, u_col0:u_col0 + b_width].reshape(nb, dseq, b_width)
        ob_s = pool_batch(u_s, pool_hist[l], w_pool, pscale, l, start=past_len).reshape(ns_rows, b_width)
        oc_p = sb_stream(p, np_rows, cq, ck, cv, heads=c_heads, tq=256, tk=128)
        oc_s = sb_decode(p, np_rows, nb, dseq, cq, ck, cv, cache_sb_k, cache_sb_v, l,
                         heads=c_heads, tk=128, tp=512)

        mix_in = jnp.concatenate([jnp.concatenate([oa_p, ob_p, oc_p], axis=1),
                                  jnp.concatenate([oa_s, ob_s, oc_s], axis=1)], axis=0)
        h, hb = matmul_residual_ln(mix_in, w_out2, l * d_model, x, ln1g, ln1b, l, alpha=alpha, tm=1056, tk=256)

        if l % 2 == 0:
            mid = swiglu_up(hb, ffn_w1, l // 2, tm=1056, tn=512)
            x, xb = matmul_residual_ln(mid, ffn_w2_2, (l // 2) * d_ff, h, ln2g, ln2b, l,
                                       alpha=alpha, tm=1056, tk=256)
        else:
            x, xb = routed_ffn(h, router_p, moe_w1, moe_w2, l // 2, ln2g, ln2b, l, ne, alpha=alpha,
                               tm_route=528, tm_comb=256, tf=512)

        outs["ps"].append(st_p)
        outs["pb"].append(p[np_rows - (POOL_HALO - 1):np_rows, u_col0:u_col0 + b_width][None])
        outs["pk"].append(p[:np_rows, k_col0:k_col0 + c_width].reshape(bp, seq, c_heads, HEAD_DIM))
        outs["pv"].append(p[:np_rows, v_col0:v_col0 + c_width].reshape(bp, seq, c_heads, HEAD_DIM))
        outs["ss"].append(st_s)
        outs["sb"].append(u_s[:, dseq - (POOL_HALO - 1):, :])
        outs["sk"].append(p[np_rows:, k_col0:k_col0 + c_width].reshape(nb, dseq, c_heads, HEAD_DIM))
        outs["sv"].append(p[np_rows:, v_col0:v_col0 + c_width].reshape(nb, dseq, c_heads, HEAD_DIM))

    stack = lambda k: jnp.stack(outs[k])
    return (x[:np_rows].reshape(bp, seq, d_model), x[np_rows:].reshape(nb, dseq, d_model),
            stack("ps"), stack("pb"), stack("pk"), stack("pv"),
            stack("ss"), stack("sb"), stack("sk"), stack("sv"))
```

```python
import functools

import jax
import jax.numpy as jnp
from jax import lax
from jax.experimental import pallas as pl
from jax.experimental.pallas import tpu as pltpu

F32 = jnp.float32
BF16 = jnp.bfloat16

HEAD_DIM = 128
HGRN_BLOCK_LOG2 = 4
HGRN_BLOCK = 1 << HGRN_BLOCK_LOG2
HGRN_TILE = 256
POOL_WINDOWS = (2, 4, 8, 16)
POOL_HALO = 16
TOP_K = 2
LN_EPS = 1e-5
RMS_EPS = 1e-6
LB_FLOOR = 1e-30
SB_DEAD_LOG = -104.0
VMEM_LIMIT_BYTES = 56 * 1024 * 1024


def _params(*sem):
    return pltpu.CompilerParams(dimension_semantics=sem, vmem_limit_bytes=VMEM_LIMIT_BYTES)


def _dot(a, b):
    return jnp.dot(a, b, preferred_element_type=F32)


def _dot_nt(a, b):
    return lax.dot_general(a, b, (((1,), (1,)), ((), ())), preferred_element_type=F32)


def _split3(x):
    h1 = x.astype(BF16)
    r1 = x - h1.astype(F32)
    h2 = r1.astype(BF16)
    h3 = (r1 - h2.astype(F32)).astype(BF16)
    return h1, h2, h3


def _sigmoid(x):
    return 1.0 / (1.0 + jnp.exp(-x))


def _log_sigmoid(x):
    return jnp.minimum(x, 0.0) - jnp.log1p(jnp.exp(-jnp.abs(x)))


def _mm_kernel(a_ref, w_ref, o_ref):
    o_ref[...] = _dot(a_ref[...], w_ref[...].astype(BF16)).astype(o_ref.dtype)


def matmul_layer(a, w, layer, *, tm, tn):
    m, k = a.shape
    n = w.shape[2]
    assert m % tm == 0 and n % tn == 0
    return pl.pallas_call(
        _mm_kernel,
        grid=(m // tm, n // tn),
        in_specs=[pl.BlockSpec((tm, k), lambda i, j: (i, 0)),
                  pl.BlockSpec((None, k, tn), lambda i, j: (layer, 0, j))],
        out_specs=pl.BlockSpec((tm, tn), lambda i, j: (i, j)),
        out_shape=jax.ShapeDtypeStruct((m, n), F32),
        compiler_params=_params("parallel", "arbitrary"),
        name="in_proj",
    )(a, w)


def _swiglu_kernel(x_ref, wg_ref, wu_ref, o_ref):
    x = x_ref[...]
    g = _dot(x, wg_ref[...].astype(BF16))
    u = _dot(x, wu_ref[...].astype(BF16))
    o_ref[...] = (g * _sigmoid(g) * u).astype(o_ref.dtype)


def swiglu_up(x, w1, layer, *, tm, tn):
    m, k = x.shape
    f = w1.shape[-1] // 2
    assert m % tm == 0 and f % tn == 0
    nj = f // tn
    return pl.pallas_call(
        _swiglu_kernel,
        grid=(m // tm, nj),
        in_specs=[pl.BlockSpec((tm, k), lambda i, j: (i, 0)),
                  pl.BlockSpec((None, k, tn), lambda i, j: (layer, 0, j)),
                  pl.BlockSpec((None, k, tn), lambda i, j: (layer, 0, nj + j))],
        out_specs=pl.BlockSpec((tm, tn), lambda i, j: (i, j)),
        out_shape=jax.ShapeDtypeStruct((m, f), BF16),
        compiler_params=_params("parallel", "arbitrary"),
        name="swiglu_up",
    )(x, w1, w1)


def _mm_ln_kernel(a_ref, w_ref, r_ref, g_ref, b_ref, of_ref, ob_ref, *, alpha, nk):
    kk = pl.program_id(1)

    @pl.when(kk == 0)
    def _():
        of_ref[...] = jnp.zeros_like(of_ref)

    of_ref[...] += _dot(a_ref[...], w_ref[...].astype(BF16))

    @pl.when(kk == nk - 1)
    def _():
        r = alpha * r_ref[...] + of_ref[...]
        mu = jnp.mean(r, axis=-1, keepdims=True)
        c = r - mu
        var = jnp.mean(c * c, axis=-1, keepdims=True)
        y = c * lax.rsqrt(var + LN_EPS) * g_ref[...] + b_ref[...]
        of_ref[...] = y
        ob_ref[...] = y.astype(BF16)


def matmul_residual_ln(a, w, w_row0, resid, g, b, layer, *, alpha, tm, tk):
    m, k = a.shape
    n = w.shape[1]
    assert m % tm == 0 and k % tk == 0 and w_row0 % tk == 0
    nk = k // tk
    kb0 = w_row0 // tk
    row = pl.BlockSpec((tm, n), lambda i, kk: (i, 0))
    vec = pl.BlockSpec((None, 1, n), lambda i, kk: (layer, 0, 0))
    return pl.pallas_call(
        functools.partial(_mm_ln_kernel, alpha=alpha, nk=nk),
        grid=(m // tm, nk),
        in_specs=[pl.BlockSpec((tm, tk), lambda i, kk: (i, kk)),
                  pl.BlockSpec((tk, n), lambda i, kk: (kb0 + kk, 0)),
                  row, vec, vec],
        out_specs=[row, row],
        out_shape=[jax.ShapeDtypeStruct((m, n), F32), jax.ShapeDtypeStruct((m, n), BF16)],
        compiler_params=_params("parallel", "arbitrary"),
        name="proj_ln",
    )(a, w, resid, g, b)


ROUTE_E1, ROUTE_E2, ROUTE_G1, ROUTE_G2, ROUTE_R1, ROUTE_R2 = range(6)


def _router_kernel(h_ref, r_ref, o_ref, cnt_ref, *, ne):
    step = pl.program_id(0)

    @pl.when(step == 0)
    def _():
        cnt_ref[...] = jnp.zeros_like(cnt_ref)

    logits = jnp.dot(h_ref[...], r_ref[...], preferred_element_type=F32,
                     precision=lax.Precision.HIGHEST)
    tm = logits.shape[0]
    lane = lax.broadcasted_iota(jnp.int32, logits.shape, 1)
    neg = jnp.float32(-jnp.inf)
    x = jnp.where(lane < ne, logits, neg)
    m1 = jnp.max(x, axis=-1, keepdims=True)
    i1 = jnp.min(jnp.where(x == m1, lane, ne), axis=-1, keepdims=True)
    x2 = jnp.where(lane == i1, neg, x)
    m2 = jnp.max(x2, axis=-1, keepdims=True)
    i2 = jnp.min(jnp.where(x2 == m2, lane, ne), axis=-1, keepdims=True)
    e2 = jnp.exp(m2 - m1)
    g1 = 1.0 / (1.0 + e2)
    g2 = e2 / (1.0 + e2)

    hit1 = lane == i1
    hit2 = lane == i2
    onehot = jnp.where(hit1, 1.0, 0.0) + jnp.where(hit2, 1.0, 0.0)
    t_i = lax.broadcasted_iota(jnp.int32, (tm, tm), 0)
    s_i = lax.broadcasted_iota(jnp.int32, (tm, tm), 1)
    before = jnp.where(s_i < t_i, 1.0, 0.0).astype(BF16)
    prior = _dot(before, onehot.astype(BF16)) + cnt_ref[...]
    r1 = jnp.sum(jnp.where(hit1, prior, 0.0), axis=-1, keepdims=True)
    r2 = jnp.sum(jnp.where(hit2, prior, 0.0), axis=-1, keepdims=True)
    cnt_ref[...] += jnp.sum(onehot, axis=0, keepdims=True)

    rec = jnp.zeros(logits.shape, F32)
    for ln, val in ((ROUTE_E1, i1.astype(F32)), (ROUTE_E2, i2.astype(F32)), (ROUTE_G1, g1),
                    (ROUTE_G2, g2), (ROUTE_R1, r1), (ROUTE_R2, r2)):
        rec = jnp.where(lane == ln, val, rec)
    o_ref[...] = rec


def router_top2(h, router_padded, layer, ne, *, tm):
    m, k = h.shape
    return pl.pallas_call(
        functools.partial(_router_kernel, ne=ne),
        grid=(m // tm,),
        in_specs=[pl.BlockSpec((tm, k), lambda i: (i, 0)),
                  pl.BlockSpec((None, k, HEAD_DIM), lambda i: (layer, 0, 0))],
        out_specs=[pl.BlockSpec((tm, HEAD_DIM), lambda i: (i, 0)),
                   pl.BlockSpec((1, HEAD_DIM), lambda i: (0, 0))],
        out_shape=[jax.ShapeDtypeStruct((m, HEAD_DIM), F32), jax.ShapeDtypeStruct((1, HEAD_DIM), F32)],
        compiler_params=_params("arbitrary"),
        name="router",
    )(h, router_padded)


MOE_HALF = 512
MOE_ITEM = 2 * MOE_HALF


def _gather_rows_kernel(nrow_ref, src_ref, h_any, o_ref, buf, sem):
    half = pl.program_id(0)
    nrow = buf.shape[0]
    live = nrow_ref[half // 2] > (half % 2) * MOE_HALF

    @pl.when(live)
    def _():
        def start(r, c):
            pltpu.make_async_copy(h_any.at[pl.ds(src_ref[0, r], 1)], buf.at[pl.ds(r, 1)], sem).start()
            return c

        lax.fori_loop(0, nrow, start, 0, unroll=8)

        def wait(r, c):
            pltpu.make_async_copy(h_any.at[pl.ds(0, 1)], buf.at[pl.ds(r, 1)], sem).wait()
            return c

        lax.fori_loop(0, nrow, wait, 0, unroll=8)
        o_ref[...] = buf[...].astype(o_ref.dtype)

    @pl.when(jnp.logical_not(live))
    def _():
        o_ref[...] = jnp.zeros_like(o_ref)


def gather_rows(h, src, item_rows):
    nhalf = src.shape[0]
    d = h.shape[1]
    grid_spec = pltpu.PrefetchScalarGridSpec(
        num_scalar_prefetch=1,
        grid=(nhalf,),
        in_specs=[pl.BlockSpec((None, 1, MOE_HALF), lambda i, nr: (i, 0, 0), memory_space=pltpu.SMEM),
                  pl.BlockSpec(memory_space=pl.ANY)],
        out_specs=pl.BlockSpec((MOE_HALF, d), lambda i, nr: (i, 0)),
        scratch_shapes=[pltpu.VMEM((MOE_HALF, d), F32), pltpu.SemaphoreType.DMA(())],
    )
    return pl.pallas_call(
        _gather_rows_kernel,
        grid_spec=grid_spec,
        out_shape=jax.ShapeDtypeStruct((nhalf * MOE_HALF, d), BF16),
        compiler_params=_params("arbitrary"),
        name="moe_gather",
    )(item_rows, src, h)


def _grouped_swiglu_kernel(ie_ref, nrow_ref, x_ref, wg_ref, wu_ref, o_ref):
    nrow = nrow_ref[pl.program_id(1)]

    def act(x):
        g = _dot(x, wg_ref[...].astype(BF16))
        u = _dot(x, wu_ref[...].astype(BF16))
        return (g * _sigmoid(g) * u).astype(o_ref.dtype)

    @pl.when(nrow > MOE_HALF)
    def _():
        o_ref[...] = act(x_ref[...])

    @pl.when(nrow <= MOE_HALF)
    def _():
        o_ref[MOE_HALF:, :] = jnp.zeros((MOE_HALF, o_ref.shape[1]), o_ref.dtype)

    @pl.when(jnp.logical_and(nrow > 0, nrow <= MOE_HALF))
    def _():
        o_ref[:MOE_HALF, :] = act(x_ref[:MOE_HALF, :])

    @pl.when(nrow == 0)
    def _():
        o_ref[:MOE_HALF, :] = jnp.zeros((MOE_HALF, o_ref.shape[1]), o_ref.dtype)


def grouped_swiglu(xs, w1, layer, item_expert, item_rows, *, tn):
    rows, k = xs.shape
    nitem = rows // MOE_ITEM
    f = w1.shape[-1] // 2
    nj = f // tn
    grid_spec = pltpu.PrefetchScalarGridSpec(
        num_scalar_prefetch=2,
        grid=(nj, nitem),
        in_specs=[pl.BlockSpec((MOE_ITEM, k), lambda j, it, ie, nr: (jnp.where(nr[it] > 0, it, 0), 0)),
                  pl.BlockSpec((None, None, k, tn), lambda j, it, ie, nr: (layer, ie[it], 0, j)),
                  pl.BlockSpec((None, None, k, tn), lambda j, it, ie, nr: (layer, ie[it], 0, nj + j))],
        out_specs=pl.BlockSpec((MOE_ITEM, tn), lambda j, it, ie, nr: (it, j)),
    )
    return pl.pallas_call(
        _grouped_swiglu_kernel,
        grid_spec=grid_spec,
        out_shape=jax.ShapeDtypeStruct((rows, f), BF16),
        compiler_params=_params("arbitrary", "arbitrary"),
        name="moe_up",
    )(item_expert, item_rows, xs, w1, w1)


def _grouped_down_kernel(ie_ref, nrow_ref, a_ref, w_ref, o_ref):
    nrow = nrow_ref[pl.program_id(0)]

    @pl.when(pl.program_id(1) == 0)
    def _():
        o_ref[...] = jnp.zeros_like(o_ref)

    @pl.when(nrow > MOE_HALF)
    def _():
        o_ref[...] += _dot(a_ref[...], w_ref[...].astype(BF16))

    @pl.when(jnp.logical_and(nrow > 0, nrow <= MOE_HALF))
    def _():
        o_ref[:MOE_HALF, :] += _dot(a_ref[:MOE_HALF, :], w_ref[...].astype(BF16))


def grouped_down(mid, w2, layer, item_expert, item_rows, *, tk):
    rows, f = mid.shape
    nitem = rows // MOE_ITEM
    d = w2.shape[-1]
    nk = f // tk

    def kidx(it, kk, nr):
        return jnp.where(nr[it] > 0, kk, nk - 1)

    grid_spec = pltpu.PrefetchScalarGridSpec(
        num_scalar_prefetch=2,
        grid=(nitem, nk),
        in_specs=[pl.BlockSpec((MOE_ITEM, tk), lambda it, kk, ie, nr: (it, kidx(it, kk, nr))),
                  pl.BlockSpec((None, None, tk, d), lambda it, kk, ie, nr: (layer, ie[it], kidx(it, kk, nr), 0))],
        out_specs=pl.BlockSpec((MOE_ITEM, d), lambda it, kk, ie, nr: (it, 0)),
    )
    return pl.pallas_call(
        _grouped_down_kernel,
        grid_spec=grid_spec,
        out_shape=jax.ShapeDtypeStruct((rows, d), F32),
        compiler_params=_params("arbitrary", "arbitrary"),
        name="moe_down",
    )(item_expert, item_rows, mid, w2)


def _combine_ln_kernel(dst_ref, y_any, route_ref, r_ref, g_ref, b_ref, of_ref, ob_ref, buf, sem, *, alpha):
    tm = r_ref.shape[0]

    def start(i, c):
        for slot in range(TOP_K):
            pltpu.make_async_copy(y_any.at[pl.ds(dst_ref[0, slot * tm + i], 1)],
                                  buf.at[slot, pl.ds(i, 1)], sem).start()
        return c

    lax.fori_loop(0, tm, start, 0, unroll=8)

    def wait(i, c):
        for slot in range(TOP_K):
            pltpu.make_async_copy(y_any.at[pl.ds(0, 1)], buf.at[slot, pl.ds(i, 1)], sem).wait()
        return c

    lax.fori_loop(0, tm, wait, 0, unroll=8)
    route = route_ref[...]
    g1 = route[:, ROUTE_G1:ROUTE_G1 + 1]
    g2 = route[:, ROUTE_G2:ROUTE_G2 + 1]
    r = alpha * r_ref[...] + (g1 * buf[0] + g2 * buf[1])
    mu = jnp.mean(r, axis=-1, keepdims=True)
    c = r - mu
    var = jnp.mean(c * c, axis=-1, keepdims=True)
    y = c * lax.rsqrt(var + LN_EPS) * g_ref[...] + b_ref[...]
    of_ref[...] = y
    ob_ref[...] = y.astype(BF16)


def combine_ln(ys, dst, route, resid, g, b, layer, *, alpha, tm):
    m, d = resid.shape
    row = pl.BlockSpec((tm, d), lambda i: (i, 0))
    vec = pl.BlockSpec((None, 1, d), lambda i: (layer, 0, 0))
    return pl.pallas_call(
        functools.partial(_combine_ln_kernel, alpha=alpha),
        grid=(m // tm,),
        in_specs=[pl.BlockSpec((None, 1, TOP_K * tm), lambda i: (i, 0, 0), memory_space=pltpu.SMEM),
                  pl.BlockSpec(memory_space=pl.ANY),
                  pl.BlockSpec((tm, HEAD_DIM), lambda i: (i, 0)),
                  row, vec, vec],
        out_specs=[row, row],
        out_shape=[jax.ShapeDtypeStruct((m, d), F32), jax.ShapeDtypeStruct((m, d), BF16)],
        scratch_shapes=[pltpu.VMEM((TOP_K, tm, d), F32), pltpu.SemaphoreType.DMA(())],
        compiler_params=_params("arbitrary"),
        name="moe_combine_ln",
    )(dst, ys, route, resid, g, b)


def routed_ffn(h, router_padded, w1, w2, layer, g, b, ln_layer, ne, *, alpha, tm_route, tm_comb, tf):
    m, d = h.shape
    route, counts = router_top2(h, router_padded, layer, ne, tm=tm_route)
    cnt = counts[0, :ne].astype(jnp.int32)
    nitem_e = (cnt + MOE_ITEM - 1) // MOE_ITEM
    item_end = jnp.cumsum(nitem_e)
    item_start = item_end - nitem_e
    row_off = item_start * MOE_ITEM
    n_items = item_end[-1]
    max_items = (TOP_K * m + MOE_ITEM - 1) // MOE_ITEM + ne
    item = jnp.arange(max_items, dtype=jnp.int32)
    item_expert = jnp.minimum(jnp.searchsorted(item_end, item, side="right"), ne - 1).astype(jnp.int32)
    item_rows = jnp.clip(cnt[item_expert] - (item - item_start[item_expert]) * MOE_ITEM, 0, MOE_ITEM)
    item_rows = jnp.where(item < n_items, item_rows, 0).astype(jnp.int32)
    item_expert = jnp.where(item < n_items, item_expert, item_expert[jnp.maximum(n_items - 1, 0)])
    e1 = route[:, ROUTE_E1].astype(jnp.int32)
    e2 = route[:, ROUTE_E2].astype(jnp.int32)
    dst1 = row_off[e1] + route[:, ROUTE_R1].astype(jnp.int32)
    dst2 = row_off[e2] + route[:, ROUTE_R2].astype(jnp.int32)
    tok = jnp.arange(m, dtype=jnp.int32)
    src = jnp.zeros((max_items * MOE_ITEM,), jnp.int32).at[dst1].set(tok).at[dst2].set(tok)
    dst = jnp.concatenate([dst1.reshape(m // tm_comb, 1, tm_comb), dst2.reshape(m // tm_comb, 1, tm_comb)], axis=2)

    xs = gather_rows(h, src.reshape(2 * max_items, 1, MOE_HALF), item_rows)
    mid = grouped_swiglu(xs, w1, layer, item_expert, item_rows, tn=tf)
    ys = grouped_down(mid, w2, layer, item_expert, item_rows, tk=tf)
    return combine_ln(ys, dst, route, h, g, b, ln_layer, alpha=alpha, tm=tm_comb)


def _hgrn_kernel(q_ref, z_ref, v_ref, g_ref, lb_ref, gn_ref, s0_ref, o_ref, sout_ref,
                 st_s, qt_s, kd_s, vt_s, ebl_s, acc_s, *, heads, per_block_state):
    rows, width = q_ref.shape
    nblk = rows // HGRN_BLOCK
    step = pl.program_id(0)

    lb = lb_ref[...]
    log_lb = jnp.log(jnp.maximum(lb, LB_FLOOR))
    z = z_ref[...]
    lo = jnp.log1p(-lb) + _log_sigmoid(z)
    logf = jnp.maximum(log_lb, lo) + jnp.log1p(jnp.exp(-jnp.abs(log_lb - lo)))
    kin = (1.0 - lb) * _sigmoid(-z)
    qraw = q_ref[...]
    q = qraw * _sigmoid(qraw)
    v = v_ref[...]

    r_i = lax.broadcasted_iota(jnp.int32, (rows, rows), 0)
    c_i = lax.broadcasted_iota(jnp.int32, (rows, rows), 1)
    same = (r_i >> HGRN_BLOCK_LOG2) == (c_i >> HGRN_BLOCK_LOG2)
    tri = jnp.where(same & (c_i <= r_i), 1.0, 0.0).astype(BF16)
    blk = jnp.where(same, 1.0, 0.0).astype(BF16)
    parts = _split3(logf)
    b = _dot(tri, parts[0]) + _dot(tri, parts[1]) + _dot(tri, parts[2])
    btot = _dot(blk, parts[0]) + _dot(blk, parts[1]) + _dot(blk, parts[2])
    qt_s[...] = q * jnp.exp(b)
    kd_s[...] = kin * jnp.exp(btot - b)
    ebl_s[...] = jnp.exp(btot)
    vt_s[...] = v.T

    pos = lax.broadcasted_iota(jnp.int32, (rows, width), 0) & (HGRN_BLOCK - 1)
    acc = None
    for d in range(HGRN_BLOCK):
        if d == 0:
            w = q * kin
            vs = v
        else:
            decay = jnp.exp(b - pltpu.roll(b, d, 0))
            w = jnp.where(pos >= d, q * pltpu.roll(kin, d, 0) * decay, 0.0)
            vs = pltpu.roll(v, d, 0)
        att = jnp.concatenate(
            [jnp.broadcast_to(jnp.sum(w[:, h * HEAD_DIM:(h + 1) * HEAD_DIM], axis=1, keepdims=True),
                              (rows, HEAD_DIM)) for h in range(heads)], axis=1)
        acc = att * vs if acc is None else acc + att * vs
    acc_s[...] = acc

    if not per_block_state:
        @pl.when(step == 0)
        def _():
            st_s[...] = jnp.zeros_like(st_s)

    lane_blk = lax.broadcasted_iota(jnp.int32, (HEAD_DIM, rows), 1) >> HGRN_BLOCK_LOG2

    def block_step(c, carry):
        r0 = pl.multiple_of(c * HGRN_BLOCK, HGRN_BLOCK)
        for h in range(heads):
            hc = slice(h * HEAD_DIM, (h + 1) * HEAD_DIM)
            if per_block_state:
                st = s0_ref[c, h].T
            else:
                st = st_s[h]
            qc = qt_s[pl.ds(r0, HGRN_BLOCK), hc]
            acc_s[pl.ds(r0, HGRN_BLOCK), hc] += _dot_nt(qc.astype(BF16), st.astype(BF16))
            vmask = jnp.where(lane_blk == c, vt_s[hc, :], 0.0).astype(BF16)
            upd = _dot(vmask, kd_s[:, hc].astype(BF16))
            st_new = st * ebl_s[pl.ds(r0, 1), hc] + upd
            if per_block_state:
                sout_ref[c, h] = st_new.T
            else:
                st_s[h] = st_new
        return carry

    lax.fori_loop(0, nblk, block_step, 0)

    if not per_block_state:
        @pl.when(step == pl.num_programs(0) - 1)
        def _():
            for h in range(heads):
                sout_ref[0, h] = st_s[h].T

    o = acc_s[...]
    ms = jnp.concatenate(
        [jnp.broadcast_to(jnp.sum(jnp.square(o[:, h * HEAD_DIM:(h + 1) * HEAD_DIM]), axis=1, keepdims=True),
                          (rows, HEAD_DIM)) for h in range(heads)], axis=1) * (1.0 / HEAD_DIM)
    gate = g_ref[...]
    o_ref[...] = (o * lax.rsqrt(ms + RMS_EPS) * gn_ref[...] * (gate * _sigmoid(gate))).astype(o_ref.dtype)


def hgrn_mixer(p, row0, nrows, lb, gnorm, layer, s0, *, per_block_state, width):
    heads = width // HEAD_DIM
    tile = HGRN_TILE
    assert nrows % tile == 0 and row0 % tile == 0
    rb0 = row0 // tile
    nt = nrows // tile
    if per_block_state:
        assert nt == 1
        nstate = s0.shape[0]
    else:
        nstate = 1
        s0 = jnp.zeros((1, heads, HEAD_DIM, HEAD_DIM), F32)

    def col(cb):
        return pl.BlockSpec((tile, width), lambda i: (rb0 + i, cb))

    vec = pl.BlockSpec((None, 1, width), lambda i: (layer, 0, 0))
    state = pl.BlockSpec((nstate, heads, HEAD_DIM, HEAD_DIM), lambda i: (0, 0, 0, 0))
    return pl.pallas_call(
        functools.partial(_hgrn_kernel, heads=heads, per_block_state=per_block_state),
        grid=(nt,),
        in_specs=[col(0), col(1), col(2), col(3), vec, vec, state],
        out_specs=[pl.BlockSpec((tile, width), lambda i: (i, 0)), state],
        out_shape=[jax.ShapeDtypeStruct((nrows, width), BF16),
                   jax.ShapeDtypeStruct((nstate, heads, HEAD_DIM, HEAD_DIM), F32)],
        scratch_shapes=[pltpu.VMEM((heads, HEAD_DIM, HEAD_DIM), F32),
                        pltpu.VMEM((tile, width), F32),
                        pltpu.VMEM((tile, width), F32),
                        pltpu.VMEM((width, tile), F32),
                        pltpu.VMEM((tile, width), F32),
                        pltpu.VMEM((tile, width), F32)],
        compiler_params=_params("arbitrary"),
        name="hgrn2",
    )(p, p, p, p, lb, gnorm, s0)


def _pool_tile(xp, pos, w_ref, scale):
    sums = {1: xp}
    s = xp
    w = 1
    while w < POOL_WINDOWS[-1]:
        s = s + pltpu.roll(s, w, 0)
        w *= 2
        sums[w] = s
    gd = HEAD_DIM
    outs = []
    for g, win in enumerate(POOL_WINDOWS):
        sl = slice(g * gd, (g + 1) * gd)
        cnt = jnp.clip(pos + 1, 1, win).astype(F32)
        d = sums[win][:, sl] / cnt - xp[:, sl]
        outs.append(_dot(d.astype(BF16), w_ref[g].astype(BF16)))
    return jnp.concatenate(outs, axis=1) * scale


def _pool_stream_kernel(u_ref, w_ref, sc_ref, o_ref, xp_s):
    step = pl.program_id(0)
    rows = u_ref.shape[0]

    @pl.when(step == 0)
    def _():
        xp_s[0:POOL_HALO, :] = jnp.zeros((POOL_HALO, xp_s.shape[1]), F32)

    xp_s[POOL_HALO:, :] = u_ref[...]
    xp = xp_s[...]
    pos = step * rows - POOL_HALO + lax.broadcasted_iota(jnp.int32, (rows + POOL_HALO, 1), 0)
    y = _pool_tile(xp, pos, w_ref, sc_ref[...])
    o_ref[...] = y[POOL_HALO:, :].astype(o_ref.dtype)
    xp_s[0:POOL_HALO, :] = xp[rows:, :]


def _pool_batch_kernel(u_ref, buf_ref, w_ref, sc_ref, o_ref, xp_s, *, start):
    nb, t, c = u_ref.shape
    seg = POOL_HALO + t
    for bi in range(nb):
        xp_s[bi * seg:bi * seg + POOL_HALO, :] = buf_ref[bi]
        xp_s[bi * seg + POOL_HALO:(bi + 1) * seg, :] = u_ref[bi]
    r = lax.broadcasted_iota(jnp.int32, (nb * seg, 1), 0) & (seg - 1)
    y = _pool_tile(xp_s[...], start - POOL_HALO + r, w_ref, sc_ref[...])
    for bi in range(nb):
        o_ref[bi] = y[bi * seg + POOL_HALO:(bi + 1) * seg, :].astype(o_ref.dtype)


def pool_stream(p, nrows, col_block, w_pool, pool_scale, layer, *, width, tile):
    assert nrows % tile == 0
    return pl.pallas_call(
        _pool_stream_kernel,
        grid=(nrows // tile,),
        in_specs=[pl.BlockSpec((tile, width), lambda i: (i, col_block)),
                  pl.BlockSpec((None,) + w_pool.shape[1:], lambda i: (layer, 0, 0, 0)),
                  pl.BlockSpec((None, 1, width), lambda i: (layer, 0, 0))],
        out_specs=pl.BlockSpec((tile, width), lambda i: (i, 0)),
        out_shape=jax.ShapeDtypeStruct((nrows, width), BF16),
        scratch_shapes=[pltpu.VMEM((tile + POOL_HALO, width), F32)],
        compiler_params=_params("arbitrary"),
        name="pool_stream",
    )(p, w_pool, pool_scale)


def pool_batch(u, buf, w_pool, pool_scale, layer, *, start):
    nb, t, c = u.shape
    return pl.pallas_call(
        functools.partial(_pool_batch_kernel, start=start),
        grid=(1,),
        in_specs=[pl.BlockSpec((nb, t, c), lambda i: (0, 0, 0)),
                  pl.BlockSpec((nb, POOL_HALO, c), lambda i: (0, 0, 0)),
                  pl.BlockSpec((None,) + w_pool.shape[1:], lambda i: (layer, 0, 0, 0)),
                  pl.BlockSpec((None, 1, c), lambda i: (layer, 0, 0))],
        out_specs=pl.BlockSpec((nb, t, c), lambda i: (0, 0, 0)),
        out_shape=jax.ShapeDtypeStruct((nb, t, c), BF16),
        scratch_shapes=[pltpu.VMEM((nb * (POOL_HALO + t), c), F32)],
        compiler_params=_params("arbitrary"),
        name="pool_batch",
    )(u, buf, w_pool, pool_scale)


def _sb_block(qb, kblk, vblk, mask, r_run, upper, scale):
    z = _dot_nt(qb, kblk.astype(BF16)) * scale
    l1m = _log_sigmoid(-z)
    if mask is not None:
        l1m = jnp.where(mask, l1m, 0.0)
    parts = _split3(l1m)
    later = _dot(parts[0], upper) + _dot(parts[1], upper) + _dot(parts[2], upper)
    a = jnp.exp(z + l1m + later + r_run)
    if mask is not None:
        a = jnp.where(mask, a, 0.0)
    return _dot(a.astype(BF16), vblk.astype(BF16)), jnp.sum(l1m, axis=1, keepdims=True)


def _strict_upper(tk):
    j = lax.broadcasted_iota(jnp.int32, (tk, tk), 0)
    s = lax.broadcasted_iota(jnp.int32, (tk, tk), 1)
    return jnp.where(j > s, 1.0, 0.0).astype(BF16)


def _sb_stream_kernel(q_ref, k_ref, v_ref, o_ref, acc_s, run_s, *, tk, scale):
    tq = q_ref.shape[0]
    qi = pl.program_id(1)
    qb = q_ref[...].astype(BF16)
    upper = _strict_upper(tk)
    acc_s[...] = jnp.zeros_like(acc_s)
    run_s[...] = jnp.zeros_like(run_s)
    qpos = qi * tq + lax.broadcasted_iota(jnp.int32, (tq, tk), 0)
    col = lax.broadcasted_iota(jnp.int32, (tq, tk), 1)

    def cond(carry):
        kb, live = carry
        return jnp.logical_and(kb >= 0, live > SB_DEAD_LOG)

    def body(carry):
        kb, _ = carry
        k0 = pl.multiple_of(kb * tk, tk)
        mask = (k0 + col) < qpos
        r_run = run_s[...]
        contrib, rowsum = _sb_block(qb, k_ref[pl.ds(k0, tk), :], v_ref[pl.ds(k0, tk), :],
                                    mask, r_run, upper, scale)
        acc_s[...] += contrib
        r_new = r_run + rowsum
        run_s[...] = r_new
        return kb - 1, jnp.max(r_new)

    lax.while_loop(cond, body, ((qi + 1) * (tq // tk) - 1, jnp.float32(0.0)))
    o_ref[...] = acc_s[...].astype(o_ref.dtype)


def sb_stream(p, nrows, col_q, col_k, col_v, *, heads, tq, tk):
    assert nrows % tq == 0 and tq % tk == 0
    scale = HEAD_DIM ** -0.5
    return pl.pallas_call(
        functools.partial(_sb_stream_kernel, tk=tk, scale=scale),
        grid=(heads, nrows // tq),
        in_specs=[pl.BlockSpec((tq, HEAD_DIM), lambda h, i: (i, col_q + h)),
                  pl.BlockSpec((nrows, HEAD_DIM), lambda h, i: (0, col_k + h)),
                  pl.BlockSpec((nrows, HEAD_DIM), lambda h, i: (0, col_v + h))],
        out_specs=pl.BlockSpec((tq, HEAD_DIM), lambda h, i: (i, h)),
        out_shape=jax.ShapeDtypeStruct((nrows, heads * HEAD_DIM), BF16),
        scratch_shapes=[pltpu.VMEM((tq, HEAD_DIM), F32), pltpu.VMEM((tq, 1), F32)],
        compiler_params=_params("parallel", "arbitrary"),
        name="sb_stream",
    )(p, p, p)


def _sb_decode_kernel(*refs, heads, tk, scale):
    npair = heads // 2
    q_refs = refs[0:npair]
    kn_refs = refs[npair:2 * npair]
    vn_refs = refs[2 * npair:3 * npair]
    kp_ref, vp_ref, o_ref, kpad_s, vpad_s, acc_s, run_s, live_s = refs[3 * npair:]
    t = o_ref.shape[0]
    tp = kp_ref.shape[1]
    step = pl.program_id(1)
    upper = _strict_upper(tk)

    def head_cols(pair_refs, h):
        return pair_refs[h // 2][:, (h % 2) * HEAD_DIM:(h % 2 + 1) * HEAD_DIM]

    qbs = [head_cols(q_refs, h).astype(BF16) for h in range(heads)]

    @pl.when(step == 0)
    def _():
        kpad_s[...] = jnp.zeros_like(kpad_s)
        vpad_s[...] = jnp.zeros_like(vpad_s)
        row = lax.broadcasted_iota(jnp.int32, (t, tk), 0)
        col = lax.broadcasted_iota(jnp.int32, (t, tk), 1)
        live = None
        for h in range(heads):
            kpad_s[h, 0:t, :] = head_cols(kn_refs, h)
            vpad_s[h, 0:t, :] = head_cols(vn_refs, h)
            contrib, rowsum = _sb_block(qbs[h], kpad_s[h], vpad_s[h], col < row,
                                        jnp.zeros((t, 1), F32), upper, scale)
            acc_s[h] = contrib
            run_s[h] = rowsum
            m = jnp.max(rowsum)
            live = m if live is None else jnp.maximum(live, m)
        live_s[0] = live

    @pl.when(live_s[0] > SB_DEAD_LOG)
    def _():
        live = None
        for h in range(heads):
            r_run = run_s[h]
            acc = acc_s[h]
            for j in reversed(range(tp // tk)):
                contrib, rowsum = _sb_block(qbs[h], kp_ref[h, j * tk:(j + 1) * tk, :],
                                            vp_ref[h, j * tk:(j + 1) * tk, :], None, r_run, upper, scale)
                acc = acc + contrib
                r_run = r_run + rowsum
            acc_s[h] = acc
            run_s[h] = r_run
            m = jnp.max(r_run)
            live = m if live is None else jnp.maximum(live, m)
        live_s[0] = live

    @pl.when(step == pl.num_programs(1) - 1)
    def _():
        for h in range(heads):
            o_ref[:, h * HEAD_DIM:(h + 1) * HEAD_DIM] = acc_s[h].astype(o_ref.dtype)


def sb_decode(p, row0, nb, t, col_q, col_k, col_v, k_past, v_past, layer, *, heads, tk, tp):
    npast = k_past.shape[3]
    assert row0 % t == 0 and npast % tp == 0 and tp % tk == 0 and t <= tk and heads % 2 == 0
    assert col_q % 2 == 0 and col_k % 2 == 0 and col_v % 2 == 0
    rb0 = row0 // t
    nchunk = npast // tp
    scale = HEAD_DIM ** -0.5

    def new(cb):
        return [pl.BlockSpec((t, 2 * HEAD_DIM), lambda b, c, j=j: (rb0 + b, cb // 2 + j)) for j in range(heads // 2)]

    past = pl.BlockSpec((None, None, heads, tp, HEAD_DIM), lambda b, c: (layer, b, 0, nchunk - 1 - c, 0))
    npair = heads // 2
    return pl.pallas_call(
        functools.partial(_sb_decode_kernel, heads=heads, tk=tk, scale=scale),
        grid=(nb, nchunk),
        in_specs=new(col_q) + new(col_k) + new(col_v) + [past, past],
        out_specs=pl.BlockSpec((t, heads * HEAD_DIM), lambda b, c: (b, 0)),
        out_shape=jax.ShapeDtypeStruct((nb * t, heads * HEAD_DIM), BF16),
        scratch_shapes=[pltpu.VMEM((heads, tk, HEAD_DIM), F32), pltpu.VMEM((heads, tk, HEAD_DIM), F32),
                        pltpu.VMEM((heads, t, HEAD_DIM), F32), pltpu.VMEM((heads, t, 1), F32),
                        pltpu.SMEM((1,), F32)],
        compiler_params=_params("parallel", "arbitrary"),
        name="sb_decode",
    )(*([p] * (3 * npair)), k_past, v_past)


def kernel(x_prompt, x_sample, cache_sb_k, cache_sb_v, state_hgrn, state_pool, w_in, lb_param, hgrn_norm,
           w_pool, pool_scale, w_out, ln1_g, ln1_b, ln2_g, ln2_b, ffn_w1, ffn_w2, moe_router, moe_w1, moe_w2):
    bp, seq, d_model = x_prompt.shape
    nb, dseq, _ = x_sample.shape
    depth = w_in.shape[0]
    past_len = cache_sb_k.shape[2]
    a_heads = state_hgrn.shape[2]
    c_heads = cache_sb_k.shape[3]
    a_width = a_heads * HEAD_DIM
    c_width = c_heads * HEAD_DIM
    b_width = state_pool.shape[3]
    ne = moe_router.shape[2]
    d_ff = ffn_w2.shape[1]
    assert bp == 1 and a_width == c_width
    alpha = (2 * depth) ** 0.25
    np_rows = bp * seq
    ns_rows = nb * dseq
    m = np_rows + ns_rows

    pool_cb = 4 * a_width // b_width
    cq = (4 * a_width + b_width) // HEAD_DIM
    ck = cq + c_heads
    cv = ck + c_heads
    k_col0 = ck * HEAD_DIM
    v_col0 = cv * HEAD_DIM
    u_col0 = 4 * a_width

    pl_soft = jax.nn.softmax(lb_param.astype(F32), axis=0)
    lb_all = (jnp.cumsum(pl_soft, axis=0) - pl_soft[0])[:, None, :]
    gnorm = hgrn_norm.reshape(depth, 1, a_width)
    pscale = pool_scale.reshape(depth, 1, b_width)
    vec = lambda a: a.reshape(depth, 1, d_model)
    ln1g, ln1b, ln2g, ln2b = vec(ln1_g), vec(ln1_b), vec(ln2_g), vec(ln2_b)
    router_p = jnp.pad(moe_router, ((0, 0), (0, 0), (0, HEAD_DIM - ne)))
    k_cache = jnp.transpose(cache_sb_k, (0, 1, 3, 2, 4))
    v_cache = jnp.transpose(cache_sb_v, (0, 1, 3, 2, 4))
    pool_hist = jnp.pad(state_pool, ((0, 0), (0, 0), (POOL_HALO - state_pool.shape[2], 0), (0, 0)))
    w_out2 = w_out.reshape(depth * d_model, d_model)
    ffn_w2_2 = ffn_w2.reshape(-1, d_model)

    x = jnp.concatenate([x_prompt.reshape(np_rows, d_model), x_sample.reshape(ns_rows, d_model)], axis=0)
    xb = x.astype(BF16)

    outs = {k: [] for k in ("ps", "pb", "pk", "pv", "ss", "sb", "sk", "sv")}
    for l in range(depth):
        p = matmul_layer(xb, w_in, l, tm=2112, tn=256)

        oa_p, st_p = hgrn_mixer(p, 0, np_rows, lb_all, gnorm, l, None, per_block_state=False, width=a_width)
        oa_s, st_s = hgrn_mixer(p, np_rows, ns_rows, lb_all, gnorm, l, state_hgrn[l],
                                per_block_state=True, width=a_width)
        ob_p = pool_stream(p, np_rows, pool_cb, w_pool, pscale, l, width=b_width, tile=512)
        u_s = p[np_rows:, u_col0:u_col0 + b_width].reshape(nb, dseq, b_width)
        ob_s = pool_batch(u_s, pool_hist[l], w_pool, pscale, l, start=past_len).reshape(ns_rows, b_width)
        oc_p = sb_stream(p, np_rows, cq, ck, cv, heads=c_heads, tq=512, tk=128)
        oc_s = sb_decode(p, np_rows, nb, dseq, cq, ck, cv, k_cache, v_cache, l,
                         heads=c_heads, tk=128, tp=512)

        mix_in = jnp.concatenate([jnp.concatenate([oa_p, ob_p, oc_p], axis=1),
                                  jnp.concatenate([oa_s, ob_s, oc_s], axis=1)], axis=0)
        h, hb = matmul_residual_ln(mix_in, w_out2, l * d_model, x, ln1g, ln1b, l, alpha=alpha, tm=1056, tk=256)

        if l % 2 == 0:
            mid = swiglu_up(hb, ffn_w1, l // 2, tm=1056, tn=512)
            x, xb = matmul_residual_ln(mid, ffn_w2_2, (l // 2) * d_ff, h, ln2g, ln2b, l,
                                       alpha=alpha, tm=1056, tk=256)
        else:
            x, xb = routed_ffn(h, router_p, moe_w1, moe_w2, l // 2, ln2g, ln2b, l, ne, alpha=alpha,
                               tm_route=528, tm_comb=256, tf=512)

        outs["ps"].append(st_p)
        outs["pb"].append(p[np_rows - (POOL_HALO - 1):np_rows, u_col0:u_col0 + b_width][None])
        outs["pk"].append(p[:np_rows, k_col0:k_col0 + c_width].reshape(bp, seq, c_heads, HEAD_DIM))
        outs["pv"].append(p[:np_rows, v_col0:v_col0 + c_width].reshape(bp, seq, c_heads, HEAD_DIM))
        outs["ss"].append(st_s)
        outs["sb"].append(u_s[:, dseq - (POOL_HALO - 1):, :])
        outs["sk"].append(p[np_rows:, k_col0:k_col0 + c_width].reshape(nb, dseq, c_heads, HEAD_DIM))
        outs["sv"].append(p[np_rows:, v_col0:v_col0 + c_width].reshape(nb, dseq, c_heads, HEAD_DIM))

    stack = lambda k: jnp.stack(outs[k])
    return (x[:np_rows].reshape(bp, seq, d_model), x[np_rows:].reshape(nb, dseq, d_model),
            stack("ps"), stack("pb"), stack("pk"), stack("pv"),
            stack("ss"), stack("sb"), stack("sk"), stack("sv"))
```

```python
import functools

import jax
import jax.numpy as jnp
from jax import lax
from jax.experimental import pallas as pl
from jax.experimental.pallas import tpu as pltpu

F32 = jnp.float32
BF16 = jnp.bfloat16

HEAD_DIM = 128
HGRN_BLOCK_LOG2 = 4
HGRN_BLOCK = 1 << HGRN_BLOCK_LOG2
HGRN_TILE = 256
POOL_WINDOWS = (2, 4, 8, 16)
POOL_HALO = 16
TOP_K = 2
LN_EPS = 1e-5
RMS_EPS = 1e-6
LB_FLOOR = 1e-30
SB_DEAD_LOG = -104.0
VMEM_LIMIT_BYTES = 56 * 1024 * 1024


def _params(*sem):
    return pltpu.CompilerParams(dimension_semantics=sem, vmem_limit_bytes=VMEM_LIMIT_BYTES)


def _dot(a, b):
    return jnp.dot(a, b, preferred_element_type=F32)


def _dot_nt(a, b):
    return lax.dot_general(a, b, (((1,), (1,)), ((), ())), preferred_element_type=F32)


def _split3(x):
    h1 = x.astype(BF16)
    r1 = x - h1.astype(F32)
    h2 = r1.astype(BF16)
    h3 = (r1 - h2.astype(F32)).astype(BF16)
    return h1, h2, h3


def _sigmoid(x):
    return 1.0 / (1.0 + jnp.exp(-x))


def _log_sigmoid(x):
    return jnp.minimum(x, 0.0) - jnp.log1p(jnp.exp(-jnp.abs(x)))


def _mm_kernel(a_ref, w_ref, o_ref):
    o_ref[...] = _dot(a_ref[...], w_ref[...].astype(BF16)).astype(o_ref.dtype)


def matmul_layer(a, w, layer, *, tm, tn):
    m, k = a.shape
    n = w.shape[2]
    assert m % tm == 0 and n % tn == 0
    return pl.pallas_call(
        _mm_kernel,
        grid=(m // tm, n // tn),
        in_specs=[pl.BlockSpec((tm, k), lambda i, j: (i, 0)),
                  pl.BlockSpec((None, k, tn), lambda i, j: (layer, 0, j))],
        out_specs=pl.BlockSpec((tm, tn), lambda i, j: (i, j)),
        out_shape=jax.ShapeDtypeStruct((m, n), F32),
        compiler_params=_params("parallel", "arbitrary"),
        name="in_proj",
    )(a, w)


def _swiglu_kernel(x_ref, wg_ref, wu_ref, o_ref):
    x = x_ref[...]
    g = _dot(x, wg_ref[...].astype(BF16))
    u = _dot(x, wu_ref[...].astype(BF16))
    o_ref[...] = (g * _sigmoid(g) * u).astype(o_ref.dtype)


def swiglu_up(x, w1, layer, *, tm, tn):
    m, k = x.shape
    f = w1.shape[-1] // 2
    assert m % tm == 0 and f % tn == 0
    nj = f // tn
    return pl.pallas_call(
        _swiglu_kernel,
        grid=(m // tm, nj),
        in_specs=[pl.BlockSpec((tm, k), lambda i, j: (i, 0)),
                  pl.BlockSpec((None, k, tn), lambda i, j: (layer, 0, j)),
                  pl.BlockSpec((None, k, tn), lambda i, j: (layer, 0, nj + j))],
        out_specs=pl.BlockSpec((tm, tn), lambda i, j: (i, j)),
        out_shape=jax.ShapeDtypeStruct((m, f), BF16),
        compiler_params=_params("parallel", "arbitrary"),
        name="swiglu_up",
    )(x, w1, w1)


def _mm_ln_kernel(a_ref, w_ref, r_ref, g_ref, b_ref, of_ref, ob_ref, *, alpha, nk):
    kk = pl.program_id(1)

    @pl.when(kk == 0)
    def _():
        of_ref[...] = jnp.zeros_like(of_ref)

    of_ref[...] += _dot(a_ref[...], w_ref[...].astype(BF16))

    @pl.when(kk == nk - 1)
    def _():
        r = alpha * r_ref[...] + of_ref[...]
        mu = jnp.mean(r, axis=-1, keepdims=True)
        c = r - mu
        var = jnp.mean(c * c, axis=-1, keepdims=True)
        y = c * lax.rsqrt(var + LN_EPS) * g_ref[...] + b_ref[...]
        of_ref[...] = y
        ob_ref[...] = y.astype(BF16)


def matmul_residual_ln(a, w, w_row0, resid, g, b, layer, *, alpha, tm, tk):
    m, k = a.shape
    n = w.shape[1]
    assert m % tm == 0 and k % tk == 0 and w_row0 % tk == 0
    nk = k // tk
    kb0 = w_row0 // tk
    row = pl.BlockSpec((tm, n), lambda i, kk: (i, 0))
    vec = pl.BlockSpec((None, 1, n), lambda i, kk: (layer, 0, 0))
    return pl.pallas_call(
        functools.partial(_mm_ln_kernel, alpha=alpha, nk=nk),
        grid=(m // tm, nk),
        in_specs=[pl.BlockSpec((tm, tk), lambda i, kk: (i, kk)),
                  pl.BlockSpec((tk, n), lambda i, kk: (kb0 + kk, 0)),
                  row, vec, vec],
        out_specs=[row, row],
        out_shape=[jax.ShapeDtypeStruct((m, n), F32), jax.ShapeDtypeStruct((m, n), BF16)],
        compiler_params=_params("parallel", "arbitrary"),
        name="proj_ln",
    )(a, w, resid, g, b)


ROUTE_E1, ROUTE_E2, ROUTE_G1, ROUTE_G2, ROUTE_R1, ROUTE_R2 = range(6)


def _router_kernel(h_ref, r_ref, o_ref, cnt_ref, *, ne):
    step = pl.program_id(0)

    @pl.when(step == 0)
    def _():
        cnt_ref[...] = jnp.zeros_like(cnt_ref)

    logits = jnp.dot(h_ref[...], r_ref[...], preferred_element_type=F32,
                     precision=lax.Precision.HIGHEST)
    tm = logits.shape[0]
    lane = lax.broadcasted_iota(jnp.int32, logits.shape, 1)
    neg = jnp.float32(-jnp.inf)
    x = jnp.where(lane < ne, logits, neg)
    m1 = jnp.max(x, axis=-1, keepdims=True)
    i1 = jnp.min(jnp.where(x == m1, lane, ne), axis=-1, keepdims=True)
    x2 = jnp.where(lane == i1, neg, x)
    m2 = jnp.max(x2, axis=-1, keepdims=True)
    i2 = jnp.min(jnp.where(x2 == m2, lane, ne), axis=-1, keepdims=True)
    e2 = jnp.exp(m2 - m1)
    g1 = 1.0 / (1.0 + e2)
    g2 = e2 / (1.0 + e2)

    hit1 = lane == i1
    hit2 = lane == i2
    onehot = jnp.where(hit1, 1.0, 0.0) + jnp.where(hit2, 1.0, 0.0)
    t_i = lax.broadcasted_iota(jnp.int32, (tm, tm), 0)
    s_i = lax.broadcasted_iota(jnp.int32, (tm, tm), 1)
    before = jnp.where(s_i < t_i, 1.0, 0.0).astype(BF16)
    prior = _dot(before, onehot.astype(BF16)) + cnt_ref[...]
    r1 = jnp.sum(jnp.where(hit1, prior, 0.0), axis=-1, keepdims=True)
    r2 = jnp.sum(jnp.where(hit2, prior, 0.0), axis=-1, keepdims=True)
    cnt_ref[...] += jnp.sum(onehot, axis=0, keepdims=True)

    rec = jnp.zeros(logits.shape, F32)
    for ln, val in ((ROUTE_E1, i1.astype(F32)), (ROUTE_E2, i2.astype(F32)), (ROUTE_G1, g1),
                    (ROUTE_G2, g2), (ROUTE_R1, r1), (ROUTE_R2, r2)):
        rec = jnp.where(lane == ln, val, rec)
    o_ref[...] = rec


def router_top2(h, router_padded, layer, ne, *, tm):
    m, k = h.shape
    return pl.pallas_call(
        functools.partial(_router_kernel, ne=ne),
        grid=(m // tm,),
        in_specs=[pl.BlockSpec((tm, k), lambda i: (i, 0)),
                  pl.BlockSpec((None, k, HEAD_DIM), lambda i: (layer, 0, 0))],
        out_specs=[pl.BlockSpec((tm, HEAD_DIM), lambda i: (i, 0)),
                   pl.BlockSpec((1, HEAD_DIM), lambda i: (0, 0))],
        out_shape=[jax.ShapeDtypeStruct((m, HEAD_DIM), F32), jax.ShapeDtypeStruct((1, HEAD_DIM), F32)],
        compiler_params=_params("arbitrary"),
        name="router",
    )(h, router_padded)


MOE_HALF = 512
MOE_ITEM = 2 * MOE_HALF


def _gather_rows_kernel(nrow_ref, src_ref, h_any, o_ref, buf, sem):
    half = pl.program_id(0)
    nrow = buf.shape[0]
    live = nrow_ref[half // 2] > (half % 2) * MOE_HALF

    @pl.when(live)
    def _():
        def start(r, c):
            pltpu.make_async_copy(h_any.at[pl.ds(src_ref[0, r], 1)], buf.at[pl.ds(r, 1)], sem).start()
            return c

        lax.fori_loop(0, nrow, start, 0, unroll=8)

        def wait(r, c):
            pltpu.make_async_copy(h_any.at[pl.ds(0, 1)], buf.at[pl.ds(r, 1)], sem).wait()
            return c

        lax.fori_loop(0, nrow, wait, 0, unroll=8)
        o_ref[...] = buf[...].astype(o_ref.dtype)

    @pl.when(jnp.logical_not(live))
    def _():
        o_ref[...] = jnp.zeros_like(o_ref)


def gather_rows(h, src, item_rows):
    nhalf = src.shape[0]
    d = h.shape[1]
    grid_spec = pltpu.PrefetchScalarGridSpec(
        num_scalar_prefetch=1,
        grid=(nhalf,),
        in_specs=[pl.BlockSpec((None, 1, MOE_HALF), lambda i, nr: (i, 0, 0), memory_space=pltpu.SMEM),
                  pl.BlockSpec(memory_space=pl.ANY)],
        out_specs=pl.BlockSpec((MOE_HALF, d), lambda i, nr: (i, 0)),
        scratch_shapes=[pltpu.VMEM((MOE_HALF, d), F32), pltpu.SemaphoreType.DMA(())],
    )
    return pl.pallas_call(
        _gather_rows_kernel,
        grid_spec=grid_spec,
        out_shape=jax.ShapeDtypeStruct((nhalf * MOE_HALF, d), BF16),
        compiler_params=_params("arbitrary"),
        name="moe_gather",
    )(item_rows, src, h)


def _grouped_swiglu_kernel(ie_ref, nrow_ref, x_ref, wg_ref, wu_ref, o_ref):
    nrow = nrow_ref[pl.program_id(1)]

    def act(x):
        g = _dot(x, wg_ref[...].astype(BF16))
        u = _dot(x, wu_ref[...].astype(BF16))
        return (g * _sigmoid(g) * u).astype(o_ref.dtype)

    @pl.when(nrow > MOE_HALF)
    def _():
        o_ref[...] = act(x_ref[...])

    @pl.when(nrow <= MOE_HALF)
    def _():
        o_ref[MOE_HALF:, :] = jnp.zeros((MOE_HALF, o_ref.shape[1]), o_ref.dtype)

    @pl.when(jnp.logical_and(nrow > 0, nrow <= MOE_HALF))
    def _():
        o_ref[:MOE_HALF, :] = act(x_ref[:MOE_HALF, :])

    @pl.when(nrow == 0)
    def _():
        o_ref[:MOE_HALF, :] = jnp.zeros((MOE_HALF, o_ref.shape[1]), o_ref.dtype)


def grouped_swiglu(xs, w1, layer, item_expert, item_rows, *, tn):
    rows, k = xs.shape
    nitem = rows // MOE_ITEM
    f = w1.shape[-1] // 2
    nj = f // tn
    grid_spec = pltpu.PrefetchScalarGridSpec(
        num_scalar_prefetch=2,
        grid=(nj, nitem),
        in_specs=[pl.BlockSpec((MOE_ITEM, k), lambda j, it, ie, nr: (jnp.where(nr[it] > 0, it, 0), 0)),
                  pl.BlockSpec((None, None, k, tn), lambda j, it, ie, nr: (layer, ie[it], 0, j)),
                  pl.BlockSpec((None, None, k, tn), lambda j, it, ie, nr: (layer, ie[it], 0, nj + j))],
        out_specs=pl.BlockSpec((MOE_ITEM, tn), lambda j, it, ie, nr: (it, j)),
    )
    return pl.pallas_call(
        _grouped_swiglu_kernel,
        grid_spec=grid_spec,
        out_shape=jax.ShapeDtypeStruct((rows, f), BF16),
        compiler_params=_params("arbitrary", "arbitrary"),
        name="moe_up",
    )(item_expert, item_rows, xs, w1, w1)


def _grouped_down_kernel(ie_ref, nrow_ref, a_ref, w_ref, o_ref):
    nrow = nrow_ref[pl.program_id(0)]

    @pl.when(pl.program_id(1) == 0)
    def _():
        o_ref[...] = jnp.zeros_like(o_ref)

    @pl.when(nrow > MOE_HALF)
    def _():
        o_ref[...] += _dot(a_ref[...], w_ref[...].astype(BF16))

    @pl.when(jnp.logical_and(nrow > 0, nrow <= MOE_HALF))
    def _():
        o_ref[:MOE_HALF, :] += _dot(a_ref[:MOE_HALF, :], w_ref[...].astype(BF16))


def grouped_down(mid, w2, layer, item_expert, item_rows, *, tk):
    rows, f = mid.shape
    nitem = rows // MOE_ITEM
    d = w2.shape[-1]
    nk = f // tk

    def kidx(it, kk, nr):
        return jnp.where(nr[it] > 0, kk, nk - 1)

    grid_spec = pltpu.PrefetchScalarGridSpec(
        num_scalar_prefetch=2,
        grid=(nitem, nk),
        in_specs=[pl.BlockSpec((MOE_ITEM, tk), lambda it, kk, ie, nr: (it, kidx(it, kk, nr))),
                  pl.BlockSpec((None, None, tk, d), lambda it, kk, ie, nr: (layer, ie[it], kidx(it, kk, nr), 0))],
        out_specs=pl.BlockSpec((MOE_ITEM, d), lambda it, kk, ie, nr: (it, 0)),
    )
    return pl.pallas_call(
        _grouped_down_kernel,
        grid_spec=grid_spec,
        out_shape=jax.ShapeDtypeStruct((rows, d), F32),
        compiler_params=_params("arbitrary", "arbitrary"),
        name="moe_down",
    )(item_expert, item_rows, mid, w2)


def _combine_ln_kernel(dst_ref, y_any, route_ref, r_ref, g_ref, b_ref, of_ref, ob_ref, buf, sem, *, alpha):
    tm = r_ref.shape[0]

    def start(i, c):
        for slot in range(TOP_K):
            pltpu.make_async_copy(y_any.at[pl.ds(dst_ref[0, slot * tm + i], 1)],
                                  buf.at[slot, pl.ds(i, 1)], sem).start()
        return c

    lax.fori_loop(0, tm, start, 0, unroll=8)

    def wait(i, c):
        for slot in range(TOP_K):
            pltpu.make_async_copy(y_any.at[pl.ds(0, 1)], buf.at[slot, pl.ds(i, 1)], sem).wait()
        return c

    lax.fori_loop(0, tm, wait, 0, unroll=8)
    route = route_ref[...]
    g1 = route[:, ROUTE_G1:ROUTE_G1 + 1]
    g2 = route[:, ROUTE_G2:ROUTE_G2 + 1]
    r = alpha * r_ref[...] + (g1 * buf[0] + g2 * buf[1])
    mu = jnp.mean(r, axis=-1, keepdims=True)
    c = r - mu
    var = jnp.mean(c * c, axis=-1, keepdims=True)
    y = c * lax.rsqrt(var + LN_EPS) * g_ref[...] + b_ref[...]
    of_ref[...] = y
    ob_ref[...] = y.astype(BF16)


def combine_ln(ys, dst, route, resid, g, b, layer, *, alpha, tm):
    m, d = resid.shape
    row = pl.BlockSpec((tm, d), lambda i: (i, 0))
    vec = pl.BlockSpec((None, 1, d), lambda i: (layer, 0, 0))
    return pl.pallas_call(
        functools.partial(_combine_ln_kernel, alpha=alpha),
        grid=(m // tm,),
        in_specs=[pl.BlockSpec((None, 1, TOP_K * tm), lambda i: (i, 0, 0), memory_space=pltpu.SMEM),
                  pl.BlockSpec(memory_space=pl.ANY),
                  pl.BlockSpec((tm, HEAD_DIM), lambda i: (i, 0)),
                  row, vec, vec],
        out_specs=[row, row],
        out_shape=[jax.ShapeDtypeStruct((m, d), F32), jax.ShapeDtypeStruct((m, d), BF16)],
        scratch_shapes=[pltpu.VMEM((TOP_K, tm, d), F32), pltpu.SemaphoreType.DMA(())],
        compiler_params=_params("arbitrary"),
        name="moe_combine_ln",
    )(dst, ys, route, resid, g, b)


def routed_ffn(h, router_padded, w1, w2, layer, g, b, ln_layer, ne, *, alpha, tm_route, tm_comb, tf):
    m, d = h.shape
    route, counts = router_top2(h, router_padded, layer, ne, tm=tm_route)
    cnt = counts[0, :ne].astype(jnp.int32)
    nitem_e = (cnt + MOE_ITEM - 1) // MOE_ITEM
    item_end = jnp.cumsum(nitem_e)
    item_start = item_end - nitem_e
    row_off = item_start * MOE_ITEM
    n_items = item_end[-1]
    max_items = (TOP_K * m + MOE_ITEM - 1) // MOE_ITEM + ne
    item = jnp.arange(max_items, dtype=jnp.int32)
    item_expert = jnp.minimum(jnp.searchsorted(item_end, item, side="right"), ne - 1).astype(jnp.int32)
    item_rows = jnp.clip(cnt[item_expert] - (item - item_start[item_expert]) * MOE_ITEM, 0, MOE_ITEM)
    item_rows = jnp.where(item < n_items, item_rows, 0).astype(jnp.int32)
    item_expert = jnp.where(item < n_items, item_expert, item_expert[jnp.maximum(n_items - 1, 0)])
    e1 = route[:, ROUTE_E1].astype(jnp.int32)
    e2 = route[:, ROUTE_E2].astype(jnp.int32)
    dst1 = row_off[e1] + route[:, ROUTE_R1].astype(jnp.int32)
    dst2 = row_off[e2] + route[:, ROUTE_R2].astype(jnp.int32)
    tok = jnp.arange(m, dtype=jnp.int32)
    src = jnp.zeros((max_items * MOE_ITEM,), jnp.int32).at[dst1].set(tok).at[dst2].set(tok)
    dst = jnp.concatenate([dst1.reshape(m // tm_comb, 1, tm_comb), dst2.reshape(m // tm_comb, 1, tm_comb)], axis=2)

    xs = gather_rows(h, src.reshape(2 * max_items, 1, MOE_HALF), item_rows)
    mid = grouped_swiglu(xs, w1, layer, item_expert, item_rows, tn=tf)
    ys = grouped_down(mid, w2, layer, item_expert, item_rows, tk=tf)
    return combine_ln(ys, dst, route, h, g, b, ln_layer, alpha=alpha, tm=tm_comb)


def _hgrn_kernel(q_ref, z_ref, v_ref, g_ref, lb_ref, gn_ref, s0_ref, o_ref, sout_ref,
                 st_s, qt_s, kd_s, vt_s, ebl_s, acc_s, *, heads, per_block_state):
    rows, width = q_ref.shape
    nblk = rows // HGRN_BLOCK
    step = pl.program_id(0)

    lb = lb_ref[...]
    log_lb = jnp.log(jnp.maximum(lb, LB_FLOOR))
    z = z_ref[...]
    lo = jnp.log1p(-lb) + _log_sigmoid(z)
    logf = jnp.maximum(log_lb, lo) + jnp.log1p(jnp.exp(-jnp.abs(log_lb - lo)))
    kin = (1.0 - lb) * _sigmoid(-z)
    qraw = q_ref[...]
    q = qraw * _sigmoid(qraw)
    v = v_ref[...]

    r_i = lax.broadcasted_iota(jnp.int32, (rows, rows), 0)
    c_i = lax.broadcasted_iota(jnp.int32, (rows, rows), 1)
    same = (r_i >> HGRN_BLOCK_LOG2) == (c_i >> HGRN_BLOCK_LOG2)
    tri = jnp.where(same & (c_i <= r_i), 1.0, 0.0).astype(BF16)
    blk = jnp.where(same, 1.0, 0.0).astype(BF16)
    parts = _split3(logf)
    b = _dot(tri, parts[0]) + _dot(tri, parts[1]) + _dot(tri, parts[2])
    btot = _dot(blk, parts[0]) + _dot(blk, parts[1]) + _dot(blk, parts[2])
    qt_s[...] = q * jnp.exp(b)
    kd_s[...] = kin * jnp.exp(btot - b)
    ebl_s[...] = jnp.exp(btot)
    vt_s[...] = v.T

    pos = lax.broadcasted_iota(jnp.int32, (rows, width), 0) & (HGRN_BLOCK - 1)
    acc = None
    for d in range(HGRN_BLOCK):
        if d == 0:
            w = q * kin
            vs = v
        else:
            decay = jnp.exp(b - pltpu.roll(b, d, 0))
            w = jnp.where(pos >= d, q * pltpu.roll(kin, d, 0) * decay, 0.0)
            vs = pltpu.roll(v, d, 0)
        att = jnp.concatenate(
            [jnp.broadcast_to(jnp.sum(w[:, h * HEAD_DIM:(h + 1) * HEAD_DIM], axis=1, keepdims=True),
                              (rows, HEAD_DIM)) for h in range(heads)], axis=1)
        acc = att * vs if acc is None else acc + att * vs
    acc_s[...] = acc

    if not per_block_state:
        @pl.when(step == 0)
        def _():
            st_s[...] = jnp.zeros_like(st_s)

    lane_blk = lax.broadcasted_iota(jnp.int32, (HEAD_DIM, rows), 1) >> HGRN_BLOCK_LOG2

    def block_step(c, carry):
        r0 = pl.multiple_of(c * HGRN_BLOCK, HGRN_BLOCK)
        for h in range(heads):
            hc = slice(h * HEAD_DIM, (h + 1) * HEAD_DIM)
            if per_block_state:
                st = s0_ref[c, h].T
            else:
                st = st_s[h]
            qc = qt_s[pl.ds(r0, HGRN_BLOCK), hc]
            acc_s[pl.ds(r0, HGRN_BLOCK), hc] += _dot_nt(qc.astype(BF16), st.astype(BF16))
            vmask = jnp.where(lane_blk == c, vt_s[hc, :], 0.0).astype(BF16)
            upd = _dot(vmask, kd_s[:, hc].astype(BF16))
            st_new = st * ebl_s[pl.ds(r0, 1), hc] + upd
            if per_block_state:
                sout_ref[c, h] = st_new.T
            else:
                st_s[h] = st_new
        return carry

    lax.fori_loop(0, nblk, block_step, 0)

    if not per_block_state:
        @pl.when(step == pl.num_programs(0) - 1)
        def _():
            for h in range(heads):
                sout_ref[0, h] = st_s[h].T

    o = acc_s[...]
    ms = jnp.concatenate(
        [jnp.broadcast_to(jnp.sum(jnp.square(o[:, h * HEAD_DIM:(h + 1) * HEAD_DIM]), axis=1, keepdims=True),
                          (rows, HEAD_DIM)) for h in range(heads)], axis=1) * (1.0 / HEAD_DIM)
    gate = g_ref[...]
    o_ref[...] = (o * lax.rsqrt(ms + RMS_EPS) * gn_ref[...] * (gate * _sigmoid(gate))).astype(o_ref.dtype)


def hgrn_mixer(p, row0, nrows, lb, gnorm, layer, s0, *, per_block_state, width):
    heads = width // HEAD_DIM
    tile = HGRN_TILE
    assert nrows % tile == 0 and row0 % tile == 0
    rb0 = row0 // tile
    nt = nrows // tile
    if per_block_state:
        assert nt == 1
        nstate = s0.shape[0]
    else:
        nstate = 1
        s0 = jnp.zeros((1, heads, HEAD_DIM, HEAD_DIM), F32)

    def col(cb):
        return pl.BlockSpec((tile, width), lambda i: (rb0 + i, cb))

    vec = pl.BlockSpec((None, 1, width), lambda i: (layer, 0, 0))
    state = pl.BlockSpec((nstate, heads, HEAD_DIM, HEAD_DIM), lambda i: (0, 0, 0, 0))
    return pl.pallas_call(
        functools.partial(_hgrn_kernel, heads=heads, per_block_state=per_block_state),
        grid=(nt,),
        in_specs=[col(0), col(1), col(2), col(3), vec, vec, state],
        out_specs=[pl.BlockSpec((tile, width), lambda i: (i, 0)), state],
        out_shape=[jax.ShapeDtypeStruct((nrows, width), BF16),
                   jax.ShapeDtypeStruct((nstate, heads, HEAD_DIM, HEAD_DIM), F32)],
        scratch_shapes=[pltpu.VMEM((heads, HEAD_DIM, HEAD_DIM), F32),
                        pltpu.VMEM((tile, width), F32),
                        pltpu.VMEM((tile, width), F32),
                        pltpu.VMEM((width, tile), F32),
                        pltpu.VMEM((tile, width), F32),
                        pltpu.VMEM((tile, width), F32)],
        compiler_params=_params("arbitrary"),
        name="hgrn2",
    )(p, p, p, p, lb, gnorm, s0)


def _pool_tile(xp, pos, w_ref, scale):
    sums = {1: xp}
    s = xp
    w = 1
    while w < POOL_WINDOWS[-1]:
        s = s + pltpu.roll(s, w, 0)
        w *= 2
        sums[w] = s
    gd = HEAD_DIM
    outs = []
    for g, win in enumerate(POOL_WINDOWS):
        sl = slice(g * gd, (g + 1) * gd)
        cnt = jnp.clip(pos + 1, 1, win).astype(F32)
        d = sums[win][:, sl] / cnt - xp[:, sl]
        outs.append(_dot(d.astype(BF16), w_ref[g].astype(BF16)))
    return jnp.concatenate(outs, axis=1) * scale


def _pool_stream_kernel(u_ref, w_ref, sc_ref, o_ref, xp_s):
    step = pl.program_id(0)
    rows = u_ref.shape[0]

    @pl.when(step == 0)
    def _():
        xp_s[0:POOL_HALO, :] = jnp.zeros((POOL_HALO, xp_s.shape[1]), F32)

    xp_s[POOL_HALO:, :] = u_ref[...]
    xp = xp_s[...]
    pos = step * rows - POOL_HALO + lax.broadcasted_iota(jnp.int32, (rows + POOL_HALO, 1), 0)
    y = _pool_tile(xp, pos, w_ref, sc_ref[...])
    o_ref[...] = y[POOL_HALO:, :].astype(o_ref.dtype)
    xp_s[0:POOL_HALO, :] = xp[rows:, :]


def _pool_batch_kernel(u_ref, buf_ref, w_ref, sc_ref, o_ref, xp_s, *, start):
    nb, t, c = u_ref.shape
    seg = POOL_HALO + t
    for bi in range(nb):
        xp_s[bi * seg:bi * seg + POOL_HALO, :] = buf_ref[bi]
        xp_s[bi * seg + POOL_HALO:(bi + 1) * seg, :] = u_ref[bi]
    r = lax.broadcasted_iota(jnp.int32, (nb * seg, 1), 0) & (seg - 1)
    y = _pool_tile(xp_s[...], start - POOL_HALO + r, w_ref, sc_ref[...])
    for bi in range(nb):
        o_ref[bi] = y[bi * seg + POOL_HALO:(bi + 1) * seg, :].astype(o_ref.dtype)


def pool_stream(p, nrows, col_block, w_pool, pool_scale, layer, *, width, tile):
    assert nrows % tile == 0
    return pl.pallas_call(
        _pool_stream_kernel,
        grid=(nrows // tile,),
        in_specs=[pl.BlockSpec((tile, width), lambda i: (i, col_block)),
                  pl.BlockSpec((None,) + w_pool.shape[1:], lambda i: (layer, 0, 0, 0)),
                  pl.BlockSpec((None, 1, width), lambda i: (layer, 0, 0))],
        out_specs=pl.BlockSpec((tile, width), lambda i: (i, 0)),
        out_shape=jax.ShapeDtypeStruct((nrows, width), BF16),
        scratch_shapes=[pltpu.VMEM((tile + POOL_HALO, width), F32)],
        compiler_params=_params("arbitrary"),
        name="pool_stream",
    )(p, w_pool, pool_scale)


def pool_batch(u, buf, w_pool, pool_scale, layer, *, start):
    nb, t, c = u.shape
    return pl.pallas_call(
        functools.partial(_pool_batch_kernel, start=start),
        grid=(1,),
        in_specs=[pl.BlockSpec((nb, t, c), lambda i: (0, 0, 0)),
                  pl.BlockSpec((nb, POOL_HALO, c), lambda i: (0, 0, 0)),
                  pl.BlockSpec((None,) + w_pool.shape[1:], lambda i: (layer, 0, 0, 0)),
                  pl.BlockSpec((None, 1, c), lambda i: (layer, 0, 0))],
        out_specs=pl.BlockSpec((nb, t, c), lambda i: (0, 0, 0)),
        out_shape=jax.ShapeDtypeStruct((nb, t, c), BF16),
        scratch_shapes=[pltpu.VMEM((nb * (POOL_HALO + t), c), F32)],
        compiler_params=_params("arbitrary"),
        name="pool_batch",
    )(u, buf, w_pool, pool_scale)


def _sb_blocks(qbs, kblks, vblks, masks, r_runs, upper, scale):
    zs = [_dot_nt(qb, kblk.astype(BF16)) * scale for qb, kblk in zip(qbs, kblks)]
    l1ms = []
    for z, mask in zip(zs, masks):
        l1m = _log_sigmoid(-z)
        l1ms.append(l1m if mask is None else jnp.where(mask, l1m, 0.0))
    parts = [_split3(l1m) for l1m in l1ms]
    laters = [_dot(p3[0], upper) + _dot(p3[1], upper) + _dot(p3[2], upper) for p3 in parts]
    avs = []
    for z, l1m, later, r_run, mask in zip(zs, l1ms, laters, r_runs, masks):
        a = jnp.exp(z + l1m + later + r_run)
        avs.append(a if mask is None else jnp.where(mask, a, 0.0))
    contribs = [_dot(a.astype(BF16), vblk.astype(BF16)) for a, vblk in zip(avs, vblks)]
    rowsums = [jnp.sum(l1m, axis=1, keepdims=True) for l1m in l1ms]
    return contribs, rowsums


def _strict_upper(tk):
    j = lax.broadcasted_iota(jnp.int32, (tk, tk), 0)
    s = lax.broadcasted_iota(jnp.int32, (tk, tk), 1)
    return jnp.where(j > s, 1.0, 0.0).astype(BF16)


def _sb_stream_kernel(q_ref, k_ref, v_ref, o_ref, acc_s, run_s, *, tk, scale):
    tq = q_ref.shape[0]
    nsub = tq // tk
    qi = pl.program_id(1)
    upper = _strict_upper(tk)
    qbs = [q_ref[s * tk:(s + 1) * tk, :].astype(BF16) for s in range(nsub)]
    row = lax.broadcasted_iota(jnp.int32, (tk, tk), 0)
    col = lax.broadcasted_iota(jnp.int32, (tk, tk), 1)

    diag0 = [pl.multiple_of((qi * nsub + s) * tk, tk) for s in range(nsub)]
    contribs, rowsums = _sb_blocks(qbs, [k_ref[pl.ds(k0, tk), :] for k0 in diag0],
                                   [v_ref[pl.ds(k0, tk), :] for k0 in diag0], [col < row] * nsub,
                                   [jnp.zeros((tk, 1), F32)] * nsub, upper, scale)
    live = None
    for s in range(nsub):
        acc_s[s * tk:(s + 1) * tk, :] = contribs[s]
        run_s[s * tk:(s + 1) * tk, :] = rowsums[s]
        m = jnp.max(rowsums[s])
        live = m if live is None else jnp.maximum(live, m)

    def cond(carry):
        i, live = carry
        return jnp.logical_and(i <= qi * nsub + nsub - 1, live > SB_DEAD_LOG)

    def body(carry):
        i, _ = carry
        kbs = [qi * nsub + s - i for s in range(nsub)]
        k0s = [pl.multiple_of(jnp.maximum(kb, 0) * tk, tk) for kb in kbs]
        r_runs = [run_s[s * tk:(s + 1) * tk, :] for s in range(nsub)]
        contribs, rowsums = _sb_blocks(qbs, [k_ref[pl.ds(k0, tk), :] for k0 in k0s],
                                       [v_ref[pl.ds(k0, tk), :] for k0 in k0s], [None] * nsub,
                                       r_runs, upper, scale)
        live = None
        for s in range(nsub):
            has_keys = kbs[s] >= 0
            rows = slice(s * tk, (s + 1) * tk)
            acc_s[rows, :] += jnp.where(has_keys, contribs[s], 0.0)
            r_new = jnp.where(has_keys, r_runs[s] + rowsums[s], 2.0 * SB_DEAD_LOG)
            run_s[rows, :] = r_new
            m = jnp.max(r_new)
            live = m if live is None else jnp.maximum(live, m)
        return i + 1, live

    lax.while_loop(cond, body, (jnp.int32(1), live))
    o_ref[...] = acc_s[...].astype(o_ref.dtype)


def sb_stream(p, nrows, col_q, col_k, col_v, *, heads, tq, tk):
    assert nrows % tq == 0 and tq % tk == 0
    scale = HEAD_DIM ** -0.5
    return pl.pallas_call(
        functools.partial(_sb_stream_kernel, tk=tk, scale=scale),
        grid=(heads, nrows // tq),
        in_specs=[pl.BlockSpec((tq, HEAD_DIM), lambda h, i: (i, col_q + h)),
                  pl.BlockSpec((nrows, HEAD_DIM), lambda h, i: (0, col_k + h)),
                  pl.BlockSpec((nrows, HEAD_DIM), lambda h, i: (0, col_v + h))],
        out_specs=pl.BlockSpec((tq, HEAD_DIM), lambda h, i: (i, h)),
        out_shape=jax.ShapeDtypeStruct((nrows, heads * HEAD_DIM), BF16),
        scratch_shapes=[pltpu.VMEM((tq, HEAD_DIM), F32), pltpu.VMEM((tq, 1), F32)],
        compiler_params=_params("parallel", "arbitrary"),
        name="sb_stream",
    )(p, p, p)


def _head_cols(pair_refs, h):
    return pair_refs[h // 2][:, (h % 2) * HEAD_DIM:(h % 2 + 1) * HEAD_DIM]


def _sb_past_chunk(qbs, kp_ref, vp_ref, accs, r_runs, upper, tk, scale):
    heads = len(qbs)
    for j in reversed(range(kp_ref.shape[1] // tk)):
        contribs, rowsums = _sb_blocks(qbs, [kp_ref[h, j * tk:(j + 1) * tk, :] for h in range(heads)],
                                       [vp_ref[h, j * tk:(j + 1) * tk, :] for h in range(heads)],
                                       [None] * heads, r_runs, upper, scale)
        accs = [a + c for a, c in zip(accs, contribs)]
        r_runs = [r + s for r, s in zip(r_runs, rowsums)]
    return accs, r_runs


def _sb_decode_first_kernel(*refs, heads, tk, scale):
    npair = heads // 2
    q_refs, kn_refs, vn_refs = refs[0:npair], refs[npair:2 * npair], refs[2 * npair:3 * npair]
    kp_ref, vp_ref, o_ref, acc_ref, run_ref, live_ref, kpad_s, vpad_s = refs[3 * npair:]
    t = o_ref.shape[0]
    upper = _strict_upper(tk)
    kpad_s[...] = jnp.zeros_like(kpad_s)
    vpad_s[...] = jnp.zeros_like(vpad_s)
    row = lax.broadcasted_iota(jnp.int32, (t, tk), 0)
    col = lax.broadcasted_iota(jnp.int32, (t, tk), 1)
    qbs = [_head_cols(q_refs, h).astype(BF16) for h in range(heads)]
    for h in range(heads):
        kpad_s[h, 0:t, :] = _head_cols(kn_refs, h)
        vpad_s[h, 0:t, :] = _head_cols(vn_refs, h)
    accs, r_runs = _sb_blocks(qbs, [kpad_s[h] for h in range(heads)], [vpad_s[h] for h in range(heads)],
                              [col < row] * heads, [jnp.zeros((t, 1), F32)] * heads, upper, scale)
    accs, r_runs = _sb_past_chunk(qbs, kp_ref, vp_ref, accs, r_runs, upper, tk, scale)
    live = None
    for h in range(heads):
        o_ref[:, h * HEAD_DIM:(h + 1) * HEAD_DIM] = accs[h].astype(o_ref.dtype)
        acc_ref[h] = accs[h]
        run_ref[h] = jnp.broadcast_to(r_runs[h], (t, HEAD_DIM))
        m = jnp.max(r_runs[h])
        live = m if live is None else jnp.maximum(live, m)
    live_ref[...] = jnp.full(live_ref.shape, live, F32)


def _sb_decode_rest_kernel(*refs, heads, tk, scale):
    npair = heads // 2
    q_refs = refs[0:npair]
    acc_in, run_in, kp_ref, vp_ref, o_ref, acc_s, run_s, live_s = refs[npair:]
    step = pl.program_id(1)
    upper = _strict_upper(tk)

    @pl.when(step == 0)
    def _():
        acc_s[...] = acc_in[...]
        run_s[...] = run_in[...]
        live_s[0] = jnp.max(run_in[...])

    @pl.when(live_s[0] > SB_DEAD_LOG)
    def _():
        qbs = [_head_cols(q_refs, h).astype(BF16) for h in range(heads)]
        accs, r_runs = _sb_past_chunk(qbs, kp_ref, vp_ref, [acc_s[h] for h in range(heads)],
                                      [run_s[h][:, 0:1] for h in range(heads)], upper, tk, scale)
        live = None
        for h in range(heads):
            acc_s[h] = accs[h]
            run_s[h] = jnp.broadcast_to(r_runs[h], run_s.shape[1:])
            m = jnp.max(r_runs[h])
            live = m if live is None else jnp.maximum(live, m)
        live_s[0] = live

    @pl.when(step == pl.num_programs(1) - 1)
    def _():
        for h in range(heads):
            o_ref[:, h * HEAD_DIM:(h + 1) * HEAD_DIM] = acc_s[h].astype(o_ref.dtype)


def sb_decode(p, row0, nb, t, col_q, col_k, col_v, k_past, v_past, layer, *, heads, tk, tp_first, tp_rest):
    npast = k_past.shape[3]
    nrest = (npast - tp_first) // tp_rest
    assert row0 % t == 0 and t <= tk and heads % 2 == 0 and tp_first % tk == 0 and tp_rest % tk == 0
    assert npast % tp_first == 0 and nrest * tp_rest == npast - tp_first
    assert col_q % 2 == 0 and col_k % 2 == 0 and col_v % 2 == 0
    rb0 = row0 // t
    npair = heads // 2
    scale = HEAD_DIM ** -0.5
    width = heads * HEAD_DIM
    o_shape = jax.ShapeDtypeStruct((nb * t, width), BF16)
    state_shape = jax.ShapeDtypeStruct((nb, heads, t, HEAD_DIM), F32)

    def new(cb, nax):
        if nax == 1:
            return [pl.BlockSpec((t, 2 * HEAD_DIM), lambda b, j=j: (rb0 + b, cb // 2 + j)) for j in range(npair)]
        return [pl.BlockSpec((t, 2 * HEAD_DIM), lambda b, c, j=j: (rb0 + b, cb // 2 + j)) for j in range(npair)]

    newest = pl.BlockSpec((None, None, heads, tp_first, HEAD_DIM),
                          lambda b: (layer, b, 0, npast // tp_first - 1, 0))
    o_first, acc, run, live = pl.pallas_call(
        functools.partial(_sb_decode_first_kernel, heads=heads, tk=tk, scale=scale),
        grid=(nb,),
        in_specs=new(col_q, 1) + new(col_k, 1) + new(col_v, 1) + [newest, newest],
        out_specs=[pl.BlockSpec((t, width), lambda b: (b, 0)),
                   pl.BlockSpec((None, heads, t, HEAD_DIM), lambda b: (b, 0, 0, 0)),
                   pl.BlockSpec((None, heads, t, HEAD_DIM), lambda b: (b, 0, 0, 0)),
                   pl.BlockSpec((None, 8, HEAD_DIM), lambda b: (b, 0, 0))],
        out_shape=[o_shape, state_shape, state_shape, jax.ShapeDtypeStruct((nb, 8, HEAD_DIM), F32)],
        scratch_shapes=[pltpu.VMEM((heads, tk, HEAD_DIM), F32), pltpu.VMEM((heads, tk, HEAD_DIM), F32)],
        compiler_params=_params("parallel"),
        name="sb_decode_first",
    )(*([p] * (3 * npair)), k_past, v_past)
    if nrest == 0:
        return o_first

    def older():
        state = pl.BlockSpec((None, heads, t, HEAD_DIM), lambda b, c: (b, 0, 0, 0))
        chunk = pl.BlockSpec((None, None, heads, tp_rest, HEAD_DIM), lambda b, c: (layer, b, 0, nrest - 1 - c, 0))
        return pl.pallas_call(
            functools.partial(_sb_decode_rest_kernel, heads=heads, tk=tk, scale=scale),
            grid=(nb, nrest),
            in_specs=new(col_q, 2) + [state, state, chunk, chunk],
            out_specs=pl.BlockSpec((t, width), lambda b, c: (b, 0)),
            out_shape=o_shape,
            scratch_shapes=[pltpu.VMEM((heads, t, HEAD_DIM), F32), pltpu.VMEM((heads, t, HEAD_DIM), F32),
                            pltpu.SMEM((1,), F32)],
            compiler_params=_params("parallel", "arbitrary"),
            name="sb_decode_rest",
        )(*([p] * npair), acc, run, k_past, v_past)

    return lax.cond(jnp.max(live) > SB_DEAD_LOG, older, lambda: o_first)


def kernel(x_prompt, x_sample, cache_sb_k, cache_sb_v, state_hgrn, state_pool, w_in, lb_param, hgrn_norm,
           w_pool, pool_scale, w_out, ln1_g, ln1_b, ln2_g, ln2_b, ffn_w1, ffn_w2, moe_router, moe_w1, moe_w2):
    bp, seq, d_model = x_prompt.shape
    nb, dseq, _ = x_sample.shape
    depth = w_in.shape[0]
    past_len = cache_sb_k.shape[2]
    a_heads = state_hgrn.shape[2]
    c_heads = cache_sb_k.shape[3]
    a_width = a_heads * HEAD_DIM
    c_width = c_heads * HEAD_DIM
    b_width = state_pool.shape[3]
    ne = moe_router.shape[2]
    d_ff = ffn_w2.shape[1]
    assert bp == 1 and a_width == c_width
    alpha = (2 * depth) ** 0.25
    np_rows = bp * seq
    ns_rows = nb * dseq
    m = np_rows + ns_rows

    pool_cb = 4 * a_width // b_width
    cq = (4 * a_width + b_width) // HEAD_DIM
    ck = cq + c_heads
    cv = ck + c_heads
    k_col0 = ck * HEAD_DIM
    v_col0 = cv * HEAD_DIM
    u_col0 = 4 * a_width

    pl_soft = jax.nn.softmax(lb_param.astype(F32), axis=0)
    lb_all = (jnp.cumsum(pl_soft, axis=0) - pl_soft[0])[:, None, :]
    gnorm = hgrn_norm.reshape(depth, 1, a_width)
    pscale = pool_scale.reshape(depth, 1, b_width)
    vec = lambda a: a.reshape(depth, 1, d_model)
    ln1g, ln1b, ln2g, ln2b = vec(ln1_g), vec(ln1_b), vec(ln2_g), vec(ln2_b)
    router_p = jnp.pad(moe_router, ((0, 0), (0, 0), (0, HEAD_DIM - ne)))
    k_cache = jnp.transpose(cache_sb_k, (0, 1, 3, 2, 4))
    v_cache = jnp.transpose(cache_sb_v, (0, 1, 3, 2, 4))
    pool_hist = jnp.pad(state_pool, ((0, 0), (0, 0), (POOL_HALO - state_pool.shape[2], 0), (0, 0)))
    w_out2 = w_out.reshape(depth * d_model, d_model)
    ffn_w2_2 = ffn_w2.reshape(-1, d_model)

    x = jnp.concatenate([x_prompt.reshape(np_rows, d_model), x_sample.reshape(ns_rows, d_model)], axis=0)
    xb = x.astype(BF16)

    outs = {k: [] for k in ("ps", "pb", "pk", "pv", "ss", "sb", "sk", "sv")}
    for l in range(depth):
        p = matmul_layer(xb, w_in, l, tm=2112, tn=256)

        oa_p, st_p = hgrn_mixer(p, 0, np_rows, lb_all, gnorm, l, None, per_block_state=False, width=a_width)
        oa_s, st_s = hgrn_mixer(p, np_rows, ns_rows, lb_all, gnorm, l, state_hgrn[l],
                                per_block_state=True, width=a_width)
        ob_p = pool_stream(p, np_rows, pool_cb, w_pool, pscale, l, width=b_width, tile=512)
        u_s = p[np_rows:, u_col0:u_col0 + b_width].reshape(nb, dseq, b_width)
        ob_s = pool_batch(u_s, pool_hist[l], w_pool, pscale, l, start=past_len).reshape(ns_rows, b_width)
        oc_p = sb_stream(p, np_rows, cq, ck, cv, heads=c_heads, tq=1024, tk=128)
        oc_s = sb_decode(p, np_rows, nb, dseq, cq, ck, cv, k_cache, v_cache, l,
                         heads=c_heads, tk=128, tp_first=256, tp_rest=768)

        mix_in = jnp.concatenate([jnp.concatenate([oa_p, ob_p, oc_p], axis=1),
                                  jnp.concatenate([oa_s, ob_s, oc_s], axis=1)], axis=0)
        h, hb = matmul_residual_ln(mix_in, w_out2, l * d_model, x, ln1g, ln1b, l, alpha=alpha, tm=528, tk=1024)

        if l % 2 == 0:
            mid = swiglu_up(hb, ffn_w1, l // 2, tm=1056, tn=512)
            x, xb = matmul_residual_ln(mid, ffn_w2_2, (l // 2) * d_ff, h, ln2g, ln2b, l,
                                       alpha=alpha, tm=704, tk=512)
        else:
            x, xb = routed_ffn(h, router_p, moe_w1, moe_w2, l // 2, ln2g, ln2b, l, ne, alpha=alpha,
                               tm_route=528, tm_comb=256, tf=512)

        outs["ps"].append(st_p)
        outs["pb"].append(p[np_rows - (POOL_HALO - 1):np_rows, u_col0:u_col0 + b_width][None])
        outs["pk"].append(p[:np_rows, k_col0:k_col0 + c_width].reshape(bp, seq, c_heads, HEAD_DIM))
        outs["pv"].append(p[:np_rows, v_col0:v_col0 + c_width].reshape(bp, seq, c_heads, HEAD_DIM))
        outs["ss"].append(st_s)
        outs["sb"].append(u_s[:, dseq - (POOL_HALO - 1):, :])
        outs["sk"].append(p[np_rows:, k_col0:k_col0 + c_width].reshape(nb, dseq, c_heads, HEAD_DIM))
        outs["sv"].append(p[np_rows:, v_col0:v_col0 + c_width].reshape(nb, dseq, c_heads, HEAD_DIM))

    stack = lambda k: jnp.stack(outs[k])
    return (x[:np_rows].reshape(bp, seq, d_model), x[np_rows:].reshape(nb, dseq, d_model),
            stack("ps"), stack("pb"), stack("pk"), stack("pv"),
            stack("ss"), stack("sb"), stack("sk"), stack("sv"))
```

```python
import functools

import jax
import jax.numpy as jnp
from jax import lax
from jax.experimental import pallas as pl
from jax.experimental.pallas import tpu as pltpu

F32 = jnp.float32
BF16 = jnp.bfloat16

HEAD_DIM = 128
HGRN_BLOCK_LOG2 = 4
HGRN_BLOCK = 1 << HGRN_BLOCK_LOG2
HGRN_TILE = 256
POOL_WINDOWS = (2, 4, 8, 16)
POOL_HALO = 16
TOP_K = 2
LN_EPS = 1e-5
RMS_EPS = 1e-6
LB_FLOOR = 1e-30
SB_DEAD_LOG = -104.0
VMEM_LIMIT_BYTES = 56 * 1024 * 1024


def _params(*sem):
    return pltpu.CompilerParams(dimension_semantics=sem, vmem_limit_bytes=VMEM_LIMIT_BYTES)


def _dot(a, b):
    return jnp.dot(a, b, preferred_element_type=F32)


def _dot_nt(a, b):
    return lax.dot_general(a, b, (((1,), (1,)), ((), ())), preferred_element_type=F32)


def _split3(x):
    h1 = x.astype(BF16)
    r1 = x - h1.astype(F32)
    h2 = r1.astype(BF16)
    h3 = (r1 - h2.astype(F32)).astype(BF16)
    return h1, h2, h3


def _sigmoid(x):
    return 1.0 / (1.0 + jnp.exp(-x))


def _log_sigmoid(x):
    return jnp.minimum(x, 0.0) - jnp.log1p(jnp.exp(-jnp.abs(x)))


def _mm_kernel(a_ref, w_ref, o_ref):
    o_ref[...] = _dot(a_ref[...], w_ref[...].astype(BF16)).astype(o_ref.dtype)


def matmul_layer(a, w, layer, *, tm, tn):
    m, k = a.shape
    n = w.shape[2]
    assert m % tm == 0 and n % tn == 0
    return pl.pallas_call(
        _mm_kernel,
        grid=(m // tm, n // tn),
        in_specs=[pl.BlockSpec((tm, k), lambda i, j: (i, 0)),
                  pl.BlockSpec((None, k, tn), lambda i, j: (layer, 0, j))],
        out_specs=pl.BlockSpec((tm, tn), lambda i, j: (i, j)),
        out_shape=jax.ShapeDtypeStruct((m, n), F32),
        compiler_params=_params("parallel", "arbitrary"),
        name="in_proj",
    )(a, w)


def _swiglu_kernel(x_ref, wg_ref, wu_ref, o_ref):
    x = x_ref[...]
    g = _dot(x, wg_ref[...].astype(BF16))
    u = _dot(x, wu_ref[...].astype(BF16))
    o_ref[...] = (g * _sigmoid(g) * u).astype(o_ref.dtype)


def swiglu_up(x, w1, layer, *, tm, tn):
    m, k = x.shape
    f = w1.shape[-1] // 2
    assert m % tm == 0 and f % tn == 0
    nj = f // tn
    return pl.pallas_call(
        _swiglu_kernel,
        grid=(m // tm, nj),
        in_specs=[pl.BlockSpec((tm, k), lambda i, j: (i, 0)),
                  pl.BlockSpec((None, k, tn), lambda i, j: (layer, 0, j)),
                  pl.BlockSpec((None, k, tn), lambda i, j: (layer, 0, nj + j))],
        out_specs=pl.BlockSpec((tm, tn), lambda i, j: (i, j)),
        out_shape=jax.ShapeDtypeStruct((m, f), BF16),
        compiler_params=_params("parallel", "arbitrary"),
        name="swiglu_up",
    )(x, w1, w1)


def _mm_ln_kernel(a_ref, w_ref, r_ref, g_ref, b_ref, of_ref, ob_ref, *, alpha, nk):
    kk = pl.program_id(1)

    @pl.when(kk == 0)
    def _():
        of_ref[...] = jnp.zeros_like(of_ref)

    of_ref[...] += _dot(a_ref[...], w_ref[...].astype(BF16))

    @pl.when(kk == nk - 1)
    def _():
        r = alpha * r_ref[...] + of_ref[...]
        mu = jnp.mean(r, axis=-1, keepdims=True)
        c = r - mu
        var = jnp.mean(c * c, axis=-1, keepdims=True)
        y = c * lax.rsqrt(var + LN_EPS) * g_ref[...] + b_ref[...]
        of_ref[...] = y
        ob_ref[...] = y.astype(BF16)


def matmul_residual_ln(a, w, w_row0, resid, g, b, layer, *, alpha, tm, tk):
    m, k = a.shape
    n = w.shape[1]
    assert m % tm == 0 and k % tk == 0 and w_row0 % tk == 0
    nk = k // tk
    kb0 = w_row0 // tk
    row = pl.BlockSpec((tm, n), lambda i, kk: (i, 0))
    vec = pl.BlockSpec((None, 1, n), lambda i, kk: (layer, 0, 0))
    return pl.pallas_call(
        functools.partial(_mm_ln_kernel, alpha=alpha, nk=nk),
        grid=(m // tm, nk),
        in_specs=[pl.BlockSpec((tm, tk), lambda i, kk: (i, kk)),
                  pl.BlockSpec((tk, n), lambda i, kk: (kb0 + kk, 0)),
                  row, vec, vec],
        out_specs=[row, row],
        out_shape=[jax.ShapeDtypeStruct((m, n), F32), jax.ShapeDtypeStruct((m, n), BF16)],
        compiler_params=_params("parallel", "arbitrary"),
        name="proj_ln",
    )(a, w, resid, g, b)


ROUTE_E1, ROUTE_E2, ROUTE_G1, ROUTE_G2, ROUTE_R1, ROUTE_R2 = range(6)


def _router_kernel(h_ref, r_ref, o_ref, cnt_ref, *, ne):
    step = pl.program_id(0)

    @pl.when(step == 0)
    def _():
        cnt_ref[...] = jnp.zeros_like(cnt_ref)

    logits = jnp.dot(h_ref[...], r_ref[...], preferred_element_type=F32,
                     precision=lax.Precision.HIGHEST)
    tm = logits.shape[0]
    lane = lax.broadcasted_iota(jnp.int32, logits.shape, 1)
    neg = jnp.float32(-jnp.inf)
    x = jnp.where(lane < ne, logits, neg)
    m1 = jnp.max(x, axis=-1, keepdims=True)
    i1 = jnp.min(jnp.where(x == m1, lane, ne), axis=-1, keepdims=True)
    x2 = jnp.where(lane == i1, neg, x)
    m2 = jnp.max(x2, axis=-1, keepdims=True)
    i2 = jnp.min(jnp.where(x2 == m2, lane, ne), axis=-1, keepdims=True)
    e2 = jnp.exp(m2 - m1)
    g1 = 1.0 / (1.0 + e2)
    g2 = e2 / (1.0 + e2)

    hit1 = lane == i1
    hit2 = lane == i2
    onehot = jnp.where(hit1, 1.0, 0.0) + jnp.where(hit2, 1.0, 0.0)
    t_i = lax.broadcasted_iota(jnp.int32, (tm, tm), 0)
    s_i = lax.broadcasted_iota(jnp.int32, (tm, tm), 1)
    before = jnp.where(s_i < t_i, 1.0, 0.0).astype(BF16)
    prior = _dot(before, onehot.astype(BF16)) + cnt_ref[...]
    r1 = jnp.sum(jnp.where(hit1, prior, 0.0), axis=-1, keepdims=True)
    r2 = jnp.sum(jnp.where(hit2, prior, 0.0), axis=-1, keepdims=True)
    cnt_ref[...] += jnp.sum(onehot, axis=0, keepdims=True)

    rec = jnp.zeros(logits.shape, F32)
    for ln, val in ((ROUTE_E1, i1.astype(F32)), (ROUTE_E2, i2.astype(F32)), (ROUTE_G1, g1),
                    (ROUTE_G2, g2), (ROUTE_R1, r1), (ROUTE_R2, r2)):
        rec = jnp.where(lane == ln, val, rec)
    o_ref[...] = rec


def router_top2(h, router_padded, layer, ne, *, tm):
    m, k = h.shape
    return pl.pallas_call(
        functools.partial(_router_kernel, ne=ne),
        grid=(m // tm,),
        in_specs=[pl.BlockSpec((tm, k), lambda i: (i, 0)),
                  pl.BlockSpec((None, k, HEAD_DIM), lambda i: (layer, 0, 0))],
        out_specs=[pl.BlockSpec((tm, HEAD_DIM), lambda i: (i, 0)),
                   pl.BlockSpec((1, HEAD_DIM), lambda i: (0, 0))],
        out_shape=[jax.ShapeDtypeStruct((m, HEAD_DIM), F32), jax.ShapeDtypeStruct((1, HEAD_DIM), F32)],
        compiler_params=_params("arbitrary"),
        name="router",
    )(h, router_padded)


MOE_HALF = 512
MOE_ITEM = 2 * MOE_HALF


def _for_item_rows(nrow, compute):
    lo = 0
    for n in (MOE_ITEM // 8, MOE_ITEM // 4, MOE_ITEM // 2, MOE_ITEM):
        pl.when(jnp.logical_and(nrow > lo, nrow <= n))(functools.partial(compute, n))
        lo = n


def _gather_rows_kernel(nrow_ref, src_ref, h_any, o_ref, buf, sem):
    half = pl.program_id(0)
    nrow = buf.shape[0]
    live = nrow_ref[half // 2] > (half % 2) * MOE_HALF

    @pl.when(live)
    def _():
        def start(r, c):
            pltpu.make_async_copy(h_any.at[pl.ds(src_ref[0, r], 1)], buf.at[pl.ds(r, 1)], sem).start()
            return c

        lax.fori_loop(0, nrow, start, 0, unroll=8)

        def wait(r, c):
            pltpu.make_async_copy(h_any.at[pl.ds(0, 1)], buf.at[pl.ds(r, 1)], sem).wait()
            return c

        lax.fori_loop(0, nrow, wait, 0, unroll=8)
        o_ref[...] = buf[...].astype(o_ref.dtype)

    @pl.when(jnp.logical_not(live))
    def _():
        o_ref[...] = jnp.zeros_like(o_ref)


def gather_rows(h, src, item_rows):
    nhalf = src.shape[0]
    d = h.shape[1]
    grid_spec = pltpu.PrefetchScalarGridSpec(
        num_scalar_prefetch=1,
        grid=(nhalf,),
        in_specs=[pl.BlockSpec((None, 1, MOE_HALF), lambda i, nr: (i, 0, 0), memory_space=pltpu.SMEM),
                  pl.BlockSpec(memory_space=pl.ANY)],
        out_specs=pl.BlockSpec((MOE_HALF, d), lambda i, nr: (i, 0)),
        scratch_shapes=[pltpu.VMEM((MOE_HALF, d), F32), pltpu.SemaphoreType.DMA(())],
    )
    return pl.pallas_call(
        _gather_rows_kernel,
        grid_spec=grid_spec,
        out_shape=jax.ShapeDtypeStruct((nhalf * MOE_HALF, d), BF16),
        compiler_params=_params("arbitrary"),
        name="moe_gather",
    )(item_rows, src, h)


def _grouped_swiglu_kernel(ie_ref, nrow_ref, x_ref, wg_ref, wu_ref, o_ref):
    nrow = nrow_ref[pl.program_id(1)]

    def compute(n):
        x = x_ref[:n, :]
        g = _dot(x, wg_ref[...].astype(BF16))
        u = _dot(x, wu_ref[...].astype(BF16))
        o_ref[:n, :] = (g * _sigmoid(g) * u).astype(o_ref.dtype)
        if n < MOE_ITEM:
            o_ref[n:, :] = jnp.zeros((MOE_ITEM - n, o_ref.shape[1]), o_ref.dtype)

    _for_item_rows(nrow, compute)

    @pl.when(nrow == 0)
    def _():
        o_ref[...] = jnp.zeros_like(o_ref)


def grouped_swiglu(xs, w1, layer, item_expert, item_rows, *, tn):
    rows, k = xs.shape
    nitem = rows // MOE_ITEM
    f = w1.shape[-1] // 2
    nj = f // tn
    grid_spec = pltpu.PrefetchScalarGridSpec(
        num_scalar_prefetch=2,
        grid=(nj, nitem),
        in_specs=[pl.BlockSpec((MOE_ITEM, k), lambda j, it, ie, nr: (jnp.where(nr[it] > 0, it, 0), 0)),
                  pl.BlockSpec((None, None, k, tn), lambda j, it, ie, nr: (layer, ie[it], 0, j)),
                  pl.BlockSpec((None, None, k, tn), lambda j, it, ie, nr: (layer, ie[it], 0, nj + j))],
        out_specs=pl.BlockSpec((MOE_ITEM, tn), lambda j, it, ie, nr: (it, j)),
    )
    return pl.pallas_call(
        _grouped_swiglu_kernel,
        grid_spec=grid_spec,
        out_shape=jax.ShapeDtypeStruct((rows, f), BF16),
        compiler_params=_params("arbitrary", "arbitrary"),
        name="moe_up",
    )(item_expert, item_rows, xs, w1, w1)


def _grouped_down_kernel(ie_ref, nrow_ref, a_ref, w_ref, o_ref):
    nrow = nrow_ref[pl.program_id(0)]

    @pl.when(pl.program_id(1) == 0)
    def _():
        o_ref[...] = jnp.zeros_like(o_ref)

    def compute(n):
        o_ref[:n, :] += _dot(a_ref[:n, :], w_ref[...].astype(BF16))

    _for_item_rows(nrow, compute)


def grouped_down(mid, w2, layer, item_expert, item_rows, *, tk):
    rows, f = mid.shape
    nitem = rows // MOE_ITEM
    d = w2.shape[-1]
    nk = f // tk

    def kidx(it, kk, nr):
        return jnp.where(nr[it] > 0, kk, nk - 1)

    grid_spec = pltpu.PrefetchScalarGridSpec(
        num_scalar_prefetch=2,
        grid=(nitem, nk),
        in_specs=[pl.BlockSpec((MOE_ITEM, tk), lambda it, kk, ie, nr: (it, kidx(it, kk, nr))),
                  pl.BlockSpec((None, None, tk, d), lambda it, kk, ie, nr: (layer, ie[it], kidx(it, kk, nr), 0))],
        out_specs=pl.BlockSpec((MOE_ITEM, d), lambda it, kk, ie, nr: (it, 0)),
    )
    return pl.pallas_call(
        _grouped_down_kernel,
        grid_spec=grid_spec,
        out_shape=jax.ShapeDtypeStruct((rows, d), F32),
        compiler_params=_params("arbitrary", "arbitrary"),
        name="moe_down",
    )(item_expert, item_rows, mid, w2)


def _combine_ln_kernel(dst_ref, y_any, route_ref, r_ref, g_ref, b_ref, of_ref, ob_ref, buf, sem, *, alpha):
    tm = r_ref.shape[0]

    def start(i, c):
        for slot in range(TOP_K):
            pltpu.make_async_copy(y_any.at[pl.ds(dst_ref[0, slot * tm + i], 1)],
                                  buf.at[slot, pl.ds(i, 1)], sem).start()
        return c

    lax.fori_loop(0, tm, start, 0, unroll=8)

    def wait(i, c):
        for slot in range(TOP_K):
            pltpu.make_async_copy(y_any.at[pl.ds(0, 1)], buf.at[slot, pl.ds(i, 1)], sem).wait()
        return c

    lax.fori_loop(0, tm, wait, 0, unroll=8)
    route = route_ref[...]
    g1 = route[:, ROUTE_G1:ROUTE_G1 + 1]
    g2 = route[:, ROUTE_G2:ROUTE_G2 + 1]
    r = alpha * r_ref[...] + (g1 * buf[0] + g2 * buf[1])
    mu = jnp.mean(r, axis=-1, keepdims=True)
    c = r - mu
    var = jnp.mean(c * c, axis=-1, keepdims=True)
    y = c * lax.rsqrt(var + LN_EPS) * g_ref[...] + b_ref[...]
    of_ref[...] = y
    ob_ref[...] = y.astype(BF16)


def combine_ln(ys, dst, route, resid, g, b, layer, *, alpha, tm):
    m, d = resid.shape
    row = pl.BlockSpec((tm, d), lambda i: (i, 0))
    vec = pl.BlockSpec((None, 1, d), lambda i: (layer, 0, 0))
    return pl.pallas_call(
        functools.partial(_combine_ln_kernel, alpha=alpha),
        grid=(m // tm,),
        in_specs=[pl.BlockSpec((None, 1, TOP_K * tm), lambda i: (i, 0, 0), memory_space=pltpu.SMEM),
                  pl.BlockSpec(memory_space=pl.ANY),
                  pl.BlockSpec((tm, HEAD_DIM), lambda i: (i, 0)),
                  row, vec, vec],
        out_specs=[row, row],
        out_shape=[jax.ShapeDtypeStruct((m, d), F32), jax.ShapeDtypeStruct((m, d), BF16)],
        scratch_shapes=[pltpu.VMEM((TOP_K, tm, d), F32), pltpu.SemaphoreType.DMA(())],
        compiler_params=_params("arbitrary"),
        name="moe_combine_ln",
    )(dst, ys, route, resid, g, b)


def routed_ffn(h, router_padded, w1, w2, layer, g, b, ln_layer, ne, *, alpha, tm_route, tm_comb, tf):
    m, d = h.shape
    route, counts = router_top2(h, router_padded, layer, ne, tm=tm_route)
    cnt = counts[0, :ne].astype(jnp.int32)
    nitem_e = (cnt + MOE_ITEM - 1) // MOE_ITEM
    item_end = jnp.cumsum(nitem_e)
    item_start = item_end - nitem_e
    row_off = item_start * MOE_ITEM
    n_items = item_end[-1]
    max_items = (TOP_K * m + MOE_ITEM - 1) // MOE_ITEM + ne
    item = jnp.arange(max_items, dtype=jnp.int32)
    item_expert = jnp.minimum(jnp.searchsorted(item_end, item, side="right"), ne - 1).astype(jnp.int32)
    item_rows = jnp.clip(cnt[item_expert] - (item - item_start[item_expert]) * MOE_ITEM, 0, MOE_ITEM)
    item_rows = jnp.where(item < n_items, item_rows, 0).astype(jnp.int32)
    item_expert = jnp.where(item < n_items, item_expert, item_expert[jnp.maximum(n_items - 1, 0)])
    e1 = route[:, ROUTE_E1].astype(jnp.int32)
    e2 = route[:, ROUTE_E2].astype(jnp.int32)
    dst1 = row_off[e1] + route[:, ROUTE_R1].astype(jnp.int32)
    dst2 = row_off[e2] + route[:, ROUTE_R2].astype(jnp.int32)
    tok = jnp.arange(m, dtype=jnp.int32)
    src = jnp.zeros((max_items * MOE_ITEM,), jnp.int32).at[dst1].set(tok).at[dst2].set(tok)
    dst = jnp.concatenate([dst1.reshape(m // tm_comb, 1, tm_comb), dst2.reshape(m // tm_comb, 1, tm_comb)], axis=2)

    xs = gather_rows(h, src.reshape(2 * max_items, 1, MOE_HALF), item_rows)
    mid = grouped_swiglu(xs, w1, layer, item_expert, item_rows, tn=tf)
    ys = grouped_down(mid, w2, layer, item_expert, item_rows, tk=tf)
    return combine_ln(ys, dst, route, h, g, b, ln_layer, alpha=alpha, tm=tm_comb)


def _hgrn_kernel(q_ref, z_ref, v_ref, g_ref, lb_ref, gn_ref, s0_ref, o_ref, sout_ref,
                 st_s, q_s, b_s, c_s, qt_s, kd_s, vb_s, ebl_s, upd_s, sb_s, *, heads, per_block_state):
    rows, width = q_ref.shape
    nblk = rows // HGRN_BLOCK
    step = pl.program_id(0)

    lb = lb_ref[...]
    log_lb = jnp.log(jnp.maximum(lb, LB_FLOOR))
    log_1m_lb = jnp.log1p(-lb)
    z = z_ref[...]
    ls = _log_sigmoid(z)
    lo = log_1m_lb + ls
    logf = jnp.maximum(log_lb, lo) + jnp.log1p(jnp.exp(-jnp.abs(log_lb - lo)))
    logk = log_1m_lb + (ls - z)
    qraw = q_ref[...]
    q = qraw * _sigmoid(qraw)

    r_i = lax.broadcasted_iota(jnp.int32, (rows, rows), 0)
    c_i = lax.broadcasted_iota(jnp.int32, (rows, rows), 1)
    same = (r_i >> HGRN_BLOCK_LOG2) == (c_i >> HGRN_BLOCK_LOG2)
    tri = jnp.where(same & (c_i <= r_i), 1.0, 0.0).astype(BF16)
    blk = jnp.where(same, 1.0, 0.0).astype(BF16)
    parts = _split3(logf)
    b = _dot(tri, parts[0]) + _dot(tri, parts[1]) + _dot(tri, parts[2])
    btot = _dot(blk, parts[0]) + _dot(blk, parts[1]) + _dot(blk, parts[2])
    c = b - logk
    q_s[...] = q
    b_s[...] = b
    c_s[...] = c
    qt_s[...] = (q * jnp.exp(b)).astype(BF16)
    kd_s[...] = jnp.exp(btot - c).astype(BF16)
    ebl_s[...] = jnp.exp(btot)
    vb_s[...] = v_ref[...].astype(BF16)

    def block_input(blk_i, carry):
        rs = pl.ds(pl.multiple_of(blk_i * HGRN_BLOCK, HGRN_BLOCK), HGRN_BLOCK)
        for h in range(heads):
            hc = slice(h * HEAD_DIM, (h + 1) * HEAD_DIM)
            upd_s[blk_i, h] = lax.dot_general(vb_s[rs, hc], kd_s[rs, hc], (((0,), (0,)), ((), ())),
                                              preferred_element_type=F32)
        return carry

    lax.fori_loop(0, nblk, block_input, 0, unroll=4)

    if not per_block_state:
        @pl.when(step == 0)
        def _():
            st_s[...] = jnp.zeros_like(st_s)

    def recur(blk_i, carry):
        r0 = pl.multiple_of(blk_i * HGRN_BLOCK, HGRN_BLOCK)
        for h in range(heads):
            hc = slice(h * HEAD_DIM, (h + 1) * HEAD_DIM)
            st = s0_ref[blk_i, h].T if per_block_state else st_s[h]
            sb_s[blk_i, h] = st.astype(BF16)
            st_new = st * ebl_s[pl.ds(r0, 1), hc] + upd_s[blk_i, h]
            if per_block_state:
                sout_ref[blk_i, h] = st_new.T
            else:
                st_s[h] = st_new
        return carry

    lax.fori_loop(0, nblk, recur, 0)

    if not per_block_state:
        @pl.when(step == pl.num_programs(0) - 1)
        def _():
            for h in range(heads):
                sout_ref[0, h] = st_s[h].T

    pos = lax.broadcasted_iota(jnp.int32, (HGRN_BLOCK, width), 0)
    ones = jnp.ones((HEAD_DIM, HEAD_DIM), BF16)
    gn = gn_ref[...]

    per_trip = 2

    def block_out(trip, carry):
        blks = [trip * per_trip + u for u in range(per_trip)]
        rss = [pl.ds(pl.multiple_of(bi * HGRN_BLOCK, HGRN_BLOCK), HGRN_BLOCK) for bi in blks]
        accs = [jnp.concatenate([_dot_nt(qt_s[rs, h * HEAD_DIM:(h + 1) * HEAD_DIM], sb_s[bi, h])
                                 for h in range(heads)], axis=1) for bi, rs in zip(blks, rss)]
        walls = []
        for rs in rss:
            qb, bb, cb = q_s[rs, :], b_s[rs, :], c_s[rs, :]
            ws = []
            for d in range(HGRN_BLOCK):
                w = qb * jnp.exp(bb - (cb if d == 0 else pltpu.roll(cb, d, 0)))
                if d > 0:
                    w = jnp.where(pos >= d, w, 0.0)
                ws.append(w.astype(BF16))
            walls.append(jnp.concatenate(ws, axis=0))
        atts = [jnp.concatenate([_dot(wall[:, h * HEAD_DIM:(h + 1) * HEAD_DIM], ones) for h in range(heads)], axis=1)
                for wall in walls]
        for rs, acc, att in zip(rss, accs, atts):
            vb = v_ref[rs, :]
            for d in range(HGRN_BLOCK):
                acc = acc + att[d * HGRN_BLOCK:(d + 1) * HGRN_BLOCK, :] * (vb if d == 0 else pltpu.roll(vb, d, 0))
            ms = jnp.concatenate(
                [jnp.broadcast_to(jnp.sum(jnp.square(acc[:, h * HEAD_DIM:(h + 1) * HEAD_DIM]), axis=1, keepdims=True),
                                  (HGRN_BLOCK, HEAD_DIM)) for h in range(heads)], axis=1) * (1.0 / HEAD_DIM)
            gate = g_ref[rs, :]
            o_ref[rs, :] = (acc * lax.rsqrt(ms + RMS_EPS) * gn * (gate * _sigmoid(gate))).astype(o_ref.dtype)
        return carry

    lax.fori_loop(0, nblk // per_trip, block_out, 0)


def hgrn_mixer(p, row0, nrows, lb, gnorm, layer, s0, *, per_block_state, width):
    heads = width // HEAD_DIM
    tile = HGRN_TILE
    assert nrows % tile == 0 and row0 % tile == 0
    rb0 = row0 // tile
    nt = nrows // tile
    if per_block_state:
        assert nt == 1
        nstate = s0.shape[0]
    else:
        nstate = 1
        s0 = jnp.zeros((1, heads, HEAD_DIM, HEAD_DIM), F32)

    def col(cb):
        return pl.BlockSpec((tile, width), lambda i: (rb0 + i, cb))

    vec = pl.BlockSpec((None, 1, width), lambda i: (layer, 0, 0))
    state = pl.BlockSpec((nstate, heads, HEAD_DIM, HEAD_DIM), lambda i: (0, 0, 0, 0))
    return pl.pallas_call(
        functools.partial(_hgrn_kernel, heads=heads, per_block_state=per_block_state),
        grid=(nt,),
        in_specs=[col(0), col(1), col(2), col(3), vec, vec, state],
        out_specs=[pl.BlockSpec((tile, width), lambda i: (i, 0)), state],
        out_shape=[jax.ShapeDtypeStruct((nrows, width), BF16),
                   jax.ShapeDtypeStruct((nstate, heads, HEAD_DIM, HEAD_DIM), F32)],
        scratch_shapes=[pltpu.VMEM((heads, HEAD_DIM, HEAD_DIM), F32),
                        pltpu.VMEM((tile, width), F32),
                        pltpu.VMEM((tile, width), F32),
                        pltpu.VMEM((tile, width), F32),
                        pltpu.VMEM((tile, width), BF16),
                        pltpu.VMEM((tile, width), BF16),
                        pltpu.VMEM((tile, width), BF16),
                        pltpu.VMEM((tile, width), F32),
                        pltpu.VMEM((tile // HGRN_BLOCK, heads, HEAD_DIM, HEAD_DIM), F32),
                        pltpu.VMEM((tile // HGRN_BLOCK, heads, HEAD_DIM, HEAD_DIM), BF16)],
        compiler_params=_params("arbitrary"),
        name="hgrn2",
    )(p, p, p, p, lb, gnorm, s0)


def _pool_tile(xp, pos, w_ref, scale):
    sums = {1: xp}
    s = xp
    w = 1
    while w < POOL_WINDOWS[-1]:
        s = s + pltpu.roll(s, w, 0)
        w *= 2
        sums[w] = s
    gd = HEAD_DIM
    outs = []
    for g, win in enumerate(POOL_WINDOWS):
        sl = slice(g * gd, (g + 1) * gd)
        cnt = jnp.clip(pos + 1, 1, win).astype(F32)
        d = sums[win][:, sl] / cnt - xp[:, sl]
        outs.append(_dot(d.astype(BF16), w_ref[g].astype(BF16)))
    return jnp.concatenate(outs, axis=1) * scale


def _pool_stream_kernel(u_ref, w_ref, sc_ref, o_ref, xp_s):
    step = pl.program_id(0)
    rows = u_ref.shape[0]

    @pl.when(step == 0)
    def _():
        xp_s[0:POOL_HALO, :] = jnp.zeros((POOL_HALO, xp_s.shape[1]), F32)

    xp_s[POOL_HALO:, :] = u_ref[...]
    xp = xp_s[...]
    pos = step * rows - POOL_HALO + lax.broadcasted_iota(jnp.int32, (rows + POOL_HALO, 1), 0)
    y = _pool_tile(xp, pos, w_ref, sc_ref[...])
    o_ref[...] = y[POOL_HALO:, :].astype(o_ref.dtype)
    xp_s[0:POOL_HALO, :] = xp[rows:, :]


def _pool_batch_kernel(u_ref, buf_ref, w_ref, sc_ref, o_ref, xp_s, *, start):
    nb, t, c = u_ref.shape
    seg = POOL_HALO + t
    for bi in range(nb):
        xp_s[bi * seg:bi * seg + POOL_HALO, :] = buf_ref[bi]
        xp_s[bi * seg + POOL_HALO:(bi + 1) * seg, :] = u_ref[bi]
    r = lax.broadcasted_iota(jnp.int32, (nb * seg, 1), 0) & (seg - 1)
    y = _pool_tile(xp_s[...], start - POOL_HALO + r, w_ref, sc_ref[...])
    for bi in range(nb):
        o_ref[bi] = y[bi * seg + POOL_HALO:(bi + 1) * seg, :].astype(o_ref.dtype)


def pool_stream(p, nrows, col_block, w_pool, pool_scale, layer, *, width, tile):
    assert nrows % tile == 0
    return pl.pallas_call(
        _pool_stream_kernel,
        grid=(nrows // tile,),
        in_specs=[pl.BlockSpec((tile, width), lambda i: (i, col_block)),
                  pl.BlockSpec((None,) + w_pool.shape[1:], lambda i: (layer, 0, 0, 0)),
                  pl.BlockSpec((None, 1, width), lambda i: (layer, 0, 0))],
        out_specs=pl.BlockSpec((tile, width), lambda i: (i, 0)),
        out_shape=jax.ShapeDtypeStruct((nrows, width), BF16),
        scratch_shapes=[pltpu.VMEM((tile + POOL_HALO, width), F32)],
        compiler_params=_params("arbitrary"),
        name="pool_stream",
    )(p, w_pool, pool_scale)


def pool_batch(u, buf, w_pool, pool_scale, layer, *, start):
    nb, t, c = u.shape
    return pl.pallas_call(
        functools.partial(_pool_batch_kernel, start=start),
        grid=(1,),
        in_specs=[pl.BlockSpec((nb, t, c), lambda i: (0, 0, 0)),
                  pl.BlockSpec((nb, POOL_HALO, c), lambda i: (0, 0, 0)),
                  pl.BlockSpec((None,) + w_pool.shape[1:], lambda i: (layer, 0, 0, 0)),
                  pl.BlockSpec((None, 1, c), lambda i: (layer, 0, 0))],
        out_specs=pl.BlockSpec((nb, t, c), lambda i: (0, 0, 0)),
        out_shape=jax.ShapeDtypeStruct((nb, t, c), BF16),
        scratch_shapes=[pltpu.VMEM((nb * (POOL_HALO + t), c), F32)],
        compiler_params=_params("arbitrary"),
        name="pool_batch",
    )(u, buf, w_pool, pool_scale)


def _sb_blocks(qbs, kblks, vblks, masks, r_runs, upper, scale):
    zs = [_dot_nt(qb, kblk.astype(BF16)) * scale for qb, kblk in zip(qbs, kblks)]
    l1ms = []
    for z, mask in zip(zs, masks):
        l1m = _log_sigmoid(-z)
        l1ms.append(l1m if mask is None else jnp.where(mask, l1m, 0.0))
    parts = [_split3(l1m) for l1m in l1ms]
    laters = [_dot(p3[0], upper) + _dot(p3[1], upper) + _dot(p3[2], upper) for p3 in parts]
    avs = []
    for z, l1m, later, r_run, mask in zip(zs, l1ms, laters, r_runs, masks):
        a = jnp.exp(z + l1m + later + r_run)
        avs.append(a if mask is None else jnp.where(mask, a, 0.0))
    contribs = [_dot(a.astype(BF16), vblk.astype(BF16)) for a, vblk in zip(avs, vblks)]
    rowsums = [jnp.sum(l1m, axis=1, keepdims=True) for l1m in l1ms]
    return contribs, rowsums


def _strict_upper(tk):
    j = lax.broadcasted_iota(jnp.int32, (tk, tk), 0)
    s = lax.broadcasted_iota(jnp.int32, (tk, tk), 1)
    return jnp.where(j > s, 1.0, 0.0).astype(BF16)


def _sb_stream_kernel(q_ref, k_ref, v_ref, o_ref, acc_s, run_s, *, tk, scale):
    tq = q_ref.shape[0]
    nsub = tq // tk
    qi = pl.program_id(1)
    upper = _strict_upper(tk)
    qbs = [q_ref[s * tk:(s + 1) * tk, :].astype(BF16) for s in range(nsub)]
    row = lax.broadcasted_iota(jnp.int32, (tk, tk), 0)
    col = lax.broadcasted_iota(jnp.int32, (tk, tk), 1)

    diag0 = [pl.multiple_of((qi * nsub + s) * tk, tk) for s in range(nsub)]
    contribs, rowsums = _sb_blocks(qbs, [k_ref[pl.ds(k0, tk), :] for k0 in diag0],
                                   [v_ref[pl.ds(k0, tk), :] for k0 in diag0], [col < row] * nsub,
                                   [jnp.zeros((tk, 1), F32)] * nsub, upper, scale)
    live = None
    for s in range(nsub):
        acc_s[s * tk:(s + 1) * tk, :] = contribs[s]
        run_s[s * tk:(s + 1) * tk, :] = rowsums[s]
        m = jnp.max(rowsums[s])
        live = m if live is None else jnp.maximum(live, m)

    def cond(carry):
        i, live = carry
        return jnp.logical_and(i <= qi * nsub + nsub - 1, live > SB_DEAD_LOG)

    def body(carry):
        i, _ = carry
        kbs = [qi * nsub + s - i for s in range(nsub)]
        k0s = [pl.multiple_of(jnp.maximum(kb, 0) * tk, tk) for kb in kbs]
        r_runs = [run_s[s * tk:(s + 1) * tk, :] for s in range(nsub)]
        contribs, rowsums = _sb_blocks(qbs, [k_ref[pl.ds(k0, tk), :] for k0 in k0s],
                                       [v_ref[pl.ds(k0, tk), :] for k0 in k0s], [None] * nsub,
                                       r_runs, upper, scale)
        live = None
        for s in range(nsub):
            has_keys = kbs[s] >= 0
            rows = slice(s * tk, (s + 1) * tk)
            acc_s[rows, :] += jnp.where(has_keys, contribs[s], 0.0)
            r_new = jnp.where(has_keys, r_runs[s] + rowsums[s], 2.0 * SB_DEAD_LOG)
            run_s[rows, :] = r_new
            m = jnp.max(r_new)
            live = m if live is None else jnp.maximum(live, m)
        return i + 1, live

    lax.while_loop(cond, body, (jnp.int32(1), live))
    o_ref[...] = acc_s[...].astype(o_ref.dtype)


def sb_stream(p, nrows, col_q, col_k, col_v, *, heads, tq, tk):
    assert nrows % tq == 0 and tq % tk == 0
    scale = HEAD_DIM ** -0.5
    return pl.pallas_call(
        functools.partial(_sb_stream_kernel, tk=tk, scale=scale),
        grid=(heads, nrows // tq),
        in_specs=[pl.BlockSpec((tq, HEAD_DIM), lambda h, i: (i, col_q + h)),
                  pl.BlockSpec((nrows, HEAD_DIM), lambda h, i: (0, col_k + h)),
                  pl.BlockSpec((nrows, HEAD_DIM), lambda h, i: (0, col_v + h))],
        out_specs=pl.BlockSpec((tq, HEAD_DIM), lambda h, i: (i, h)),
        out_shape=jax.ShapeDtypeStruct((nrows, heads * HEAD_DIM), BF16),
        scratch_shapes=[pltpu.VMEM((tq, HEAD_DIM), F32), pltpu.VMEM((tq, 1), F32)],
        compiler_params=_params("parallel", "arbitrary"),
        name="sb_stream",
    )(p, p, p)


def _head_cols(pair_refs, h):
    return pair_refs[h // 2][:, (h % 2) * HEAD_DIM:(h % 2 + 1) * HEAD_DIM]


def _sb_past_chunk(qbs, kp_ref, vp_ref, accs, r_runs, upper, tk, scale):
    heads = len(qbs)
    for j in reversed(range(kp_ref.shape[1] // tk)):
        contribs, rowsums = _sb_blocks(qbs, [kp_ref[h, j * tk:(j + 1) * tk, :] for h in range(heads)],
                                       [vp_ref[h, j * tk:(j + 1) * tk, :] for h in range(heads)],
                                       [None] * heads, r_runs, upper, scale)
        accs = [a + c for a, c in zip(accs, contribs)]
        r_runs = [r + s for r, s in zip(r_runs, rowsums)]
    return accs, r_runs


def _sb_decode_first_kernel(*refs, heads, tk, scale):
    npair = heads // 2
    q_refs, kn_refs, vn_refs = refs[0:npair], refs[npair:2 * npair], refs[2 * npair:3 * npair]
    kp_ref, vp_ref, o_ref, acc_ref, run_ref, live_ref, kpad_s, vpad_s = refs[3 * npair:]
    t = o_ref.shape[0]
    upper = _strict_upper(tk)
    kpad_s[...] = jnp.zeros_like(kpad_s)
    vpad_s[...] = jnp.zeros_like(vpad_s)
    row = lax.broadcasted_iota(jnp.int32, (t, tk), 0)
    col = lax.broadcasted_iota(jnp.int32, (t, tk), 1)
    qbs = [_head_cols(q_refs, h).astype(BF16) for h in range(heads)]
    for h in range(heads):
        kpad_s[h, 0:t, :] = _head_cols(kn_refs, h)
        vpad_s[h, 0:t, :] = _head_cols(vn_refs, h)
    accs, r_runs = _sb_blocks(qbs, [kpad_s[h] for h in range(heads)], [vpad_s[h] for h in range(heads)],
                              [col < row] * heads, [jnp.zeros((t, 1), F32)] * heads, upper, scale)
    accs, r_runs = _sb_past_chunk(qbs, kp_ref, vp_ref, accs, r_runs, upper, tk, scale)
    live = None
    for h in range(heads):
        o_ref[:, h * HEAD_DIM:(h + 1) * HEAD_DIM] = accs[h].astype(o_ref.dtype)
        acc_ref[h] = accs[h]
        run_ref[h] = jnp.broadcast_to(r_runs[h], (t, HEAD_DIM))
        m = jnp.max(r_runs[h])
        live = m if live is None else jnp.maximum(live, m)
    live_ref[...] = jnp.full(live_ref.shape, live, F32)


def _sb_decode_rest_kernel(*refs, heads, tk, scale):
    npair = heads // 2
    q_refs = refs[0:npair]
    acc_in, run_in, kp_ref, vp_ref, o_ref, acc_s, run_s, live_s = refs[npair:]
    step = pl.program_id(1)
    upper = _strict_upper(tk)

    @pl.when(step == 0)
    def _():
        acc_s[...] = acc_in[...]
        run_s[...] = run_in[...]
        live_s[0] = jnp.max(run_in[...])

    @pl.when(live_s[0] > SB_DEAD_LOG)
    def _():
        qbs = [_head_cols(q_refs, h).astype(BF16) for h in range(heads)]
        accs, r_runs = _sb_past_chunk(qbs, kp_ref, vp_ref, [acc_s[h] for h in range(heads)],
                                      [run_s[h][:, 0:1] for h in range(heads)], upper, tk, scale)
        live = None
        for h in range(heads):
            acc_s[h] = accs[h]
            run_s[h] = jnp.broadcast_to(r_runs[h], run_s.shape[1:])
            m = jnp.max(r_runs[h])
            live = m if live is None else jnp.maximum(live, m)
        live_s[0] = live

    @pl.when(step == pl.num_programs(1) - 1)
    def _():
        for h in range(heads):
            o_ref[:, h * HEAD_DIM:(h + 1) * HEAD_DIM] = acc_s[h].astype(o_ref.dtype)


def sb_decode(p, row0, nb, t, col_q, col_k, col_v, k_past, v_past, layer, *, heads, tk, tp_first, tp_rest):
    npast = k_past.shape[3]
    nrest = (npast - tp_first) // tp_rest
    assert row0 % t == 0 and t <= tk and heads % 2 == 0 and tp_first % tk == 0 and tp_rest % tk == 0
    assert npast % tp_first == 0 and nrest * tp_rest == npast - tp_first
    assert col_q % 2 == 0 and col_k % 2 == 0 and col_v % 2 == 0
    rb0 = row0 // t
    npair = heads // 2
    scale = HEAD_DIM ** -0.5
    width = heads * HEAD_DIM
    o_shape = jax.ShapeDtypeStruct((nb * t, width), BF16)
    state_shape = jax.ShapeDtypeStruct((nb, heads, t, HEAD_DIM), F32)

    def new(cb, nax):
        if nax == 1:
            return [pl.BlockSpec((t, 2 * HEAD_DIM), lambda b, j=j: (rb0 + b, cb // 2 + j)) for j in range(npair)]
        return [pl.BlockSpec((t, 2 * HEAD_DIM), lambda b, c, j=j: (rb0 + b, cb // 2 + j)) for j in range(npair)]

    newest = pl.BlockSpec((None, None, heads, tp_first, HEAD_DIM),
                          lambda b: (layer, b, 0, npast // tp_first - 1, 0))
    o_first, acc, run, live = pl.pallas_call(
        functools.partial(_sb_decode_first_kernel, heads=heads, tk=tk, scale=scale),
        grid=(nb,),
        in_specs=new(col_q, 1) + new(col_k, 1) + new(col_v, 1) + [newest, newest],
        out_specs=[pl.BlockSpec((t, width), lambda b: (b, 0)),
                   pl.BlockSpec((None, heads, t, HEAD_DIM), lambda b: (b, 0, 0, 0)),
                   pl.BlockSpec((None, heads, t, HEAD_DIM), lambda b: (b, 0, 0, 0)),
                   pl.BlockSpec((None, 8, HEAD_DIM), lambda b: (b, 0, 0))],
        out_shape=[o_shape, state_shape, state_shape, jax.ShapeDtypeStruct((nb, 8, HEAD_DIM), F32)],
        scratch_shapes=[pltpu.VMEM((heads, tk, HEAD_DIM), F32), pltpu.VMEM((heads, tk, HEAD_DIM), F32)],
        compiler_params=_params("parallel"),
        name="sb_decode_first",
    )(*([p] * (3 * npair)), k_past, v_past)
    if nrest == 0:
        return o_first

    def older():
        state = pl.BlockSpec((None, heads, t, HEAD_DIM), lambda b, c: (b, 0, 0, 0))
        chunk = pl.BlockSpec((None, None, heads, tp_rest, HEAD_DIM), lambda b, c: (layer, b, 0, nrest - 1 - c, 0))
        return pl.pallas_call(
            functools.partial(_sb_decode_rest_kernel, heads=heads, tk=tk, scale=scale),
            grid=(nb, nrest),
            in_specs=new(col_q, 2) + [state, state, chunk, chunk],
            out_specs=pl.BlockSpec((t, width), lambda b, c: (b, 0)),
            out_shape=o_shape,
            scratch_shapes=[pltpu.VMEM((heads, t, HEAD_DIM), F32), pltpu.VMEM((heads, t, HEAD_DIM), F32),
                            pltpu.SMEM((1,), F32)],
            compiler_params=_params("parallel", "arbitrary"),
            name="sb_decode_rest",
        )(*([p] * npair), acc, run, k_past, v_past)

    return lax.cond(jnp.max(live) > SB_DEAD_LOG, older, lambda: o_first)


def kernel(x_prompt, x_sample, cache_sb_k, cache_sb_v, state_hgrn, state_pool, w_in, lb_param, hgrn_norm,
           w_pool, pool_scale, w_out, ln1_g, ln1_b, ln2_g, ln2_b, ffn_w1, ffn_w2, moe_router, moe_w1, moe_w2):
    bp, seq, d_model = x_prompt.shape
    nb, dseq, _ = x_sample.shape
    depth = w_in.shape[0]
    past_len = cache_sb_k.shape[2]
    a_heads = state_hgrn.shape[2]
    c_heads = cache_sb_k.shape[3]
    a_width = a_heads * HEAD_DIM
    c_width = c_heads * HEAD_DIM
    b_width = state_pool.shape[3]
    ne = moe_router.shape[2]
    d_ff = ffn_w2.shape[1]
    assert bp == 1 and a_width == c_width
    alpha = (2 * depth) ** 0.25
    np_rows = bp * seq
    ns_rows = nb * dseq
    m = np_rows + ns_rows

    pool_cb = 4 * a_width // b_width
    cq = (4 * a_width + b_width) // HEAD_DIM
    ck = cq + c_heads
    cv = ck + c_heads
    k_col0 = ck * HEAD_DIM
    v_col0 = cv * HEAD_DIM
    u_col0 = 4 * a_width

    pl_soft = jax.nn.softmax(lb_param.astype(F32), axis=0)
    lb_all = (jnp.cumsum(pl_soft, axis=0) - pl_soft[0])[:, None, :]
    gnorm = hgrn_norm.reshape(depth, 1, a_width)
    pscale = pool_scale.reshape(depth, 1, b_width)
    vec = lambda a: a.reshape(depth, 1, d_model)
    ln1g, ln1b, ln2g, ln2b = vec(ln1_g), vec(ln1_b), vec(ln2_g), vec(ln2_b)
    router_p = jnp.pad(moe_router, ((0, 0), (0, 0), (0, HEAD_DIM - ne)))
    k_cache = jnp.transpose(cache_sb_k, (0, 1, 3, 2, 4))
    v_cache = jnp.transpose(cache_sb_v, (0, 1, 3, 2, 4))
    pool_hist = jnp.pad(state_pool, ((0, 0), (0, 0), (POOL_HALO - state_pool.shape[2], 0), (0, 0)))
    w_out2 = w_out.reshape(depth * d_model, d_model)
    ffn_w2_2 = ffn_w2.reshape(-1, d_model)

    x = jnp.concatenate([x_prompt.reshape(np_rows, d_model), x_sample.reshape(ns_rows, d_model)], axis=0)
    xb = x.astype(BF16)

    outs = {k: [] for k in ("ps", "pb", "pk", "pv", "ss", "sb", "sk", "sv")}
    for l in range(depth):
        p = matmul_layer(xb, w_in, l, tm=2112, tn=256)

        oa_p, st_p = hgrn_mixer(p, 0, np_rows, lb_all, gnorm, l, None, per_block_state=False, width=a_width)
        oa_s, st_s = hgrn_mixer(p, np_rows, ns_rows, lb_all, gnorm, l, state_hgrn[l],
                                per_block_state=True, width=a_width)
        ob_p = pool_stream(p, np_rows, pool_cb, w_pool, pscale, l, width=b_width, tile=512)
        u_s = p[np_rows:, u_col0:u_col0 + b_width].reshape(nb, dseq, b_width)
        ob_s = pool_batch(u_s, pool_hist[l], w_pool, pscale, l, start=past_len).reshape(ns_rows, b_width)
        oc_p = sb_stream(p, np_rows, cq, ck, cv, heads=c_heads, tq=1024, tk=128)
        oc_s = sb_decode(p, np_rows, nb, dseq, cq, ck, cv, k_cache, v_cache, l,
                         heads=c_heads, tk=128, tp_first=256, tp_rest=768)

        mix_in = jnp.concatenate([jnp.concatenate([oa_p, ob_p, oc_p], axis=1),
                                  jnp.concatenate([oa_s, ob_s, oc_s], axis=1)], axis=0)
        h, hb = matmul_residual_ln(mix_in, w_out2, l * d_model, x, ln1g, ln1b, l, alpha=alpha, tm=528, tk=1024)

        if l % 2 == 0:
            mid = swiglu_up(hb, ffn_w1, l // 2, tm=1056, tn=512)
            x, xb = matmul_residual_ln(mid, ffn_w2_2, (l // 2) * d_ff, h, ln2g, ln2b, l,
                                       alpha=alpha, tm=704, tk=512)
        else:
            x, xb = routed_ffn(h, router_p, moe_w1, moe_w2, l // 2, ln2g, ln2b, l, ne, alpha=alpha,
                               tm_route=528, tm_comb=256, tf=512)

        outs["ps"].append(st_p)
        outs["pb"].append(p[np_rows - (POOL_HALO - 1):np_rows, u_col0:u_col0 + b_width][None])
        outs["pk"].append(p[:np_rows, k_col0:k_col0 + c_width].reshape(bp, seq, c_heads, HEAD_DIM))
        outs["pv"].append(p[:np_rows, v_col0:v_col0 + c_width].reshape(bp, seq, c_heads, HEAD_DIM))
        outs["ss"].append(st_s)
        outs["sb"].append(u_s[:, dseq - (POOL_HALO - 1):, :])
        outs["sk"].append(p[np_rows:, k_col0:k_col0 + c_width].reshape(nb, dseq, c_heads, HEAD_DIM))
        outs["sv"].append(p[np_rows:, v_col0:v_col0 + c_width].reshape(nb, dseq, c_heads, HEAD_DIM))

    stack = lambda k: jnp.stack(outs[k])
    return (x[:np_rows].reshape(bp, seq, d_model), x[np_rows:].reshape(nb, dseq, d_model),
            stack("ps"), stack("pb"), stack("pk"), stack("pv"),
            stack("ss"), stack("sb"), stack("sk"), stack("sv"))
```

```python
import functools

import jax
import jax.numpy as jnp
from jax import lax
from jax.experimental import pallas as pl
from jax.experimental.pallas import tpu as pltpu

F32 = jnp.float32
BF16 = jnp.bfloat16

HEAD_DIM = 128
HGRN_BLOCK_LOG2 = 4
HGRN_BLOCK = 1 << HGRN_BLOCK_LOG2
HGRN_TILE = 256
POOL_WINDOWS = (2, 4, 8, 16)
POOL_HALO = 16
TOP_K = 2
LN_EPS = 1e-5
RMS_EPS = 1e-6
LB_FLOOR = 1e-30
SB_DEAD_LOG = -104.0
VMEM_LIMIT_BYTES = 56 * 1024 * 1024


def _params(*sem):
    return pltpu.CompilerParams(dimension_semantics=sem, vmem_limit_bytes=VMEM_LIMIT_BYTES)


def _dot(a, b):
    return jnp.dot(a, b, preferred_element_type=F32)


def _dot_nt(a, b):
    return lax.dot_general(a, b, (((1,), (1,)), ((), ())), preferred_element_type=F32)


def _split3(x):
    h1 = x.astype(BF16)
    r1 = x - h1.astype(F32)
    h2 = r1.astype(BF16)
    h3 = (r1 - h2.astype(F32)).astype(BF16)
    return h1, h2, h3


def _sigmoid(x):
    return 1.0 / (1.0 + jnp.exp(-x))


def _log_sigmoid(x):
    return jnp.minimum(x, 0.0) - jnp.log1p(jnp.exp(-jnp.abs(x)))


def _mm_kernel(a_ref, w_ref, o_ref):
    o_ref[...] = _dot(a_ref[...], w_ref[...].astype(BF16)).astype(o_ref.dtype)


def matmul_layer(a, w, layer, n, *, tm, tn):
    m, k = a.shape
    assert m % tm == 0 and n % tn == 0 and n <= w.shape[2]
    return pl.pallas_call(
        _mm_kernel,
        grid=(m // tm, n // tn),
        in_specs=[pl.BlockSpec((tm, k), lambda i, j: (i, 0)),
                  pl.BlockSpec((None, k, tn), lambda i, j: (layer, 0, j))],
        out_specs=pl.BlockSpec((tm, tn), lambda i, j: (i, j)),
        out_shape=jax.ShapeDtypeStruct((m, n), F32),
        compiler_params=_params("parallel", "arbitrary"),
        name="in_proj",
    )(a, w)


def _kv_proj_kernel(x_ref, wk_ref, wv_ref, kin_any, vin_any, k_ref, v_ref):
    del kin_any, vin_any
    x = x_ref[...]
    for w_ref, o_ref in ((wk_ref, k_ref), (wv_ref, v_ref)):
        r = _dot(x, w_ref[...].astype(BF16))
        o_ref[0] = r[:, :HEAD_DIM]
        o_ref[1] = r[:, HEAD_DIM:]


def kv_proj(a, w, layer, col_k, col_v, kbuf, vbuf, row0, *, tm):
    k = a.shape[1]
    nlayer, heads, r, _ = kbuf.shape
    assert r % tm == 0 and row0 % tm == 0 and heads % 2 == 0 and col_k % 2 == 0 and col_v % 2 == 0
    rb0 = row0 // tm
    out = pl.BlockSpec((None, 2, tm, HEAD_DIM), lambda i, j: (layer, j, i, 0))
    anyspec = pl.BlockSpec(memory_space=pl.ANY)
    return pl.pallas_call(
        _kv_proj_kernel,
        grid=(r // tm, heads // 2),
        in_specs=[pl.BlockSpec((tm, k), lambda i, j: (rb0 + i, 0)),
                  pl.BlockSpec((None, k, 2 * HEAD_DIM), lambda i, j: (layer, 0, col_k // 2 + j)),
                  pl.BlockSpec((None, k, 2 * HEAD_DIM), lambda i, j: (layer, 0, col_v // 2 + j)),
                  anyspec, anyspec],
        out_specs=[out, out],
        out_shape=[jax.ShapeDtypeStruct(kbuf.shape, F32), jax.ShapeDtypeStruct(vbuf.shape, F32)],
        input_output_aliases={3: 0, 4: 1},
        compiler_params=_params("parallel", "arbitrary"),
        name="kv_proj",
    )(a, w, w, kbuf, vbuf)


def _swiglu_kernel(x_ref, wg_ref, wu_ref, o_ref):
    x = x_ref[...]
    g = _dot(x, wg_ref[...].astype(BF16))
    u = _dot(x, wu_ref[...].astype(BF16))
    o_ref[...] = (g * _sigmoid(g) * u).astype(o_ref.dtype)


def swiglu_up(x, w1, layer, *, tm, tn):
    m, k = x.shape
    f = w1.shape[-1] // 2
    assert m % tm == 0 and f % tn == 0
    nj = f // tn
    return pl.pallas_call(
        _swiglu_kernel,
        grid=(m // tm, nj),
        in_specs=[pl.BlockSpec((tm, k), lambda i, j: (i, 0)),
                  pl.BlockSpec((None, k, tn), lambda i, j: (layer, 0, j)),
                  pl.BlockSpec((None, k, tn), lambda i, j: (layer, 0, nj + j))],
        out_specs=pl.BlockSpec((tm, tn), lambda i, j: (i, j)),
        out_shape=jax.ShapeDtypeStruct((m, f), BF16),
        compiler_params=_params("parallel", "arbitrary"),
        name="swiglu_up",
    )(x, w1, w1)


def _mm_ln_kernel(a_ref, w_ref, r_ref, g_ref, b_ref, of_ref, ob_ref, *, alpha, nk):
    kk = pl.program_id(1)

    @pl.when(kk == 0)
    def _():
        of_ref[...] = jnp.zeros_like(of_ref)

    of_ref[...] += _dot(a_ref[...], w_ref[...].astype(BF16))

    @pl.when(kk == nk - 1)
    def _():
        r = alpha * r_ref[...] + of_ref[...]
        mu = jnp.mean(r, axis=-1, keepdims=True)
        c = r - mu
        var = jnp.mean(c * c, axis=-1, keepdims=True)
        y = c * lax.rsqrt(var + LN_EPS) * g_ref[...] + b_ref[...]
        of_ref[...] = y
        ob_ref[...] = y.astype(BF16)


def matmul_residual_ln(a, w, w_row0, resid, g, b, layer, *, alpha, tm, tk, w_block_of=None):
    m, k = a.shape
    n = w.shape[1]
    assert m % tm == 0 and k % tk == 0 and w_row0 % tk == 0
    nk = k // tk
    kb0 = w_row0 // tk
    if w_block_of is None:
        w_block_of = lambda kk: kk
    row = pl.BlockSpec((tm, n), lambda i, kk: (i, 0))
    vec = pl.BlockSpec((None, 1, n), lambda i, kk: (layer, 0, 0))
    return pl.pallas_call(
        functools.partial(_mm_ln_kernel, alpha=alpha, nk=nk),
        grid=(m // tm, nk),
        in_specs=[pl.BlockSpec((tm, tk), lambda i, kk: (i, kk)),
                  pl.BlockSpec((tk, n), lambda i, kk: (kb0 + w_block_of(kk), 0)),
                  row, vec, vec],
        out_specs=[row, row],
        out_shape=[jax.ShapeDtypeStruct((m, n), F32), jax.ShapeDtypeStruct((m, n), BF16)],
        compiler_params=_params("parallel", "arbitrary"),
        name="proj_ln",
    )(a, w, resid, g, b)


ROUTE_E1, ROUTE_E2, ROUTE_G1, ROUTE_G2, ROUTE_R1, ROUTE_R2 = range(6)


def _router_kernel(h_ref, r_ref, o_ref, cnt_ref, *, ne):
    step = pl.program_id(0)

    @pl.when(step == 0)
    def _():
        cnt_ref[...] = jnp.zeros_like(cnt_ref)

    logits = jnp.dot(h_ref[...], r_ref[...], preferred_element_type=F32,
                     precision=lax.Precision.HIGHEST)
    tm = logits.shape[0]
    lane = lax.broadcasted_iota(jnp.int32, logits.shape, 1)
    neg = jnp.float32(-jnp.inf)
    x = jnp.where(lane < ne, logits, neg)
    m1 = jnp.max(x, axis=-1, keepdims=True)
    i1 = jnp.min(jnp.where(x == m1, lane, ne), axis=-1, keepdims=True)
    x2 = jnp.where(lane == i1, neg, x)
    m2 = jnp.max(x2, axis=-1, keepdims=True)
    i2 = jnp.min(jnp.where(x2 == m2, lane, ne), axis=-1, keepdims=True)
    e2 = jnp.exp(m2 - m1)
    g1 = 1.0 / (1.0 + e2)
    g2 = e2 / (1.0 + e2)

    hit1 = lane == i1
    hit2 = lane == i2
    onehot = jnp.where(hit1, 1.0, 0.0) + jnp.where(hit2, 1.0, 0.0)
    t_i = lax.broadcasted_iota(jnp.int32, (tm, tm), 0)
    s_i = lax.broadcasted_iota(jnp.int32, (tm, tm), 1)
    before = jnp.where(s_i < t_i, 1.0, 0.0).astype(BF16)
    prior = _dot(before, onehot.astype(BF16)) + cnt_ref[...]
    r1 = jnp.sum(jnp.where(hit1, prior, 0.0), axis=-1, keepdims=True)
    r2 = jnp.sum(jnp.where(hit2, prior, 0.0), axis=-1, keepdims=True)
    cnt_ref[...] += jnp.sum(onehot, axis=0, keepdims=True)

    rec = jnp.zeros(logits.shape, F32)
    for ln, val in ((ROUTE_E1, i1.astype(F32)), (ROUTE_E2, i2.astype(F32)), (ROUTE_G1, g1),
                    (ROUTE_G2, g2), (ROUTE_R1, r1), (ROUTE_R2, r2)):
        rec = jnp.where(lane == ln, val, rec)
    o_ref[...] = rec


def router_top2(h, router_padded, layer, ne, *, tm):
    m, k = h.shape
    return pl.pallas_call(
        functools.partial(_router_kernel, ne=ne),
        grid=(m // tm,),
        in_specs=[pl.BlockSpec((tm, k), lambda i: (i, 0)),
                  pl.BlockSpec((None, k, HEAD_DIM), lambda i: (layer, 0, 0))],
        out_specs=[pl.BlockSpec((tm, HEAD_DIM), lambda i: (i, 0)),
                   pl.BlockSpec((1, HEAD_DIM), lambda i: (0, 0))],
        out_shape=[jax.ShapeDtypeStruct((m, HEAD_DIM), F32), jax.ShapeDtypeStruct((1, HEAD_DIM), F32)],
        compiler_params=_params("arbitrary"),
        name="router",
    )(h, router_padded)


MOE_HALF = 512
MOE_ITEM = 2 * MOE_HALF


def _for_item_rows(nrow, compute):
    lo = 0
    for n in (MOE_ITEM // 8, MOE_ITEM // 4, MOE_ITEM // 2, MOE_ITEM):
        pl.when(jnp.logical_and(nrow > lo, nrow <= n))(functools.partial(compute, n))
        lo = n


def _gather_rows_kernel(nrow_ref, src_ref, h_any, o_ref, buf, sem):
    half = pl.program_id(0)
    nrow = buf.shape[0]
    live = nrow_ref[half // 2] > (1 - half % 2) * MOE_HALF

    @pl.when(live)
    def _():
        def start(r, c):
            pltpu.make_async_copy(h_any.at[pl.ds(src_ref[0, r], 1)], buf.at[pl.ds(r, 1)], sem).start()
            return c

        lax.fori_loop(0, nrow, start, 0, unroll=8)

        def wait(r, c):
            pltpu.make_async_copy(h_any.at[pl.ds(0, 1)], buf.at[pl.ds(r, 1)], sem).wait()
            return c

        lax.fori_loop(0, nrow, wait, 0, unroll=8)
        o_ref[...] = buf[...].astype(o_ref.dtype)

    @pl.when(jnp.logical_not(live))
    def _():
        o_ref[...] = jnp.zeros_like(o_ref)


def gather_rows(h, src, item_rows):
    nhalf = src.shape[0]
    d = h.shape[1]
    grid_spec = pltpu.PrefetchScalarGridSpec(
        num_scalar_prefetch=1,
        grid=(nhalf,),
        in_specs=[pl.BlockSpec((None, 1, MOE_HALF), lambda i, nr: (i, 0, 0), memory_space=pltpu.SMEM),
                  pl.BlockSpec(memory_space=pl.ANY)],
        out_specs=pl.BlockSpec((MOE_HALF, d), lambda i, nr: (i, 0)),
        scratch_shapes=[pltpu.VMEM((MOE_HALF, d), F32), pltpu.SemaphoreType.DMA(())],
    )
    return pl.pallas_call(
        _gather_rows_kernel,
        grid_spec=grid_spec,
        out_shape=jax.ShapeDtypeStruct((nhalf * MOE_HALF, d), BF16),
        compiler_params=_params("arbitrary"),
        name="moe_gather",
    )(item_rows, src, h)


def _grouped_swiglu_kernel(ie_ref, nrow_ref, x_ref, wg_ref, wu_ref, o_ref):
    nrow = nrow_ref[pl.program_id(1)]

    def compute(n):
        x = x_ref[MOE_ITEM - n:, :]
        g = _dot(x, wg_ref[...].astype(BF16))
        u = _dot(x, wu_ref[...].astype(BF16))
        o_ref[MOE_ITEM - n:, :] = (g * _sigmoid(g) * u).astype(o_ref.dtype)
        if n < MOE_ITEM:
            o_ref[:MOE_ITEM - n, :] = jnp.zeros((MOE_ITEM - n, o_ref.shape[1]), o_ref.dtype)

    _for_item_rows(nrow, compute)

    @pl.when(nrow == 0)
    def _():
        o_ref[...] = jnp.zeros_like(o_ref)


def grouped_swiglu(xs, w1, layer, item_expert, item_rows, *, tn):
    rows, k = xs.shape
    nitem = rows // MOE_ITEM
    f = w1.shape[-1] // 2
    nj = f // tn
    grid_spec = pltpu.PrefetchScalarGridSpec(
        num_scalar_prefetch=2,
        grid=(nj, nitem),
        in_specs=[pl.BlockSpec((MOE_ITEM, k), lambda j, it, ie, nr: (jnp.where(nr[it] > 0, it, 0), 0)),
                  pl.BlockSpec((None, None, k, tn), lambda j, it, ie, nr: (layer, ie[it], 0, j)),
                  pl.BlockSpec((None, None, k, tn), lambda j, it, ie, nr: (layer, ie[it], 0, nj + j))],
        out_specs=pl.BlockSpec((MOE_ITEM, tn), lambda j, it, ie, nr: (it, j)),
    )
    return pl.pallas_call(
        _grouped_swiglu_kernel,
        grid_spec=grid_spec,
        out_shape=jax.ShapeDtypeStruct((rows, f), BF16),
        compiler_params=_params("arbitrary", "arbitrary"),
        name="moe_up",
    )(item_expert, item_rows, xs, w1, w1)


def _grouped_down_kernel(ie_ref, nrow_ref, a_ref, w_ref, o_ref):
    nrow = nrow_ref[pl.program_id(0)]

    @pl.when(pl.program_id(1) == 0)
    def _():
        o_ref[...] = jnp.zeros_like(o_ref)

    def compute(n):
        o_ref[MOE_ITEM - n:, :] += _dot(a_ref[MOE_ITEM - n:, :], w_ref[...].astype(BF16))

    _for_item_rows(nrow, compute)


def grouped_down(mid, w2, layer, item_expert, item_rows, *, tk):
    rows, f = mid.shape
    nitem = rows // MOE_ITEM
    d = w2.shape[-1]
    nk = f // tk

    def kidx(it, kk, nr):
        return jnp.where(nr[it] > 0, kk, nk - 1)

    grid_spec = pltpu.PrefetchScalarGridSpec(
        num_scalar_prefetch=2,
        grid=(nitem, nk),
        in_specs=[pl.BlockSpec((MOE_ITEM, tk), lambda it, kk, ie, nr: (it, kidx(it, kk, nr))),
                  pl.BlockSpec((None, None, tk, d), lambda it, kk, ie, nr: (layer, ie[it], kidx(it, kk, nr), 0))],
        out_specs=pl.BlockSpec((MOE_ITEM, d), lambda it, kk, ie, nr: (it, 0)),
    )
    return pl.pallas_call(
        _grouped_down_kernel,
        grid_spec=grid_spec,
        out_shape=jax.ShapeDtypeStruct((rows, d), F32),
        compiler_params=_params("arbitrary", "arbitrary"),
        name="moe_down",
    )(item_expert, item_rows, mid, w2)


def _combine_ln_kernel(dst_ref, y_any, route_ref, r_ref, g_ref, b_ref, of_ref, ob_ref, buf, sem, *, alpha):
    tm = r_ref.shape[0]

    def start(i, c):
        for slot in range(TOP_K):
            pltpu.make_async_copy(y_any.at[pl.ds(dst_ref[0, slot * tm + i], 1)],
                                  buf.at[slot, pl.ds(i, 1)], sem).start()
        return c

    lax.fori_loop(0, tm, start, 0, unroll=8)

    def wait(i, c):
        for slot in range(TOP_K):
            pltpu.make_async_copy(y_any.at[pl.ds(0, 1)], buf.at[slot, pl.ds(i, 1)], sem).wait()
        return c

    lax.fori_loop(0, tm, wait, 0, unroll=8)
    route = route_ref[...]
    g1 = route[:, ROUTE_G1:ROUTE_G1 + 1]
    g2 = route[:, ROUTE_G2:ROUTE_G2 + 1]
    r = alpha * r_ref[...] + (g1 * buf[0] + g2 * buf[1])
    mu = jnp.mean(r, axis=-1, keepdims=True)
    c = r - mu
    var = jnp.mean(c * c, axis=-1, keepdims=True)
    y = c * lax.rsqrt(var + LN_EPS) * g_ref[...] + b_ref[...]
    of_ref[...] = y
    ob_ref[...] = y.astype(BF16)


def combine_ln(ys, dst, route, resid, g, b, layer, *, alpha, tm):
    m, d = resid.shape
    row = pl.BlockSpec((tm, d), lambda i: (i, 0))
    vec = pl.BlockSpec((None, 1, d), lambda i: (layer, 0, 0))
    return pl.pallas_call(
        functools.partial(_combine_ln_kernel, alpha=alpha),
        grid=(m // tm,),
        in_specs=[pl.BlockSpec((None, 1, TOP_K * tm), lambda i: (i, 0, 0), memory_space=pltpu.SMEM),
                  pl.BlockSpec(memory_space=pl.ANY),
                  pl.BlockSpec((tm, HEAD_DIM), lambda i: (i, 0)),
                  row, vec, vec],
        out_specs=[row, row],
        out_shape=[jax.ShapeDtypeStruct((m, d), F32), jax.ShapeDtypeStruct((m, d), BF16)],
        scratch_shapes=[pltpu.VMEM((TOP_K, tm, d), F32), pltpu.SemaphoreType.DMA(())],
        compiler_params=_params("arbitrary"),
        name="moe_combine_ln",
    )(dst, ys, route, resid, g, b)


def routed_ffn(h, router_padded, w1, w2, layer, g, b, ln_layer, ne, *, alpha, tm_route, tm_comb, tf):
    m, d = h.shape
    route, counts = router_top2(h, router_padded, layer, ne, tm=tm_route)
    cnt = counts[0, :ne].astype(jnp.int32)
    nitem_e = (cnt + MOE_ITEM - 1) // MOE_ITEM
    item_end = jnp.cumsum(nitem_e)
    item_start = item_end - nitem_e
    row_off = item_end * MOE_ITEM - cnt
    n_items = item_end[-1]
    max_items = (TOP_K * m + MOE_ITEM - 1) // MOE_ITEM + ne
    item = jnp.arange(max_items, dtype=jnp.int32)
    item_expert = jnp.minimum(jnp.searchsorted(item_end, item, side="right"), ne - 1).astype(jnp.int32)
    first_rows = cnt - (nitem_e - 1) * MOE_ITEM
    item_rows = jnp.where(item == item_start[item_expert], first_rows[item_expert], MOE_ITEM)
    item_rows = jnp.where(item < n_items, item_rows, 0).astype(jnp.int32)
    item_expert = jnp.where(item < n_items, item_expert, item_expert[jnp.maximum(n_items - 1, 0)])
    e1 = route[:, ROUTE_E1].astype(jnp.int32)
    e2 = route[:, ROUTE_E2].astype(jnp.int32)
    dst1 = row_off[e1] + route[:, ROUTE_R1].astype(jnp.int32)
    dst2 = row_off[e2] + route[:, ROUTE_R2].astype(jnp.int32)
    tok = jnp.arange(m, dtype=jnp.int32)
    src = jnp.zeros((max_items * MOE_ITEM,), jnp.int32).at[jnp.concatenate([dst1, dst2])].set(
        jnp.concatenate([tok, tok]), unique_indices=True)
    dst = jnp.concatenate([dst1.reshape(m // tm_comb, 1, tm_comb), dst2.reshape(m // tm_comb, 1, tm_comb)], axis=2)

    xs = gather_rows(h, src.reshape(2 * max_items, 1, MOE_HALF), item_rows)
    mid = grouped_swiglu(xs, w1, layer, item_expert, item_rows, tn=tf)
    ys = grouped_down(mid, w2, layer, item_expert, item_rows, tk=tf)
    return combine_ln(ys, dst, route, h, g, b, ln_layer, alpha=alpha, tm=tm_comb)


def _hgrn_kernel(q_ref, z_ref, v_ref, g_ref, lb_ref, gn_ref, s0_ref, o_ref, sout_ref,
                 st_s, q_s, b_s, c_s, qt_s, kd_s, vb_s, ebl_s, upd_s, sb_s, *, heads, per_block_state):
    rows, width = q_ref.shape
    nblk = rows // HGRN_BLOCK
    step = pl.program_id(0)

    lb = lb_ref[...]
    log_lb = jnp.log(jnp.maximum(lb, LB_FLOOR))
    log_1m_lb = jnp.log1p(-lb)
    z = z_ref[...]
    ls = _log_sigmoid(z)
    lo = log_1m_lb + ls
    logf = jnp.maximum(log_lb, lo) + jnp.log1p(jnp.exp(-jnp.abs(log_lb - lo)))
    logk = log_1m_lb + (ls - z)
    qraw = q_ref[...]
    q = qraw * _sigmoid(qraw)

    r_i = lax.broadcasted_iota(jnp.int32, (rows, rows), 0)
    c_i = lax.broadcasted_iota(jnp.int32, (rows, rows), 1)
    same = (r_i >> HGRN_BLOCK_LOG2) == (c_i >> HGRN_BLOCK_LOG2)
    tri = jnp.where(same & (c_i <= r_i), 1.0, 0.0).astype(BF16)
    blk = jnp.where(same, 1.0, 0.0).astype(BF16)
    parts = _split3(logf)
    b = _dot(tri, parts[0]) + _dot(tri, parts[1]) + _dot(tri, parts[2])
    btot = _dot(blk, parts[0]) + _dot(blk, parts[1]) + _dot(blk, parts[2])
    c = b - logk
    q_s[...] = q
    b_s[...] = b
    c_s[...] = c
    qt_s[...] = (q * jnp.exp(b)).astype(BF16)
    kd_s[...] = jnp.exp(btot - c).astype(BF16)
    ebl_s[...] = jnp.exp(btot)
    vb_s[...] = v_ref[...].astype(BF16)

    def block_input(blk_i, carry):
        rs = pl.ds(pl.multiple_of(blk_i * HGRN_BLOCK, HGRN_BLOCK), HGRN_BLOCK)
        for h in range(heads):
            hc = slice(h * HEAD_DIM, (h + 1) * HEAD_DIM)
            upd_s[blk_i, h] = lax.dot_general(vb_s[rs, hc], kd_s[rs, hc], (((0,), (0,)), ((), ())),
                                              preferred_element_type=F32)
        return carry

    lax.fori_loop(0, nblk, block_input, 0, unroll=4)

    if not per_block_state:
        @pl.when(step == 0)
        def _():
            st_s[...] = jnp.zeros_like(st_s)

    def recur(blk_i, carry):
        r0 = pl.multiple_of(blk_i * HGRN_BLOCK, HGRN_BLOCK)
        for h in range(heads):
            hc = slice(h * HEAD_DIM, (h + 1) * HEAD_DIM)
            st = s0_ref[blk_i, h].T if per_block_state else st_s[h]
            sb_s[blk_i, h] = st.astype(BF16)
            st_new = st * ebl_s[pl.ds(r0, 1), hc] + upd_s[blk_i, h]
            if per_block_state:
                sout_ref[blk_i, h] = st_new.T
            else:
                st_s[h] = st_new
        return carry

    lax.fori_loop(0, nblk, recur, 0)

    if not per_block_state:
        @pl.when(step == pl.num_programs(0) - 1)
        def _():
            for h in range(heads):
                sout_ref[0, h] = st_s[h].T

    pos = lax.broadcasted_iota(jnp.int32, (HGRN_BLOCK, width), 0)
    ones = jnp.ones((HEAD_DIM, HEAD_DIM), BF16)
    gn = gn_ref[...]

    per_trip = 2

    def block_out(trip, carry):
        blks = [trip * per_trip + u for u in range(per_trip)]
        rss = [pl.ds(pl.multiple_of(bi * HGRN_BLOCK, HGRN_BLOCK), HGRN_BLOCK) for bi in blks]
        accs = [jnp.concatenate([_dot_nt(qt_s[rs, h * HEAD_DIM:(h + 1) * HEAD_DIM], sb_s[bi, h])
                                 for h in range(heads)], axis=1) for bi, rs in zip(blks, rss)]
        walls = []
        for rs in rss:
            qb, bb, cb = q_s[rs, :], b_s[rs, :], c_s[rs, :]
            ws = []
            for d in range(HGRN_BLOCK):
                w = qb * jnp.exp(bb - (cb if d == 0 else pltpu.roll(cb, d, 0)))
                if d > 0:
                    w = jnp.where(pos >= d, w, 0.0)
                ws.append(w.astype(BF16))
            walls.append(jnp.concatenate(ws, axis=0))
        atts = [jnp.concatenate([_dot(wall[:, h * HEAD_DIM:(h + 1) * HEAD_DIM], ones) for h in range(heads)], axis=1)
                for wall in walls]
        for rs, acc, att in zip(rss, accs, atts):
            vb = v_ref[rs, :]
            for d in range(HGRN_BLOCK):
                acc = acc + att[d * HGRN_BLOCK:(d + 1) * HGRN_BLOCK, :] * (vb if d == 0 else pltpu.roll(vb, d, 0))
            ms = jnp.concatenate(
                [jnp.broadcast_to(jnp.sum(jnp.square(acc[:, h * HEAD_DIM:(h + 1) * HEAD_DIM]), axis=1, keepdims=True),
                                  (HGRN_BLOCK, HEAD_DIM)) for h in range(heads)], axis=1) * (1.0 / HEAD_DIM)
            gate = g_ref[rs, :]
            o_ref[rs, :] = (acc * lax.rsqrt(ms + RMS_EPS) * gn * (gate * _sigmoid(gate))).astype(o_ref.dtype)
        return carry

    lax.fori_loop(0, nblk // per_trip, block_out, 0)


def _writes_into(kernel, n_in):
    def wrapped(*refs):
        return kernel(*refs[:n_in], *refs[n_in + 1:])
    return wrapped


def _dest_kwargs(dest, n_in):
    return {} if dest is None else {"input_output_aliases": {n_in: 0}}


def hgrn_mixer(p, row0, nrows, lb, gnorm, layer, s0, *, per_block_state, width, dest=None):
    heads = width // HEAD_DIM
    tile = HGRN_TILE
    assert nrows % tile == 0 and row0 % tile == 0
    rb0 = row0 // tile
    nt = nrows // tile
    if per_block_state:
        assert nt == 1
        nstate = s0.shape[0]
    else:
        nstate = 1
        s0 = jnp.zeros((1, heads, HEAD_DIM, HEAD_DIM), F32)

    def col(cb):
        return pl.BlockSpec((tile, width), lambda i: (rb0 + i, cb))

    vec = pl.BlockSpec((None, 1, width), lambda i: (layer, 0, 0))
    state = pl.BlockSpec((nstate, heads, HEAD_DIM, HEAD_DIM), lambda i: (0, 0, 0, 0))
    body = functools.partial(_hgrn_kernel, heads=heads, per_block_state=per_block_state)
    in_specs = [col(0), col(1), col(2), col(3), vec, vec, state]
    args = (p, p, p, p, lb, gnorm, s0)
    if dest is not None:
        body = _writes_into(body, len(in_specs))
        in_specs = in_specs + [pl.BlockSpec(memory_space=pl.ANY)]
        args = args + (dest,)
    return pl.pallas_call(
        body,
        grid=(nt,),
        in_specs=in_specs,
        out_specs=[pl.BlockSpec((tile, width), lambda i: (i, 0)), state],
        out_shape=[jax.ShapeDtypeStruct((nrows, width) if dest is None else dest.shape, BF16),
                   jax.ShapeDtypeStruct((nstate, heads, HEAD_DIM, HEAD_DIM), F32)],
        **_dest_kwargs(dest, len(args) - 1),
        scratch_shapes=[pltpu.VMEM((heads, HEAD_DIM, HEAD_DIM), F32),
                        pltpu.VMEM((tile, width), F32),
                        pltpu.VMEM((tile, width), F32),
                        pltpu.VMEM((tile, width), F32),
                        pltpu.VMEM((tile, width), BF16),
                        pltpu.VMEM((tile, width), BF16),
                        pltpu.VMEM((tile, width), BF16),
                        pltpu.VMEM((tile, width), F32),
                        pltpu.VMEM((tile // HGRN_BLOCK, heads, HEAD_DIM, HEAD_DIM), F32),
                        pltpu.VMEM((tile // HGRN_BLOCK, heads, HEAD_DIM, HEAD_DIM), BF16)],
        compiler_params=_params("arbitrary"),
        name="hgrn2",
    )(*args)


def _pool_tile(xp, pos, w_ref, scale):
    sums = {1: xp}
    s = xp
    w = 1
    while w < POOL_WINDOWS[-1]:
        s = s + pltpu.roll(s, w, 0)
        w *= 2
        sums[w] = s
    gd = HEAD_DIM
    outs = []
    for g, win in enumerate(POOL_WINDOWS):
        sl = slice(g * gd, (g + 1) * gd)
        cnt = jnp.clip(pos + 1, 1, win).astype(F32)
        d = sums[win][:, sl] / cnt - xp[:, sl]
        outs.append(_dot(d.astype(BF16), w_ref[g].astype(BF16)))
    return jnp.concatenate(outs, axis=1) * scale


def _pool_stream_kernel(u_ref, w_ref, sc_ref, o_ref, xp_s):
    step = pl.program_id(0)
    rows = u_ref.shape[0]

    @pl.when(step == 0)
    def _():
        xp_s[0:POOL_HALO, :] = jnp.zeros((POOL_HALO, xp_s.shape[1]), F32)

    xp_s[POOL_HALO:, :] = u_ref[...]
    xp = xp_s[...]
    pos = step * rows - POOL_HALO + lax.broadcasted_iota(jnp.int32, (rows + POOL_HALO, 1), 0)
    y = _pool_tile(xp, pos, w_ref, sc_ref[...])
    o_ref[...] = y[POOL_HALO:, :].astype(o_ref.dtype)
    xp_s[0:POOL_HALO, :] = xp[rows:, :]


def _pool_batch_kernel(u_ref, buf_ref, w_ref, sc_ref, o_ref, xp_s, *, start):
    nb, t, c = u_ref.shape
    seg = POOL_HALO + t
    for bi in range(nb):
        xp_s[bi * seg:bi * seg + POOL_HALO, :] = buf_ref[bi]
        xp_s[bi * seg + POOL_HALO:(bi + 1) * seg, :] = u_ref[bi]
    r = lax.broadcasted_iota(jnp.int32, (nb * seg, 1), 0) & (seg - 1)
    y = _pool_tile(xp_s[...], start - POOL_HALO + r, w_ref, sc_ref[...])
    for bi in range(nb):
        o_ref[bi] = y[bi * seg + POOL_HALO:(bi + 1) * seg, :].astype(o_ref.dtype)


def pool_stream(p, nrows, col_block, w_pool, pool_scale, layer, *, width, tile, dest=None, dest_col_block=0):
    assert nrows % tile == 0
    body = _pool_stream_kernel
    in_specs = [pl.BlockSpec((tile, width), lambda i: (i, col_block)),
                pl.BlockSpec((None,) + w_pool.shape[1:], lambda i: (layer, 0, 0, 0)),
                pl.BlockSpec((None, 1, width), lambda i: (layer, 0, 0))]
    args = (p, w_pool, pool_scale)
    if dest is not None:
        body = _writes_into(body, len(in_specs))
        in_specs = in_specs + [pl.BlockSpec(memory_space=pl.ANY)]
        args = args + (dest,)
    return pl.pallas_call(
        body,
        grid=(nrows // tile,),
        in_specs=in_specs,
        out_specs=pl.BlockSpec((tile, width), lambda i: (i, dest_col_block)),
        out_shape=jax.ShapeDtypeStruct((nrows, width) if dest is None else dest.shape, BF16),
        scratch_shapes=[pltpu.VMEM((tile + POOL_HALO, width), F32)],
        compiler_params=_params("arbitrary"),
        name="pool_stream",
        **_dest_kwargs(dest, len(args) - 1),
    )(*args)


def pool_batch(u, buf, w_pool, pool_scale, layer, *, start):
    nb, t, c = u.shape
    return pl.pallas_call(
        functools.partial(_pool_batch_kernel, start=start),
        grid=(1,),
        in_specs=[pl.BlockSpec((nb, t, c), lambda i: (0, 0, 0)),
                  pl.BlockSpec((nb, POOL_HALO, c), lambda i: (0, 0, 0)),
                  pl.BlockSpec((None,) + w_pool.shape[1:], lambda i: (layer, 0, 0, 0)),
                  pl.BlockSpec((None, 1, c), lambda i: (layer, 0, 0))],
        out_specs=pl.BlockSpec((nb, t, c), lambda i: (0, 0, 0)),
        out_shape=jax.ShapeDtypeStruct((nb, t, c), BF16),
        scratch_shapes=[pltpu.VMEM((nb * (POOL_HALO + t), c), F32)],
        compiler_params=_params("arbitrary"),
        name="pool_batch",
    )(u, buf, w_pool, pool_scale)


def _sb_blocks(qbs, kblks, vblks, masks, r_runs, upper, scale):
    zs = [_dot_nt(qb, kblk.astype(BF16)) * scale for qb, kblk in zip(qbs, kblks)]
    l1ms = []
    for z, mask in zip(zs, masks):
        l1m = _log_sigmoid(-z)
        l1ms.append(l1m if mask is None else jnp.where(mask, l1m, 0.0))
    parts = [_split3(l1m) for l1m in l1ms]
    laters = [_dot(p3[0], upper) + _dot(p3[1], upper) + _dot(p3[2], upper) for p3 in parts]
    avs = []
    for z, l1m, later, r_run, mask in zip(zs, l1ms, laters, r_runs, masks):
        a = jnp.exp(z + l1m + later + r_run)
        avs.append(a if mask is None else jnp.where(mask, a, 0.0))
    contribs = [_dot(a.astype(BF16), vblk.astype(BF16)) for a, vblk in zip(avs, vblks)]
    rowsums = [jnp.sum(l1m, axis=1, keepdims=True) for l1m in l1ms]
    return contribs, rowsums


def _strict_upper(tk):
    j = lax.broadcasted_iota(jnp.int32, (tk, tk), 0)
    s = lax.broadcasted_iota(jnp.int32, (tk, tk), 1)
    return jnp.where(j > s, 1.0, 0.0).astype(BF16)


def _sb_stream_kernel(q_ref, k_ref, v_ref, o_ref, acc_s, run_s, *, tk, scale):
    tq = q_ref.shape[0]
    nsub = tq // tk
    qi = pl.program_id(1)
    upper = _strict_upper(tk)
    qbs = [q_ref[s * tk:(s + 1) * tk, :].astype(BF16) for s in range(nsub)]
    row = lax.broadcasted_iota(jnp.int32, (tk, tk), 0)
    col = lax.broadcasted_iota(jnp.int32, (tk, tk), 1)

    diag0 = [pl.multiple_of((qi * nsub + s) * tk, tk) for s in range(nsub)]
    contribs, rowsums = _sb_blocks(qbs, [k_ref[pl.ds(k0, tk), :] for k0 in diag0],
                                   [v_ref[pl.ds(k0, tk), :] for k0 in diag0], [col < row] * nsub,
                                   [jnp.zeros((tk, 1), F32)] * nsub, upper, scale)
    live = None
    for s in range(nsub):
        acc_s[s * tk:(s + 1) * tk, :] = contribs[s]
        run_s[s * tk:(s + 1) * tk, :] = rowsums[s]
        m = jnp.max(rowsums[s])
        live = m if live is None else jnp.maximum(live, m)

    def cond(carry):
        i, live = carry
        return jnp.logical_and(i <= qi * nsub + nsub - 1, live > SB_DEAD_LOG)

    def body(carry):
        i, _ = carry
        kbs = [qi * nsub + s - i for s in range(nsub)]
        k0s = [pl.multiple_of(jnp.maximum(kb, 0) * tk, tk) for kb in kbs]
        r_runs = [run_s[s * tk:(s + 1) * tk, :] for s in range(nsub)]
        contribs, rowsums = _sb_blocks(qbs, [k_ref[pl.ds(k0, tk), :] for k0 in k0s],
                                       [v_ref[pl.ds(k0, tk), :] for k0 in k0s], [None] * nsub,
                                       r_runs, upper, scale)
        live = None
        for s in range(nsub):
            has_keys = kbs[s] >= 0
            rows = slice(s * tk, (s + 1) * tk)
            acc_s[rows, :] += jnp.where(has_keys, contribs[s], 0.0)
            r_new = jnp.where(has_keys, r_runs[s] + rowsums[s], 2.0 * SB_DEAD_LOG)
            run_s[rows, :] = r_new
            m = jnp.max(r_new)
            live = m if live is None else jnp.maximum(live, m)
        return i + 1, live

    lax.while_loop(cond, body, (jnp.int32(1), live))
    o_ref[...] = acc_s[...].astype(o_ref.dtype)


def sb_stream(p, col_q, kbuf, vbuf, layer, *, tq, tk, dest=None, dest_col_block=0):
    _, heads, nrows, _ = kbuf.shape
    assert nrows % tq == 0 and tq % tk == 0
    scale = HEAD_DIM ** -0.5
    kv = pl.BlockSpec((None, None, nrows, HEAD_DIM), lambda h, i: (layer, h, 0, 0))
    body = functools.partial(_sb_stream_kernel, tk=tk, scale=scale)
    in_specs = [pl.BlockSpec((tq, HEAD_DIM), lambda h, i: (i, col_q + h)), kv, kv]
    args = (p, kbuf, vbuf)
    if dest is not None:
        body = _writes_into(body, len(in_specs))
        in_specs = in_specs + [pl.BlockSpec(memory_space=pl.ANY)]
        args = args + (dest,)
    return pl.pallas_call(
        body,
        grid=(heads, nrows // tq),
        in_specs=in_specs,
        out_specs=pl.BlockSpec((tq, HEAD_DIM), lambda h, i: (i, dest_col_block + h)),
        out_shape=jax.ShapeDtypeStruct((nrows, heads * HEAD_DIM) if dest is None else dest.shape, BF16),
        scratch_shapes=[pltpu.VMEM((tq, HEAD_DIM), F32), pltpu.VMEM((tq, 1), F32)],
        compiler_params=_params("parallel", "arbitrary"),
        name="sb_stream",
        **_dest_kwargs(dest, len(args) - 1),
    )(*args)


def _head_cols(pair_refs, h):
    return pair_refs[h // 2][:, (h % 2) * HEAD_DIM:(h % 2 + 1) * HEAD_DIM]


def _sb_past_chunk(qbs, kp_ref, vp_ref, accs, r_runs, upper, tk, scale):
    heads = len(qbs)
    for j in reversed(range(kp_ref.shape[1] // tk)):
        contribs, rowsums = _sb_blocks(qbs, [kp_ref[h, j * tk:(j + 1) * tk, :] for h in range(heads)],
                                       [vp_ref[h, j * tk:(j + 1) * tk, :] for h in range(heads)],
                                       [None] * heads, r_runs, upper, scale)
        accs = [a + c for a, c in zip(accs, contribs)]
        r_runs = [r + s for r, s in zip(r_runs, rowsums)]
    return accs, r_runs


def _sb_decode_first_kernel(*refs, heads, tk, scale):
    npair = heads // 2
    q_refs = refs[0:npair]
    kn_ref, vn_ref, kp_ref, vp_ref, o_ref, acc_ref, run_ref, live_ref, kpad_s, vpad_s = refs[npair:]
    t = o_ref.shape[0]
    upper = _strict_upper(tk)
    kpad_s[...] = jnp.zeros_like(kpad_s)
    vpad_s[...] = jnp.zeros_like(vpad_s)
    row = lax.broadcasted_iota(jnp.int32, (t, tk), 0)
    col = lax.broadcasted_iota(jnp.int32, (t, tk), 1)
    qbs = [_head_cols(q_refs, h).astype(BF16) for h in range(heads)]
    for h in range(heads):
        kpad_s[h, 0:t, :] = kn_ref[h]
        vpad_s[h, 0:t, :] = vn_ref[h]
    accs, r_runs = _sb_blocks(qbs, [kpad_s[h] for h in range(heads)], [vpad_s[h] for h in range(heads)],
                              [col < row] * heads, [jnp.zeros((t, 1), F32)] * heads, upper, scale)
    accs, r_runs = _sb_past_chunk(qbs, kp_ref, vp_ref, accs, r_runs, upper, tk, scale)
    live = None
    for h in range(heads):
        o_ref[:, h * HEAD_DIM:(h + 1) * HEAD_DIM] = accs[h].astype(o_ref.dtype)
        acc_ref[h] = accs[h]
        run_ref[h] = jnp.broadcast_to(r_runs[h], (t, HEAD_DIM))
        m = jnp.max(r_runs[h])
        live = m if live is None else jnp.maximum(live, m)
    live_ref[...] = jnp.full(live_ref.shape, live, F32)


def _sb_decode_rest_kernel(*refs, heads, tk, scale):
    npair = heads // 2
    q_refs = refs[0:npair]
    acc_in, run_in, kp_ref, vp_ref, o_ref, acc_s, run_s, live_s = refs[npair:]
    step = pl.program_id(1)
    upper = _strict_upper(tk)

    @pl.when(step == 0)
    def _():
        acc_s[...] = acc_in[...]
        run_s[...] = run_in[...]
        live_s[0] = jnp.max(run_in[...])

    @pl.when(live_s[0] > SB_DEAD_LOG)
    def _():
        qbs = [_head_cols(q_refs, h).astype(BF16) for h in range(heads)]
        accs, r_runs = _sb_past_chunk(qbs, kp_ref, vp_ref, [acc_s[h] for h in range(heads)],
                                      [run_s[h][:, 0:1] for h in range(heads)], upper, tk, scale)
        live = None
        for h in range(heads):
            acc_s[h] = accs[h]
            run_s[h] = jnp.broadcast_to(r_runs[h], run_s.shape[1:])
            m = jnp.max(r_runs[h])
            live = m if live is None else jnp.maximum(live, m)
        live_s[0] = live

    @pl.when(step == pl.num_programs(1) - 1)
    def _():
        for h in range(heads):
            o_ref[:, h * HEAD_DIM:(h + 1) * HEAD_DIM] = acc_s[h].astype(o_ref.dtype)


def sb_decode(p, row0, nb, t, col_q, k_new, v_new, k_past, v_past, layer, *, heads, tk, tp_first, tp_rest):
    npast = k_past.shape[3]
    nrest = (npast - tp_first) // tp_rest
    assert row0 % t == 0 and t <= tk and heads % 2 == 0 and tp_first % tk == 0 and tp_rest % tk == 0
    assert npast % tp_first == 0 and nrest * tp_rest == npast - tp_first
    assert col_q % 2 == 0
    rb0 = row0 // t
    npair = heads // 2
    scale = HEAD_DIM ** -0.5
    width = heads * HEAD_DIM
    o_shape = jax.ShapeDtypeStruct((nb * t, width), BF16)
    state_shape = jax.ShapeDtypeStruct((nb, heads, t, HEAD_DIM), F32)

    def new(cb, nax):
        if nax == 1:
            return [pl.BlockSpec((t, 2 * HEAD_DIM), lambda b, j=j: (rb0 + b, cb // 2 + j)) for j in range(npair)]
        return [pl.BlockSpec((t, 2 * HEAD_DIM), lambda b, c, j=j: (rb0 + b, cb // 2 + j)) for j in range(npair)]

    fresh = pl.BlockSpec((None, heads, t, HEAD_DIM), lambda b: (layer, 0, b, 0))
    newest = pl.BlockSpec((None, None, heads, tp_first, HEAD_DIM),
                          lambda b: (layer, b, 0, npast // tp_first - 1, 0))
    o_first, acc, run, live = pl.pallas_call(
        functools.partial(_sb_decode_first_kernel, heads=heads, tk=tk, scale=scale),
        grid=(nb,),
        in_specs=new(col_q, 1) + [fresh, fresh, newest, newest],
        out_specs=[pl.BlockSpec((t, width), lambda b: (b, 0)),
                   pl.BlockSpec((None, heads, t, HEAD_DIM), lambda b: (b, 0, 0, 0)),
                   pl.BlockSpec((None, heads, t, HEAD_DIM), lambda b: (b, 0, 0, 0)),
                   pl.BlockSpec((None, 8, HEAD_DIM), lambda b: (b, 0, 0))],
        out_shape=[o_shape, state_shape, state_shape, jax.ShapeDtypeStruct((nb, 8, HEAD_DIM), F32)],
        scratch_shapes=[pltpu.VMEM((heads, tk, HEAD_DIM), F32), pltpu.VMEM((heads, tk, HEAD_DIM), F32)],
        compiler_params=_params("parallel"),
        name="sb_decode_first",
    )(*([p] * npair), k_new, v_new, k_past, v_past)
    if nrest == 0:
        return o_first

    def older():
        state = pl.BlockSpec((None, heads, t, HEAD_DIM), lambda b, c: (b, 0, 0, 0))
        chunk = pl.BlockSpec((None, None, heads, tp_rest, HEAD_DIM), lambda b, c: (layer, b, 0, nrest - 1 - c, 0))
        return pl.pallas_call(
            functools.partial(_sb_decode_rest_kernel, heads=heads, tk=tk, scale=scale),
            grid=(nb, nrest),
            in_specs=new(col_q, 2) + [state, state, chunk, chunk],
            out_specs=pl.BlockSpec((t, width), lambda b, c: (b, 0)),
            out_shape=o_shape,
            scratch_shapes=[pltpu.VMEM((heads, t, HEAD_DIM), F32), pltpu.VMEM((heads, t, HEAD_DIM), F32),
                            pltpu.SMEM((1,), F32)],
            compiler_params=_params("parallel", "arbitrary"),
            name="sb_decode_rest",
        )(*([p] * npair), acc, run, k_past, v_past)

    return lax.cond(jnp.max(live) > SB_DEAD_LOG, older, lambda: o_first)


def kernel(x_prompt, x_sample, cache_sb_k, cache_sb_v, state_hgrn, state_pool, w_in, lb_param, hgrn_norm,
           w_pool, pool_scale, w_out, ln1_g, ln1_b, ln2_g, ln2_b, ffn_w1, ffn_w2, moe_router, moe_w1, moe_w2):
    bp, seq, d_model = x_prompt.shape
    nb, dseq, _ = x_sample.shape
    depth = w_in.shape[0]
    past_len = cache_sb_k.shape[2]
    a_heads = state_hgrn.shape[2]
    c_heads = cache_sb_k.shape[3]
    a_width = a_heads * HEAD_DIM
    c_width = c_heads * HEAD_DIM
    b_width = state_pool.shape[3]
    ne = moe_router.shape[2]
    d_ff = ffn_w2.shape[1]
    assert bp == 1 and a_width == c_width
    alpha = (2 * depth) ** 0.25
    np_rows = bp * seq
    ns_rows = nb * dseq
    m = np_rows + ns_rows

    pool_cb = 4 * a_width // b_width
    cq = (4 * a_width + b_width) // HEAD_DIM
    ck = cq + c_heads
    cv = ck + c_heads
    k_col0 = ck * HEAD_DIM
    u_col0 = 4 * a_width

    pl_soft = jax.nn.softmax(lb_param.astype(F32), axis=0)
    lb_all = (jnp.cumsum(pl_soft, axis=0) - pl_soft[0])[:, None, :]
    gnorm = hgrn_norm.reshape(depth, 1, a_width)
    pscale = pool_scale.reshape(depth, 1, b_width)
    vec = lambda a: a.reshape(depth, 1, d_model)
    ln1g, ln1b, ln2g, ln2b = vec(ln1_g), vec(ln1_b), vec(ln2_g), vec(ln2_b)
    router_p = jnp.pad(moe_router, ((0, 0), (0, 0), (0, HEAD_DIM - ne)))
    k_cache = jnp.transpose(cache_sb_k, (0, 1, 3, 2, 4))
    v_cache = jnp.transpose(cache_sb_v, (0, 1, 3, 2, 4))
    pool_hist = jnp.pad(state_pool, ((0, 0), (0, 0), (POOL_HALO - state_pool.shape[2], 0), (0, 0)))
    w_out2 = w_out.reshape(depth * d_model, d_model)
    ffn_w2_2 = ffn_w2.reshape(-1, d_model)

    x = jnp.concatenate([x_prompt.reshape(np_rows, d_model), x_sample.reshape(ns_rows, d_model)], axis=0)
    xb = x.astype(BF16)

    kp_buf = jnp.zeros((depth, c_heads, np_rows, HEAD_DIM), F32)
    vp_buf = jnp.zeros((depth, c_heads, np_rows, HEAD_DIM), F32)
    ks_buf = jnp.zeros((depth, c_heads, ns_rows, HEAD_DIM), F32)
    vs_buf = jnp.zeros((depth, c_heads, ns_rows, HEAD_DIM), F32)

    outs = {k: [] for k in ("ps", "pb", "ss", "sb")}
    for l in range(depth):
        p = matmul_layer(xb, w_in, l, k_col0, tm=2112, tn=256)
        kp_buf, vp_buf = kv_proj(xb, w_in, l, ck, cv, kp_buf, vp_buf, 0, tm=2048)
        ks_buf, vs_buf = kv_proj(xb, w_in, l, ck, cv, ks_buf, vs_buf, np_rows, tm=ns_rows)

        mix = jnp.zeros((m, d_model), BF16)
        mix, st_p = hgrn_mixer(p, 0, np_rows, lb_all, gnorm, l, None, per_block_state=False, width=a_width,
                               dest=mix)
        oa_s, st_s = hgrn_mixer(p, np_rows, ns_rows, lb_all, gnorm, l, state_hgrn[l],
                                per_block_state=True, width=a_width)
        mix = pool_stream(p, np_rows, pool_cb, w_pool, pscale, l, width=b_width, tile=512,
                          dest=mix, dest_col_block=(a_width + c_width) // b_width)
        u_s = p[np_rows:, u_col0:u_col0 + b_width].reshape(nb, dseq, b_width)
        ob_s = pool_batch(u_s, pool_hist[l], w_pool, pscale, l, start=past_len).reshape(ns_rows, b_width)
        mix = sb_stream(p, cq, kp_buf, vp_buf, l, tq=1024, tk=128, dest=mix, dest_col_block=a_heads)
        oc_s = sb_decode(p, np_rows, nb, dseq, cq, ks_buf, vs_buf, k_cache, v_cache, l,
                         heads=c_heads, tk=128, tp_first=256, tp_rest=768)
        mix = lax.dynamic_update_slice(mix, jnp.concatenate([oa_s, oc_s, ob_s], axis=1), (np_rows, 0))

        tk_out = 256
        na, nbk, nc = a_width // tk_out, b_width // tk_out, c_width // tk_out
        w_block_of = lambda kk: jnp.where(kk < na, kk, jnp.where(kk < na + nc, kk + nbk, kk - nc))
        h, hb = matmul_residual_ln(mix, w_out2, l * d_model, x, ln1g, ln1b, l, alpha=alpha, tm=1056, tk=tk_out,
                                   w_block_of=w_block_of)

        if l % 2 == 0:
            mid = swiglu_up(hb, ffn_w1, l // 2, tm=1056, tn=512)
            x, xb = matmul_residual_ln(mid, ffn_w2_2, (l // 2) * d_ff, h, ln2g, ln2b, l,
                                       alpha=alpha, tm=704, tk=512)
        else:
            x, xb = routed_ffn(h, router_p, moe_w1, moe_w2, l // 2, ln2g, ln2b, l, ne, alpha=alpha,
                               tm_route=528, tm_comb=256, tf=512)

        outs["ps"].append(st_p)
        outs["pb"].append(p[np_rows - (POOL_HALO - 1):np_rows, u_col0:u_col0 + b_width][None])
        outs["ss"].append(st_s)
        outs["sb"].append(u_s[:, dseq - (POOL_HALO - 1):, :])

    stack = lambda k: jnp.stack(outs[k])
    prompt_kv = lambda a: jnp.transpose(a, (0, 2, 1, 3)).reshape(depth, bp, seq, c_heads, HEAD_DIM)
    sample_kv = lambda a: jnp.transpose(a, (0, 2, 1, 3)).reshape(depth, nb, dseq, c_heads, HEAD_DIM)
    return (x[:np_rows].reshape(bp, seq, d_model), x[np_rows:].reshape(nb, dseq, d_model),
            stack("ps"), stack("pb"), prompt_kv(kp_buf), prompt_kv(vp_buf),
            stack("ss"), stack("sb"), sample_kv(ks_buf), sample_kv(vs_buf))
```

```python
import functools

import jax
import jax.numpy as jnp
from jax import lax
from jax.experimental import pallas as pl
from jax.experimental.pallas import tpu as pltpu

F32 = jnp.float32
BF16 = jnp.bfloat16

HEAD_DIM = 128
HGRN_BLOCK_LOG2 = 4
HGRN_BLOCK = 1 << HGRN_BLOCK_LOG2
HGRN_TILE = 256
POOL_WINDOWS = (2, 4, 8, 16)
POOL_HALO = 16
TOP_K = 2
LN_EPS = 1e-5
RMS_EPS = 1e-6
LB_FLOOR = 1e-30
SB_DEAD_LOG = -104.0
VMEM_LIMIT_BYTES = 56 * 1024 * 1024


def _params(*sem):
    return pltpu.CompilerParams(dimension_semantics=sem, vmem_limit_bytes=VMEM_LIMIT_BYTES)


def _dot(a, b):
    return jnp.dot(a, b, preferred_element_type=F32)


def _dot_nt(a, b):
    return lax.dot_general(a, b, (((1,), (1,)), ((), ())), preferred_element_type=F32)


def _split3(x):
    h1 = x.astype(BF16)
    r1 = x - h1.astype(F32)
    h2 = r1.astype(BF16)
    h3 = (r1 - h2.astype(F32)).astype(BF16)
    return h1, h2, h3


def _sigmoid(x):
    return 1.0 / (1.0 + jnp.exp(-x))


def _log_sigmoid(x):
    return jnp.minimum(x, 0.0) - jnp.log1p(jnp.exp(-jnp.abs(x)))


def _mm_kernel(a_ref, w_ref, o_ref):
    o_ref[...] = _dot(a_ref[...], w_ref[...].astype(BF16)).astype(o_ref.dtype)


def matmul_layer(a, w, layer, n, *, tm, tn):
    m, k = a.shape
    assert m % tm == 0 and n % tn == 0 and n <= w.shape[2]
    return pl.pallas_call(
        _mm_kernel,
        grid=(m // tm, n // tn),
        in_specs=[pl.BlockSpec((tm, k), lambda i, j: (i, 0)),
                  pl.BlockSpec((None, k, tn), lambda i, j: (layer, 0, j))],
        out_specs=pl.BlockSpec((tm, tn), lambda i, j: (i, j)),
        out_shape=jax.ShapeDtypeStruct((m, n), F32),
        compiler_params=_params("parallel", "arbitrary"),
        name="in_proj",
    )(a, w)


def _kv_proj_kernel(x_ref, wk_ref, wv_ref, kin_any, vin_any, k_ref, v_ref):
    del kin_any, vin_any
    x = x_ref[...]
    for w_ref, o_ref in ((wk_ref, k_ref), (wv_ref, v_ref)):
        r = _dot(x, w_ref[...].astype(BF16))
        o_ref[0] = r[:, :HEAD_DIM]
        o_ref[1] = r[:, HEAD_DIM:]


def kv_proj(a, w, layer, col_k, col_v, kbuf, vbuf, row0, *, tm):
    k = a.shape[1]
    nlayer, heads, r, _ = kbuf.shape
    assert r % tm == 0 and row0 % tm == 0 and heads % 2 == 0 and col_k % 2 == 0 and col_v % 2 == 0
    rb0 = row0 // tm
    out = pl.BlockSpec((None, 2, tm, HEAD_DIM), lambda i, j: (layer, j, i, 0))
    anyspec = pl.BlockSpec(memory_space=pl.ANY)
    return pl.pallas_call(
        _kv_proj_kernel,
        grid=(r // tm, heads // 2),
        in_specs=[pl.BlockSpec((tm, k), lambda i, j: (rb0 + i, 0)),
                  pl.BlockSpec((None, k, 2 * HEAD_DIM), lambda i, j: (layer, 0, col_k // 2 + j)),
                  pl.BlockSpec((None, k, 2 * HEAD_DIM), lambda i, j: (layer, 0, col_v // 2 + j)),
                  anyspec, anyspec],
        out_specs=[out, out],
        out_shape=[jax.ShapeDtypeStruct(kbuf.shape, F32), jax.ShapeDtypeStruct(vbuf.shape, F32)],
        input_output_aliases={3: 0, 4: 1},
        compiler_params=_params("parallel", "arbitrary"),
        name="kv_proj",
    )(a, w, w, kbuf, vbuf)


def _swiglu_kernel(x_ref, wg_ref, wu_ref, o_ref):
    x = x_ref[...]
    g = _dot(x, wg_ref[...].astype(BF16))
    u = _dot(x, wu_ref[...].astype(BF16))
    o_ref[...] = (g * _sigmoid(g) * u).astype(o_ref.dtype)


def swiglu_up(x, w1, layer, *, tm, tn):
    m, k = x.shape
    f = w1.shape[-1] // 2
    assert m % tm == 0 and f % tn == 0
    nj = f // tn
    return pl.pallas_call(
        _swiglu_kernel,
        grid=(m // tm, nj),
        in_specs=[pl.BlockSpec((tm, k), lambda i, j: (i, 0)),
                  pl.BlockSpec((None, k, tn), lambda i, j: (layer, 0, j)),
                  pl.BlockSpec((None, k, tn), lambda i, j: (layer, 0, nj + j))],
        out_specs=pl.BlockSpec((tm, tn), lambda i, j: (i, j)),
        out_shape=jax.ShapeDtypeStruct((m, f), BF16),
        compiler_params=_params("parallel", "arbitrary"),
        name="swiglu_up",
    )(x, w1, w1)


def _mm_ln_kernel(a_ref, w_ref, r_ref, g_ref, b_ref, of_ref, ob_ref, *, alpha, nk):
    kk = pl.program_id(1)

    @pl.when(kk == 0)
    def _():
        of_ref[...] = jnp.zeros_like(of_ref)

    of_ref[...] += _dot(a_ref[...], w_ref[...].astype(BF16))

    @pl.when(kk == nk - 1)
    def _():
        r = alpha * r_ref[...] + of_ref[...]
        mu = jnp.mean(r, axis=-1, keepdims=True)
        c = r - mu
        var = jnp.mean(c * c, axis=-1, keepdims=True)
        y = c * lax.rsqrt(var + LN_EPS) * g_ref[...] + b_ref[...]
        of_ref[...] = y
        ob_ref[...] = y.astype(BF16)


def matmul_residual_ln(a, w, w_row0, resid, g, b, layer, *, alpha, tm, tk, w_block_of=None):
    m, k = a.shape
    n = w.shape[1]
    assert m % tm == 0 and k % tk == 0 and w_row0 % tk == 0
    nk = k // tk
    kb0 = w_row0 // tk
    if w_block_of is None:
        w_block_of = lambda kk: kk
    row = pl.BlockSpec((tm, n), lambda i, kk: (i, 0))
    vec = pl.BlockSpec((None, 1, n), lambda i, kk: (layer, 0, 0))
    return pl.pallas_call(
        functools.partial(_mm_ln_kernel, alpha=alpha, nk=nk),
        grid=(m // tm, nk),
        in_specs=[pl.BlockSpec((tm, tk), lambda i, kk: (i, kk)),
                  pl.BlockSpec((tk, n), lambda i, kk: (kb0 + w_block_of(kk), 0)),
                  row, vec, vec],
        out_specs=[row, row],
        out_shape=[jax.ShapeDtypeStruct((m, n), F32), jax.ShapeDtypeStruct((m, n), BF16)],
        compiler_params=_params("parallel", "arbitrary"),
        name="proj_ln",
    )(a, w, resid, g, b)


ROUTE_E1, ROUTE_E2, ROUTE_G1, ROUTE_G2, ROUTE_R1, ROUTE_R2 = range(6)


def _router_kernel(h_ref, r_ref, o_ref, cnt_ref, *, ne):
    step = pl.program_id(0)

    @pl.when(step == 0)
    def _():
        cnt_ref[...] = jnp.zeros_like(cnt_ref)

    logits = jnp.dot(h_ref[...], r_ref[...], preferred_element_type=F32,
                     precision=lax.Precision.HIGHEST)
    tm = logits.shape[0]
    lane = lax.broadcasted_iota(jnp.int32, logits.shape, 1)
    neg = jnp.float32(-jnp.inf)
    x = jnp.where(lane < ne, logits, neg)
    m1 = jnp.max(x, axis=-1, keepdims=True)
    i1 = jnp.min(jnp.where(x == m1, lane, ne), axis=-1, keepdims=True)
    x2 = jnp.where(lane == i1, neg, x)
    m2 = jnp.max(x2, axis=-1, keepdims=True)
    i2 = jnp.min(jnp.where(x2 == m2, lane, ne), axis=-1, keepdims=True)
    e2 = jnp.exp(m2 - m1)
    g1 = 1.0 / (1.0 + e2)
    g2 = e2 / (1.0 + e2)

    hit1 = lane == i1
    hit2 = lane == i2
    onehot = jnp.where(hit1, 1.0, 0.0) + jnp.where(hit2, 1.0, 0.0)
    t_i = lax.broadcasted_iota(jnp.int32, (tm, tm), 0)
    s_i = lax.broadcasted_iota(jnp.int32, (tm, tm), 1)
    before = jnp.where(s_i < t_i, 1.0, 0.0).astype(BF16)
    prior = _dot(before, onehot.astype(BF16)) + cnt_ref[...]
    r1 = jnp.sum(jnp.where(hit1, prior, 0.0), axis=-1, keepdims=True)
    r2 = jnp.sum(jnp.where(hit2, prior, 0.0), axis=-1, keepdims=True)
    cnt_ref[...] += jnp.sum(onehot, axis=0, keepdims=True)

    rec = jnp.zeros(logits.shape, F32)
    for ln, val in ((ROUTE_E1, i1.astype(F32)), (ROUTE_E2, i2.astype(F32)), (ROUTE_G1, g1),
                    (ROUTE_G2, g2), (ROUTE_R1, r1), (ROUTE_R2, r2)):
        rec = jnp.where(lane == ln, val, rec)
    o_ref[...] = rec


def router_top2(h, router_padded, layer, ne, *, tm):
    m, k = h.shape
    return pl.pallas_call(
        functools.partial(_router_kernel, ne=ne),
        grid=(m // tm,),
        in_specs=[pl.BlockSpec((tm, k), lambda i: (i, 0)),
                  pl.BlockSpec((None, k, HEAD_DIM), lambda i: (layer, 0, 0))],
        out_specs=[pl.BlockSpec((tm, HEAD_DIM), lambda i: (i, 0)),
                   pl.BlockSpec((1, HEAD_DIM), lambda i: (0, 0))],
        out_shape=[jax.ShapeDtypeStruct((m, HEAD_DIM), F32), jax.ShapeDtypeStruct((1, HEAD_DIM), F32)],
        compiler_params=_params("arbitrary"),
        name="router",
    )(h, router_padded)


MOE_HALF = 512
MOE_ITEM = 2 * MOE_HALF


def _for_item_rows(nrow, compute):
    lo = 0
    for n in (MOE_ITEM // 8, MOE_ITEM // 4, MOE_ITEM // 2, MOE_ITEM):
        pl.when(jnp.logical_and(nrow > lo, nrow <= n))(functools.partial(compute, n))
        lo = n


def _gather_rows_kernel(nrow_ref, src_ref, h_any, o_ref, buf, sem):
    half = pl.program_id(0)
    nrow = buf.shape[0]
    live = nrow_ref[half // 2] > (1 - half % 2) * MOE_HALF

    @pl.when(live)
    def _():
        def start(i, c):
            for u in range(2):
                r = 2 * i + u
                pltpu.make_async_copy(h_any.at[pl.ds(src_ref[0, r], 1)], buf.at[pl.ds(r, 1)],
                                      sem).start(priority=u)
            return c

        lax.fori_loop(0, nrow // 2, start, 0, unroll=4)

        def wait(r, c):
            pltpu.make_async_copy(h_any.at[pl.ds(0, 1)], buf.at[pl.ds(r, 1)], sem).wait()
            return c

        lax.fori_loop(0, nrow, wait, 0, unroll=8)
        o_ref[...] = buf[...].astype(o_ref.dtype)

    @pl.when(jnp.logical_not(live))
    def _():
        o_ref[...] = jnp.zeros_like(o_ref)


def gather_rows(h, src, item_rows):
    nhalf = src.shape[0]
    d = h.shape[1]
    grid_spec = pltpu.PrefetchScalarGridSpec(
        num_scalar_prefetch=1,
        grid=(nhalf,),
        in_specs=[pl.BlockSpec((None, 1, MOE_HALF), lambda i, nr: (i, 0, 0), memory_space=pltpu.SMEM),
                  pl.BlockSpec(memory_space=pl.ANY)],
        out_specs=pl.BlockSpec((MOE_HALF, d), lambda i, nr: (i, 0)),
        scratch_shapes=[pltpu.VMEM((MOE_HALF, d), F32), pltpu.SemaphoreType.DMA(())],
    )
    return pl.pallas_call(
        _gather_rows_kernel,
        grid_spec=grid_spec,
        out_shape=jax.ShapeDtypeStruct((nhalf * MOE_HALF, d), BF16),
        compiler_params=_params("arbitrary"),
        name="moe_gather",
    )(item_rows, src, h)


def _grouped_swiglu_kernel(ie_ref, nrow_ref, x_ref, wg_ref, wu_ref, o_ref):
    nrow = nrow_ref[pl.program_id(1)]

    def compute(n):
        x = x_ref[MOE_ITEM - n:, :]
        g = _dot(x, wg_ref[...].astype(BF16))
        u = _dot(x, wu_ref[...].astype(BF16))
        o_ref[MOE_ITEM - n:, :] = (g * _sigmoid(g) * u).astype(o_ref.dtype)
        if n < MOE_ITEM:
            o_ref[:MOE_ITEM - n, :] = jnp.zeros((MOE_ITEM - n, o_ref.shape[1]), o_ref.dtype)

    _for_item_rows(nrow, compute)

    @pl.when(nrow == 0)
    def _():
        o_ref[...] = jnp.zeros_like(o_ref)


def grouped_swiglu(xs, w1, layer, item_expert, item_rows, *, tn):
    rows, k = xs.shape
    nitem = rows // MOE_ITEM
    f = w1.shape[-1] // 2
    nj = f // tn
    grid_spec = pltpu.PrefetchScalarGridSpec(
        num_scalar_prefetch=2,
        grid=(nj, nitem),
        in_specs=[pl.BlockSpec((MOE_ITEM, k), lambda j, it, ie, nr: (jnp.where(nr[it] > 0, it, 0), 0)),
                  pl.BlockSpec((None, None, k, tn), lambda j, it, ie, nr: (layer, ie[it], 0, j)),
                  pl.BlockSpec((None, None, k, tn), lambda j, it, ie, nr: (layer, ie[it], 0, nj + j))],
        out_specs=pl.BlockSpec((MOE_ITEM, tn), lambda j, it, ie, nr: (it, j)),
    )
    return pl.pallas_call(
        _grouped_swiglu_kernel,
        grid_spec=grid_spec,
        out_shape=jax.ShapeDtypeStruct((rows, f), BF16),
        compiler_params=_params("arbitrary", "arbitrary"),
        name="moe_up",
    )(item_expert, item_rows, xs, w1, w1)


def _grouped_down_kernel(ie_ref, nrow_ref, a_ref, w_ref, o_ref):
    nrow = nrow_ref[pl.program_id(0)]

    @pl.when(pl.program_id(1) == 0)
    def _():
        o_ref[...] = jnp.zeros_like(o_ref)

    def compute(n):
        o_ref[MOE_ITEM - n:, :] += _dot(a_ref[MOE_ITEM - n:, :], w_ref[...].astype(BF16))

    _for_item_rows(nrow, compute)


def grouped_down(mid, w2, layer, item_expert, item_rows, *, tk):
    rows, f = mid.shape
    nitem = rows // MOE_ITEM
    d = w2.shape[-1]
    nk = f // tk

    def kidx(it, kk, nr):
        return jnp.where(nr[it] > 0, kk, nk - 1)

    grid_spec = pltpu.PrefetchScalarGridSpec(
        num_scalar_prefetch=2,
        grid=(nitem, nk),
        in_specs=[pl.BlockSpec((MOE_ITEM, tk), lambda it, kk, ie, nr: (it, kidx(it, kk, nr))),
                  pl.BlockSpec((None, None, tk, d), lambda it, kk, ie, nr: (layer, ie[it], kidx(it, kk, nr), 0))],
        out_specs=pl.BlockSpec((MOE_ITEM, d), lambda it, kk, ie, nr: (it, 0)),
    )
    return pl.pallas_call(
        _grouped_down_kernel,
        grid_spec=grid_spec,
        out_shape=jax.ShapeDtypeStruct((rows, d), F32),
        compiler_params=_params("arbitrary", "arbitrary"),
        name="moe_down",
    )(item_expert, item_rows, mid, w2)


def _combine_ln_kernel(dst_ref, y_any, route_ref, r_ref, g_ref, b_ref, of_ref, ob_ref, buf, sem, *, alpha):
    tm = r_ref.shape[0]

    def start(i, c):
        for slot in range(TOP_K):
            pltpu.make_async_copy(y_any.at[pl.ds(dst_ref[0, slot * tm + i], 1)],
                                  buf.at[slot, pl.ds(i, 1)], sem).start(priority=slot)
        return c

    lax.fori_loop(0, tm, start, 0, unroll=8)

    def wait(i, c):
        for slot in range(TOP_K):
            pltpu.make_async_copy(y_any.at[pl.ds(0, 1)], buf.at[slot, pl.ds(i, 1)], sem).wait()
        return c

    lax.fori_loop(0, tm, wait, 0, unroll=8)
    route = route_ref[...]
    g1 = route[:, ROUTE_G1:ROUTE_G1 + 1]
    g2 = route[:, ROUTE_G2:ROUTE_G2 + 1]
    r = alpha * r_ref[...] + (g1 * buf[0] + g2 * buf[1])
    mu = jnp.mean(r, axis=-1, keepdims=True)
    c = r - mu
    var = jnp.mean(c * c, axis=-1, keepdims=True)
    y = c * lax.rsqrt(var + LN_EPS) * g_ref[...] + b_ref[...]
    of_ref[...] = y
    ob_ref[...] = y.astype(BF16)


def combine_ln(ys, dst, route, resid, g, b, layer, *, alpha, tm):
    m, d = resid.shape
    row = pl.BlockSpec((tm, d), lambda i: (i, 0))
    vec = pl.BlockSpec((None, 1, d), lambda i: (layer, 0, 0))
    return pl.pallas_call(
        functools.partial(_combine_ln_kernel, alpha=alpha),
        grid=(m // tm,),
        in_specs=[pl.BlockSpec((None, 1, TOP_K * tm), lambda i: (i, 0, 0), memory_space=pltpu.SMEM),
                  pl.BlockSpec(memory_space=pl.ANY),
                  pl.BlockSpec((tm, HEAD_DIM), lambda i: (i, 0)),
                  row, vec, vec],
        out_specs=[row, row],
        out_shape=[jax.ShapeDtypeStruct((m, d), F32), jax.ShapeDtypeStruct((m, d), BF16)],
        scratch_shapes=[pltpu.VMEM((TOP_K, tm, d), F32), pltpu.SemaphoreType.DMA(())],
        compiler_params=_params("arbitrary"),
        name="moe_combine_ln",
    )(dst, ys, route, resid, g, b)


def routed_ffn(h, router_padded, w1, w2, layer, g, b, ln_layer, ne, *, alpha, tm_route, tm_comb, tf):
    m, d = h.shape
    route, counts = router_top2(h, router_padded, layer, ne, tm=tm_route)
    cnt = counts[0, :ne].astype(jnp.int32)
    nitem_e = (cnt + MOE_ITEM - 1) // MOE_ITEM
    item_end = jnp.cumsum(nitem_e)
    item_start = item_end - nitem_e
    row_off = item_end * MOE_ITEM - cnt
    n_items = item_end[-1]
    max_items = (TOP_K * m + MOE_ITEM - 1) // MOE_ITEM + ne
    item = jnp.arange(max_items, dtype=jnp.int32)
    item_expert = jnp.minimum(jnp.searchsorted(item_end, item, side="right"), ne - 1).astype(jnp.int32)
    first_rows = cnt - (nitem_e - 1) * MOE_ITEM
    item_rows = jnp.where(item == item_start[item_expert], first_rows[item_expert], MOE_ITEM)
    item_rows = jnp.where(item < n_items, item_rows, 0).astype(jnp.int32)
    item_expert = jnp.where(item < n_items, item_expert, item_expert[jnp.maximum(n_items - 1, 0)])
    e1 = route[:, ROUTE_E1].astype(jnp.int32)
    e2 = route[:, ROUTE_E2].astype(jnp.int32)
    dst1 = row_off[e1] + route[:, ROUTE_R1].astype(jnp.int32)
    dst2 = row_off[e2] + route[:, ROUTE_R2].astype(jnp.int32)
    tok = jnp.arange(m, dtype=jnp.int32)
    src = jnp.zeros((max_items * MOE_ITEM,), jnp.int32).at[jnp.concatenate([dst1, dst2])].set(
        jnp.concatenate([tok, tok]), unique_indices=True)
    dst = jnp.concatenate([dst1.reshape(m // tm_comb, 1, tm_comb), dst2.reshape(m // tm_comb, 1, tm_comb)], axis=2)

    xs = gather_rows(h, src.reshape(2 * max_items, 1, MOE_HALF), item_rows)
    mid = grouped_swiglu(xs, w1, layer, item_expert, item_rows, tn=tf)
    ys = grouped_down(mid, w2, layer, item_expert, item_rows, tk=tf)
    return combine_ln(ys, dst, route, h, g, b, ln_layer, alpha=alpha, tm=tm_comb)


def _hgrn_kernel(q_ref, z_ref, v_ref, g_ref, lb_ref, gn_ref, s0_ref, o_ref, sout_ref,
                 st_s, q_s, b_s, c_s, qt_s, kd_s, vb_s, ebl_s, upd_s, sb_s, *, heads, per_block_state):
    rows, width = q_ref.shape
    nblk = rows // HGRN_BLOCK
    step = pl.program_id(0)

    lb = lb_ref[...]
    log_lb = jnp.log(jnp.maximum(lb, LB_FLOOR))
    log_1m_lb = jnp.log1p(-lb)
    z = z_ref[...]
    ls = _log_sigmoid(z)
    lo = log_1m_lb + ls
    logf = jnp.maximum(log_lb, lo) + jnp.log1p(jnp.exp(-jnp.abs(log_lb - lo)))
    logk = log_1m_lb + (ls - z)
    qraw = q_ref[...]
    q = qraw * _sigmoid(qraw)

    r_i = lax.broadcasted_iota(jnp.int32, (rows, rows), 0)
    c_i = lax.broadcasted_iota(jnp.int32, (rows, rows), 1)
    same = (r_i >> HGRN_BLOCK_LOG2) == (c_i >> HGRN_BLOCK_LOG2)
    tri = jnp.where(same & (c_i <= r_i), 1.0, 0.0).astype(BF16)
    blk = jnp.where(same, 1.0, 0.0).astype(BF16)
    parts = _split3(logf)
    b = _dot(tri, parts[0]) + _dot(tri, parts[1]) + _dot(tri, parts[2])
    btot = _dot(blk, parts[0]) + _dot(blk, parts[1]) + _dot(blk, parts[2])
    c = b - logk
    q_s[...] = q
    b_s[...] = b
    c_s[...] = c
    qt_s[...] = (q * jnp.exp(b)).astype(BF16)
    kd_s[...] = jnp.exp(btot - c).astype(BF16)
    ebl_s[...] = jnp.exp(btot)
    vb_s[...] = v_ref[...].astype(BF16)

    def block_input(blk_i, carry):
        rs = pl.ds(pl.multiple_of(blk_i * HGRN_BLOCK, HGRN_BLOCK), HGRN_BLOCK)
        for h in range(heads):
            hc = slice(h * HEAD_DIM, (h + 1) * HEAD_DIM)
            upd_s[blk_i, h] = lax.dot_general(vb_s[rs, hc], kd_s[rs, hc], (((0,), (0,)), ((), ())),
                                              preferred_element_type=F32)
        return carry

    lax.fori_loop(0, nblk, block_input, 0, unroll=4)

    if not per_block_state:
        @pl.when(step == 0)
        def _():
            st_s[...] = jnp.zeros_like(st_s)

    def recur(blk_i, carry):
        r0 = pl.multiple_of(blk_i * HGRN_BLOCK, HGRN_BLOCK)
        for h in range(heads):
            hc = slice(h * HEAD_DIM, (h + 1) * HEAD_DIM)
            st = s0_ref[blk_i, h].T if per_block_state else st_s[h]
            sb_s[blk_i, h] = st.astype(BF16)
            st_new = st * ebl_s[pl.ds(r0, 1), hc] + upd_s[blk_i, h]
            if per_block_state:
                sout_ref[blk_i, h] = st_new.T
            else:
                st_s[h] = st_new
        return carry

    lax.fori_loop(0, nblk, recur, 0)

    if not per_block_state:
        @pl.when(step == pl.num_programs(0) - 1)
        def _():
            for h in range(heads):
                sout_ref[0, h] = st_s[h].T

    pos = lax.broadcasted_iota(jnp.int32, (HGRN_BLOCK, width), 0)
    ones = jnp.ones((HEAD_DIM, HEAD_DIM), BF16)
    gn = gn_ref[...]

    per_trip = 2

    def block_out(trip, carry):
        blks = [trip * per_trip + u for u in range(per_trip)]
        rss = [pl.ds(pl.multiple_of(bi * HGRN_BLOCK, HGRN_BLOCK), HGRN_BLOCK) for bi in blks]
        accs = [jnp.concatenate([_dot_nt(qt_s[rs, h * HEAD_DIM:(h + 1) * HEAD_DIM], sb_s[bi, h])
                                 for h in range(heads)], axis=1) for bi, rs in zip(blks, rss)]
        walls = []
        for rs in rss:
            qb, bb, cb = q_s[rs, :], b_s[rs, :], c_s[rs, :]
            ws = []
            for d in range(HGRN_BLOCK):
                w = qb * jnp.exp(bb - (cb if d == 0 else pltpu.roll(cb, d, 0)))
                if d > 0:
                    w = jnp.where(pos >= d, w, 0.0)
                ws.append(w.astype(BF16))
            walls.append(jnp.concatenate(ws, axis=0))
        atts = [jnp.concatenate([_dot(wall[:, h * HEAD_DIM:(h + 1) * HEAD_DIM], ones) for h in range(heads)], axis=1)
                for wall in walls]
        for rs, acc, att in zip(rss, accs, atts):
            vb = v_ref[rs, :]
            for d in range(HGRN_BLOCK):
                acc = acc + att[d * HGRN_BLOCK:(d + 1) * HGRN_BLOCK, :] * (vb if d == 0 else pltpu.roll(vb, d, 0))
            ms = jnp.concatenate(
                [jnp.broadcast_to(jnp.sum(jnp.square(acc[:, h * HEAD_DIM:(h + 1) * HEAD_DIM]), axis=1, keepdims=True),
                                  (HGRN_BLOCK, HEAD_DIM)) for h in range(heads)], axis=1) * (1.0 / HEAD_DIM)
            gate = g_ref[rs, :]
            o_ref[rs, :] = (acc * lax.rsqrt(ms + RMS_EPS) * gn * (gate * _sigmoid(gate))).astype(o_ref.dtype)
        return carry

    lax.fori_loop(0, nblk // per_trip, block_out, 0)


def _writes_into(kernel, n_in):
    def wrapped(*refs):
        return kernel(*refs[:n_in], *refs[n_in + 1:])
    return wrapped


def _dest_kwargs(dest, n_in):
    return {} if dest is None else {"input_output_aliases": {n_in: 0}}


def hgrn_mixer(p, row0, nrows, lb, gnorm, layer, s0, *, per_block_state, width, dest=None):
    heads = width // HEAD_DIM
    tile = HGRN_TILE
    assert nrows % tile == 0 and row0 % tile == 0
    rb0 = row0 // tile
    nt = nrows // tile
    if per_block_state:
        assert nt == 1
        nstate = s0.shape[0]
    else:
        nstate = 1
        s0 = jnp.zeros((1, heads, HEAD_DIM, HEAD_DIM), F32)

    def col(cb):
        return pl.BlockSpec((tile, width), lambda i: (rb0 + i, cb))

    vec = pl.BlockSpec((None, 1, width), lambda i: (layer, 0, 0))
    state = pl.BlockSpec((nstate, heads, HEAD_DIM, HEAD_DIM), lambda i: (0, 0, 0, 0))
    body = functools.partial(_hgrn_kernel, heads=heads, per_block_state=per_block_state)
    in_specs = [col(0), col(1), col(2), col(3), vec, vec, state]
    args = (p, p, p, p, lb, gnorm, s0)
    if dest is not None:
        body = _writes_into(body, len(in_specs))
        in_specs = in_specs + [pl.BlockSpec(memory_space=pl.ANY)]
        args = args + (dest,)
    return pl.pallas_call(
        body,
        grid=(nt,),
        in_specs=in_specs,
        out_specs=[pl.BlockSpec((tile, width), lambda i: (i, 0)), state],
        out_shape=[jax.ShapeDtypeStruct((nrows, width) if dest is None else dest.shape, BF16),
                   jax.ShapeDtypeStruct((nstate, heads, HEAD_DIM, HEAD_DIM), F32)],
        **_dest_kwargs(dest, len(args) - 1),
        scratch_shapes=[pltpu.VMEM((heads, HEAD_DIM, HEAD_DIM), F32),
                        pltpu.VMEM((tile, width), F32),
                        pltpu.VMEM((tile, width), F32),
                        pltpu.VMEM((tile, width), F32),
                        pltpu.VMEM((tile, width), BF16),
                        pltpu.VMEM((tile, width), BF16),
                        pltpu.VMEM((tile, width), BF16),
                        pltpu.VMEM((tile, width), F32),
                        pltpu.VMEM((tile // HGRN_BLOCK, heads, HEAD_DIM, HEAD_DIM), F32),
                        pltpu.VMEM((tile // HGRN_BLOCK, heads, HEAD_DIM, HEAD_DIM), BF16)],
        compiler_params=_params("arbitrary"),
        name="hgrn2",
    )(*args)


def _pool_tile(xp, pos, w_ref, scale):
    sums = {1: xp}
    s = xp
    w = 1
    while w < POOL_WINDOWS[-1]:
        s = s + pltpu.roll(s, w, 0)
        w *= 2
        sums[w] = s
    gd = HEAD_DIM
    outs = []
    for g, win in enumerate(POOL_WINDOWS):
        sl = slice(g * gd, (g + 1) * gd)
        cnt = jnp.clip(pos + 1, 1, win).astype(F32)
        d = sums[win][:, sl] / cnt - xp[:, sl]
        outs.append(_dot(d.astype(BF16), w_ref[g].astype(BF16)))
    return jnp.concatenate(outs, axis=1) * scale


def _pool_stream_kernel(u_ref, w_ref, sc_ref, o_ref, xp_s):
    step = pl.program_id(0)
    rows = u_ref.shape[0]

    @pl.when(step == 0)
    def _():
        xp_s[0:POOL_HALO, :] = jnp.zeros((POOL_HALO, xp_s.shape[1]), F32)

    xp_s[POOL_HALO:, :] = u_ref[...]
    xp = xp_s[...]
    pos = step * rows - POOL_HALO + lax.broadcasted_iota(jnp.int32, (rows + POOL_HALO, 1), 0)
    y = _pool_tile(xp, pos, w_ref, sc_ref[...])
    o_ref[...] = y[POOL_HALO:, :].astype(o_ref.dtype)
    xp_s[0:POOL_HALO, :] = xp[rows:, :]


def _pool_batch_kernel(u_ref, buf_ref, w_ref, sc_ref, o_ref, xp_s, *, start):
    nb, t, c = u_ref.shape
    seg = POOL_HALO + t
    for bi in range(nb):
        xp_s[bi * seg:bi * seg + POOL_HALO, :] = buf_ref[bi]
        xp_s[bi * seg + POOL_HALO:(bi + 1) * seg, :] = u_ref[bi]
    r = lax.broadcasted_iota(jnp.int32, (nb * seg, 1), 0) & (seg - 1)
    y = _pool_tile(xp_s[...], start - POOL_HALO + r, w_ref, sc_ref[...])
    for bi in range(nb):
        o_ref[bi] = y[bi * seg + POOL_HALO:(bi + 1) * seg, :].astype(o_ref.dtype)


def pool_stream(p, nrows, col_block, w_pool, pool_scale, layer, *, width, tile, dest=None, dest_col_block=0):
    assert nrows % tile == 0
    body = _pool_stream_kernel
    in_specs = [pl.BlockSpec((tile, width), lambda i: (i, col_block)),
                pl.BlockSpec((None,) + w_pool.shape[1:], lambda i: (layer, 0, 0, 0)),
                pl.BlockSpec((None, 1, width), lambda i: (layer, 0, 0))]
    args = (p, w_pool, pool_scale)
    if dest is not None:
        body = _writes_into(body, len(in_specs))
        in_specs = in_specs + [pl.BlockSpec(memory_space=pl.ANY)]
        args = args + (dest,)
    return pl.pallas_call(
        body,
        grid=(nrows // tile,),
        in_specs=in_specs,
        out_specs=pl.BlockSpec((tile, width), lambda i: (i, dest_col_block)),
        out_shape=jax.ShapeDtypeStruct((nrows, width) if dest is None else dest.shape, BF16),
        scratch_shapes=[pltpu.VMEM((tile + POOL_HALO, width), F32)],
        compiler_params=_params("arbitrary"),
        name="pool_stream",
        **_dest_kwargs(dest, len(args) - 1),
    )(*args)


def pool_batch(u, buf, w_pool, pool_scale, layer, *, start):
    nb, t, c = u.shape
    return pl.pallas_call(
        functools.partial(_pool_batch_kernel, start=start),
        grid=(1,),
        in_specs=[pl.BlockSpec((nb, t, c), lambda i: (0, 0, 0)),
                  pl.BlockSpec((nb, POOL_HALO, c), lambda i: (0, 0, 0)),
                  pl.BlockSpec((None,) + w_pool.shape[1:], lambda i: (layer, 0, 0, 0)),
                  pl.BlockSpec((None, 1, c), lambda i: (layer, 0, 0))],
        out_specs=pl.BlockSpec((nb, t, c), lambda i: (0, 0, 0)),
        out_shape=jax.ShapeDtypeStruct((nb, t, c), BF16),
        scratch_shapes=[pltpu.VMEM((nb * (POOL_HALO + t), c), F32)],
        compiler_params=_params("arbitrary"),
        name="pool_batch",
    )(u, buf, w_pool, pool_scale)


def _sb_blocks(qbs, kblks, vblks, masks, r_runs, upper, scale):
    zs = [_dot_nt(qb, kblk.astype(BF16)) * scale for qb, kblk in zip(qbs, kblks)]
    l1ms = []
    for z, mask in zip(zs, masks):
        l1m = _log_sigmoid(-z)
        l1ms.append(l1m if mask is None else jnp.where(mask, l1m, 0.0))
    parts = [_split3(l1m) for l1m in l1ms]
    laters = [_dot(p3[0], upper) + _dot(p3[1], upper) + _dot(p3[2], upper) for p3 in parts]
    avs = []
    for z, l1m, later, r_run, mask in zip(zs, l1ms, laters, r_runs, masks):
        a = jnp.exp(z + l1m + later + r_run)
        avs.append(a if mask is None else jnp.where(mask, a, 0.0))
    contribs = [_dot(a.astype(BF16), vblk.astype(BF16)) for a, vblk in zip(avs, vblks)]
    rowsums = [jnp.sum(l1m, axis=1, keepdims=True) for l1m in l1ms]
    return contribs, rowsums


def _strict_upper(tk):
    j = lax.broadcasted_iota(jnp.int32, (tk, tk), 0)
    s = lax.broadcasted_iota(jnp.int32, (tk, tk), 1)
    return jnp.where(j > s, 1.0, 0.0).astype(BF16)


def _sb_stream_kernel(q_ref, k_ref, v_ref, o_ref, acc_s, run_s, *, tk, scale):
    tq = q_ref.shape[0]
    nsub = tq // tk
    qi = pl.program_id(1)
    upper = _strict_upper(tk)
    qbs = [q_ref[s * tk:(s + 1) * tk, :].astype(BF16) for s in range(nsub)]
    row = lax.broadcasted_iota(jnp.int32, (tk, tk), 0)
    col = lax.broadcasted_iota(jnp.int32, (tk, tk), 1)

    diag0 = [pl.multiple_of((qi * nsub + s) * tk, tk) for s in range(nsub)]
    contribs, rowsums = _sb_blocks(qbs, [k_ref[pl.ds(k0, tk), :] for k0 in diag0],
                                   [v_ref[pl.ds(k0, tk), :] for k0 in diag0], [col < row] * nsub,
                                   [jnp.zeros((tk, 1), F32)] * nsub, upper, scale)
    live = None
    for s in range(nsub):
        acc_s[s * tk:(s + 1) * tk, :] = contribs[s]
        run_s[s * tk:(s + 1) * tk, :] = rowsums[s]
        m = jnp.max(rowsums[s])
        live = m if live is None else jnp.maximum(live, m)

    def cond(carry):
        i, live = carry
        return jnp.logical_and(i <= qi * nsub + nsub - 1, live > SB_DEAD_LOG)

    def body(carry):
        i, _ = carry
        kbs = [qi * nsub + s - i for s in range(nsub)]
        k0s = [pl.multiple_of(jnp.maximum(kb, 0) * tk, tk) for kb in kbs]
        r_runs = [run_s[s * tk:(s + 1) * tk, :] for s in range(nsub)]
        contribs, rowsums = _sb_blocks(qbs, [k_ref[pl.ds(k0, tk), :] for k0 in k0s],
                                       [v_ref[pl.ds(k0, tk), :] for k0 in k0s], [None] * nsub,
                                       r_runs, upper, scale)
        live = None
        for s in range(nsub):
            has_keys = kbs[s] >= 0
            rows = slice(s * tk, (s + 1) * tk)
            acc_s[rows, :] += jnp.where(has_keys, contribs[s], 0.0)
            r_new = jnp.where(has_keys, r_runs[s] + rowsums[s], 2.0 * SB_DEAD_LOG)
            run_s[rows, :] = r_new
            m = jnp.max(r_new)
            live = m if live is None else jnp.maximum(live, m)
        return i + 1, live

    lax.while_loop(cond, body, (jnp.int32(1), live))
    o_ref[...] = acc_s[...].astype(o_ref.dtype)


def sb_stream(p, col_q, kbuf, vbuf, layer, *, tq, tk, dest=None, dest_col_block=0):
    _, heads, nrows, _ = kbuf.shape
    assert nrows % tq == 0 and tq % tk == 0
    scale = HEAD_DIM ** -0.5
    kv = pl.BlockSpec((None, None, nrows, HEAD_DIM), lambda h, i: (layer, h, 0, 0))
    body = functools.partial(_sb_stream_kernel, tk=tk, scale=scale)
    in_specs = [pl.BlockSpec((tq, HEAD_DIM), lambda h, i: (i, col_q + h)), kv, kv]
    args = (p, kbuf, vbuf)
    if dest is not None:
        body = _writes_into(body, len(in_specs))
        in_specs = in_specs + [pl.BlockSpec(memory_space=pl.ANY)]
        args = args + (dest,)
    return pl.pallas_call(
        body,
        grid=(heads, nrows // tq),
        in_specs=in_specs,
        out_specs=pl.BlockSpec((tq, HEAD_DIM), lambda h, i: (i, dest_col_block + h)),
        out_shape=jax.ShapeDtypeStruct((nrows, heads * HEAD_DIM) if dest is None else dest.shape, BF16),
        scratch_shapes=[pltpu.VMEM((tq, HEAD_DIM), F32), pltpu.VMEM((tq, 1), F32)],
        compiler_params=_params("parallel", "arbitrary"),
        name="sb_stream",
        **_dest_kwargs(dest, len(args) - 1),
    )(*args)


def _head_cols(pair_refs, h):
    return pair_refs[h // 2][:, (h % 2) * HEAD_DIM:(h % 2 + 1) * HEAD_DIM]


def _sb_past_chunk(qbs, kp_ref, vp_ref, accs, r_runs, upper, tk, scale):
    heads = len(qbs)
    for j in reversed(range(kp_ref.shape[1] // tk)):
        contribs, rowsums = _sb_blocks(qbs, [kp_ref[h, j * tk:(j + 1) * tk, :] for h in range(heads)],
                                       [vp_ref[h, j * tk:(j + 1) * tk, :] for h in range(heads)],
                                       [None] * heads, r_runs, upper, scale)
        accs = [a + c for a, c in zip(accs, contribs)]
        r_runs = [r + s for r, s in zip(r_runs, rowsums)]
    return accs, r_runs


def _sb_decode_first_kernel(*refs, heads, tk, scale):
    npair = heads // 2
    q_refs = refs[0:npair]
    kn_ref, vn_ref, kp_ref, vp_ref, o_ref, acc_ref, run_ref, live_ref, kpad_s, vpad_s = refs[npair:]
    t = o_ref.shape[0]
    upper = _strict_upper(tk)
    kpad_s[...] = jnp.zeros_like(kpad_s)
    vpad_s[...] = jnp.zeros_like(vpad_s)
    row = lax.broadcasted_iota(jnp.int32, (t, tk), 0)
    col = lax.broadcasted_iota(jnp.int32, (t, tk), 1)
    qbs = [_head_cols(q_refs, h).astype(BF16) for h in range(heads)]
    for h in range(heads):
        kpad_s[h, 0:t, :] = kn_ref[h]
        vpad_s[h, 0:t, :] = vn_ref[h]
    accs, r_runs = _sb_blocks(qbs, [kpad_s[h] for h in range(heads)], [vpad_s[h] for h in range(heads)],
                              [col < row] * heads, [jnp.zeros((t, 1), F32)] * heads, upper, scale)
    accs, r_runs = _sb_past_chunk(qbs, kp_ref, vp_ref, accs, r_runs, upper, tk, scale)
    live = None
    for h in range(heads):
        o_ref[:, h * HEAD_DIM:(h + 1) * HEAD_DIM] = accs[h].astype(o_ref.dtype)
        acc_ref[h] = accs[h]
        run_ref[h] = jnp.broadcast_to(r_runs[h], (t, HEAD_DIM))
        m = jnp.max(r_runs[h])
        live = m if live is None else jnp.maximum(live, m)
    live_ref[...] = jnp.full(live_ref.shape, live, F32)


def _sb_decode_rest_kernel(*refs, heads, tk, scale):
    npair = heads // 2
    q_refs = refs[0:npair]
    acc_in, run_in, kp_ref, vp_ref, o_ref, acc_s, run_s, live_s = refs[npair:]
    step = pl.program_id(1)
    upper = _strict_upper(tk)

    @pl.when(step == 0)
    def _():
        acc_s[...] = acc_in[...]
        run_s[...] = run_in[...]
        live_s[0] = jnp.max(run_in[...])

    @pl.when(live_s[0] > SB_DEAD_LOG)
    def _():
        qbs = [_head_cols(q_refs, h).astype(BF16) for h in range(heads)]
        accs, r_runs = _sb_past_chunk(qbs, kp_ref, vp_ref, [acc_s[h] for h in range(heads)],
                                      [run_s[h][:, 0:1] for h in range(heads)], upper, tk, scale)
        live = None
        for h in range(heads):
            acc_s[h] = accs[h]
            run_s[h] = jnp.broadcast_to(r_runs[h], run_s.shape[1:])
            m = jnp.max(r_runs[h])
            live = m if live is None else jnp.maximum(live, m)
        live_s[0] = live

    @pl.when(step == pl.num_programs(1) - 1)
    def _():
        for h in range(heads):
            o_ref[:, h * HEAD_DIM:(h + 1) * HEAD_DIM] = acc_s[h].astype(o_ref.dtype)


def sb_decode(p, row0, nb, t, col_q, k_new, v_new, k_past, v_past, layer, *, heads, tk, tp_first, tp_rest):
    npast = k_past.shape[3]
    nrest = (npast - tp_first) // tp_rest
    assert row0 % t == 0 and t <= tk and heads % 2 == 0 and tp_first % tk == 0 and tp_rest % tk == 0
    assert npast % tp_first == 0 and nrest * tp_rest == npast - tp_first
    assert col_q % 2 == 0
    rb0 = row0 // t
    npair = heads // 2
    scale = HEAD_DIM ** -0.5
    width = heads * HEAD_DIM
    o_shape = jax.ShapeDtypeStruct((nb * t, width), BF16)
    state_shape = jax.ShapeDtypeStruct((nb, heads, t, HEAD_DIM), F32)

    def new(cb, nax):
        if nax == 1:
            return [pl.BlockSpec((t, 2 * HEAD_DIM), lambda b, j=j: (rb0 + b, cb // 2 + j)) for j in range(npair)]
        return [pl.BlockSpec((t, 2 * HEAD_DIM), lambda b, c, j=j: (rb0 + b, cb // 2 + j)) for j in range(npair)]

    fresh = pl.BlockSpec((None, heads, t, HEAD_DIM), lambda b: (layer, 0, b, 0))
    newest = pl.BlockSpec((None, None, heads, tp_first, HEAD_DIM),
                          lambda b: (layer, b, 0, npast // tp_first - 1, 0))
    o_first, acc, run, live = pl.pallas_call(
        functools.partial(_sb_decode_first_kernel, heads=heads, tk=tk, scale=scale),
        grid=(nb,),
        in_specs=new(col_q, 1) + [fresh, fresh, newest, newest],
        out_specs=[pl.BlockSpec((t, width), lambda b: (b, 0)),
                   pl.BlockSpec((None, heads, t, HEAD_DIM), lambda b: (b, 0, 0, 0)),
                   pl.BlockSpec((None, heads, t, HEAD_DIM), lambda b: (b, 0, 0, 0)),
                   pl.BlockSpec((None, 8, HEAD_DIM), lambda b: (b, 0, 0))],
        out_shape=[o_shape, state_shape, state_shape, jax.ShapeDtypeStruct((nb, 8, HEAD_DIM), F32)],
        scratch_shapes=[pltpu.VMEM((heads, tk, HEAD_DIM), F32), pltpu.VMEM((heads, tk, HEAD_DIM), F32)],
        compiler_params=_params("parallel"),
        name="sb_decode_first",
    )(*([p] * npair), k_new, v_new, k_past, v_past)
    if nrest == 0:
        return o_first

    def older():
        state = pl.BlockSpec((None, heads, t, HEAD_DIM), lambda b, c: (b, 0, 0, 0))
        chunk = pl.BlockSpec((None, None, heads, tp_rest, HEAD_DIM), lambda b, c: (layer, b, 0, nrest - 1 - c, 0))
        return pl.pallas_call(
            functools.partial(_sb_decode_rest_kernel, heads=heads, tk=tk, scale=scale),
            grid=(nb, nrest),
            in_specs=new(col_q, 2) + [state, state, chunk, chunk],
            out_specs=pl.BlockSpec((t, width), lambda b, c: (b, 0)),
            out_shape=o_shape,
            scratch_shapes=[pltpu.VMEM((heads, t, HEAD_DIM), F32), pltpu.VMEM((heads, t, HEAD_DIM), F32),
                            pltpu.SMEM((1,), F32)],
            compiler_params=_params("parallel", "arbitrary"),
            name="sb_decode_rest",
        )(*([p] * npair), acc, run, k_past, v_past)

    return lax.cond(jnp.max(live) > SB_DEAD_LOG, older, lambda: o_first)


def kernel(x_prompt, x_sample, cache_sb_k, cache_sb_v, state_hgrn, state_pool, w_in, lb_param, hgrn_norm,
           w_pool, pool_scale, w_out, ln1_g, ln1_b, ln2_g, ln2_b, ffn_w1, ffn_w2, moe_router, moe_w1, moe_w2):
    bp, seq, d_model = x_prompt.shape
    nb, dseq, _ = x_sample.shape
    depth = w_in.shape[0]
    past_len = cache_sb_k.shape[2]
    a_heads = state_hgrn.shape[2]
    c_heads = cache_sb_k.shape[3]
    a_width = a_heads * HEAD_DIM
    c_width = c_heads * HEAD_DIM
    b_width = state_pool.shape[3]
    ne = moe_router.shape[2]
    d_ff = ffn_w2.shape[1]
    assert bp == 1 and a_width == c_width
    alpha = (2 * depth) ** 0.25
    np_rows = bp * seq
    ns_rows = nb * dseq
    m = np_rows + ns_rows

    pool_cb = 4 * a_width // b_width
    cq = (4 * a_width + b_width) // HEAD_DIM
    ck = cq + c_heads
    cv = ck + c_heads
    k_col0 = ck * HEAD_DIM
    u_col0 = 4 * a_width

    pl_soft = jax.nn.softmax(lb_param.astype(F32), axis=0)
    lb_all = (jnp.cumsum(pl_soft, axis=0) - pl_soft[0])[:, None, :]
    gnorm = hgrn_norm.reshape(depth, 1, a_width)
    pscale = pool_scale.reshape(depth, 1, b_width)
    vec = lambda a: a.reshape(depth, 1, d_model)
    ln1g, ln1b, ln2g, ln2b = vec(ln1_g), vec(ln1_b), vec(ln2_g), vec(ln2_b)
    router_p = jnp.pad(moe_router, ((0, 0), (0, 0), (0, HEAD_DIM - ne)))
    k_cache = jnp.transpose(cache_sb_k, (0, 1, 3, 2, 4))
    v_cache = jnp.transpose(cache_sb_v, (0, 1, 3, 2, 4))
    pool_hist = jnp.pad(state_pool, ((0, 0), (0, 0), (POOL_HALO - state_pool.shape[2], 0), (0, 0)))
    w_out2 = w_out.reshape(depth * d_model, d_model)
    ffn_w2_2 = ffn_w2.reshape(-1, d_model)

    x = jnp.concatenate([x_prompt.reshape(np_rows, d_model), x_sample.reshape(ns_rows, d_model)], axis=0)
    xb = x.astype(BF16)

    kp_buf = jnp.zeros((depth, c_heads, np_rows, HEAD_DIM), F32)
    vp_buf = jnp.zeros((depth, c_heads, np_rows, HEAD_DIM), F32)
    ks_buf = jnp.zeros((depth, c_heads, ns_rows, HEAD_DIM), F32)
    vs_buf = jnp.zeros((depth, c_heads, ns_rows, HEAD_DIM), F32)

    outs = {k: [] for k in ("ps", "pb", "ss", "sb")}
    for l in range(depth):
        p = matmul_layer(xb, w_in, l, k_col0, tm=2112, tn=256)
        kp_buf, vp_buf = kv_proj(xb, w_in, l, ck, cv, kp_buf, vp_buf, 0, tm=2048)
        ks_buf, vs_buf = kv_proj(xb, w_in, l, ck, cv, ks_buf, vs_buf, np_rows, tm=ns_rows)

        mix = jnp.zeros((m, d_model), BF16)
        mix, st_p = hgrn_mixer(p, 0, np_rows, lb_all, gnorm, l, None, per_block_state=False, width=a_width,
                               dest=mix)
        oa_s, st_s = hgrn_mixer(p, np_rows, ns_rows, lb_all, gnorm, l, state_hgrn[l],
                                per_block_state=True, width=a_width)
        mix = pool_stream(p, np_rows, pool_cb, w_pool, pscale, l, width=b_width, tile=512,
                          dest=mix, dest_col_block=(a_width + c_width) // b_width)
        u_s = p[np_rows:, u_col0:u_col0 + b_width].reshape(nb, dseq, b_width)
        ob_s = pool_batch(u_s, pool_hist[l], w_pool, pscale, l, start=past_len).reshape(ns_rows, b_width)
        mix = sb_stream(p, cq, kp_buf, vp_buf, l, tq=1024, tk=128, dest=mix, dest_col_block=a_heads)
        oc_s = sb_decode(p, np_rows, nb, dseq, cq, ks_buf, vs_buf, k_cache, v_cache, l,
                         heads=c_heads, tk=128, tp_first=256, tp_rest=768)
        mix = lax.dynamic_update_slice(mix, jnp.concatenate([oa_s, oc_s, ob_s], axis=1), (np_rows, 0))

        tk_out = 256
        na, nbk, nc = a_width // tk_out, b_width // tk_out, c_width // tk_out
        w_block_of = lambda kk: jnp.where(kk < na, kk, jnp.where(kk < na + nc, kk + nbk, kk - nc))
        h, hb = matmul_residual_ln(mix, w_out2, l * d_model, x, ln1g, ln1b, l, alpha=alpha, tm=1056, tk=tk_out,
                                   w_block_of=w_block_of)

        if l % 2 == 0:
            mid = swiglu_up(hb, ffn_w1, l // 2, tm=1056, tn=512)
            x, xb = matmul_residual_ln(mid, ffn_w2_2, (l // 2) * d_ff, h, ln2g, ln2b, l,
                                       alpha=alpha, tm=704, tk=512)
        else:
            x, xb = routed_ffn(h, router_p, moe_w1, moe_w2, l // 2, ln2g, ln2b, l, ne, alpha=alpha,
                               tm_route=528, tm_comb=256, tf=512)

        outs["ps"].append(st_p)
        outs["pb"].append(p[np_rows - (POOL_HALO - 1):np_rows, u_col0:u_col0 + b_width][None])
        outs["ss"].append(st_s)
        outs["sb"].append(u_s[:, dseq - (POOL_HALO - 1):, :])

    stack = lambda k: jnp.stack(outs[k])
    prompt_kv = lambda a: jnp.transpose(a, (0, 2, 1, 3)).reshape(depth, bp, seq, c_heads, HEAD_DIM)
    sample_kv = lambda a: jnp.transpose(a, (0, 2, 1, 3)).reshape(depth, nb, dseq, c_heads, HEAD_DIM)
    return (x[:np_rows].reshape(bp, seq, d_model), x[np_rows:].reshape(nb, dseq, d_model),
            stack("ps"), stack("pb"), prompt_kv(kp_buf), prompt_kv(vp_buf),
            stack("ss"), stack("sb"), sample_kv(ks_buf), sample_kv(vs_buf))
```

```python
import functools

import jax
import jax.numpy as jnp
from jax import lax
from jax.experimental import pallas as pl
from jax.experimental.pallas import tpu as pltpu

F32 = jnp.float32
BF16 = jnp.bfloat16

HEAD_DIM = 128
HGRN_BLOCK_LOG2 = 4
HGRN_BLOCK = 1 << HGRN_BLOCK_LOG2
HGRN_TILE = 256
POOL_WINDOWS = (2, 4, 8, 16)
POOL_HALO = 16
TOP_K = 2
LN_EPS = 1e-5
RMS_EPS = 1e-6
LB_FLOOR = 1e-30
SB_DEAD_LOG = -104.0
VMEM_LIMIT_BYTES = 56 * 1024 * 1024


def _params(*sem):
    return pltpu.CompilerParams(dimension_semantics=sem, vmem_limit_bytes=VMEM_LIMIT_BYTES)


def _dot(a, b):
    return jnp.dot(a, b, preferred_element_type=F32)


def _dot_nt(a, b):
    return lax.dot_general(a, b, (((1,), (1,)), ((), ())), preferred_element_type=F32)


def _split3(x):
    h1 = x.astype(BF16)
    r1 = x - h1.astype(F32)
    h2 = r1.astype(BF16)
    h3 = (r1 - h2.astype(F32)).astype(BF16)
    return h1, h2, h3


def _sigmoid(x):
    return 1.0 / (1.0 + jnp.exp(-x))


def _log_sigmoid(x):
    return jnp.minimum(x, 0.0) - jnp.log1p(jnp.exp(-jnp.abs(x)))


def _mm_kernel(a_ref, w_ref, o_ref):
    o_ref[...] = _dot(a_ref[...], w_ref[...].astype(BF16)).astype(o_ref.dtype)


def matmul_layer(a, w, layer, n, *, tm, tn):
    m, k = a.shape
    assert m % tm == 0 and n % tn == 0 and n <= w.shape[2]
    return pl.pallas_call(
        _mm_kernel,
        grid=(m // tm, n // tn),
        in_specs=[pl.BlockSpec((tm, k), lambda i, j: (i, 0)),
                  pl.BlockSpec((None, k, tn), lambda i, j: (layer, 0, j))],
        out_specs=pl.BlockSpec((tm, tn), lambda i, j: (i, j)),
        out_shape=jax.ShapeDtypeStruct((m, n), F32),
        compiler_params=_params("parallel", "arbitrary"),
        name="in_proj",
    )(a, w)


def _kv_proj_kernel(x_ref, wk_ref, wv_ref, kin_any, vin_any, k_ref, v_ref):
    del kin_any, vin_any
    x = x_ref[...]
    for w_ref, o_ref in ((wk_ref, k_ref), (wv_ref, v_ref)):
        r = _dot(x, w_ref[...].astype(BF16))
        o_ref[0] = r[:, :HEAD_DIM]
        o_ref[1] = r[:, HEAD_DIM:]


def kv_proj(a, w, layer, col_k, col_v, kbuf, vbuf, row0, *, tm):
    k = a.shape[1]
    nlayer, heads, r, _ = kbuf.shape
    assert r % tm == 0 and row0 % tm == 0 and heads % 2 == 0 and col_k % 2 == 0 and col_v % 2 == 0
    rb0 = row0 // tm
    out = pl.BlockSpec((None, 2, tm, HEAD_DIM), lambda i, j: (layer, j, i, 0))
    anyspec = pl.BlockSpec(memory_space=pl.ANY)
    return pl.pallas_call(
        _kv_proj_kernel,
        grid=(r // tm, heads // 2),
        in_specs=[pl.BlockSpec((tm, k), lambda i, j: (rb0 + i, 0)),
                  pl.BlockSpec((None, k, 2 * HEAD_DIM), lambda i, j: (layer, 0, col_k // 2 + j)),
                  pl.BlockSpec((None, k, 2 * HEAD_DIM), lambda i, j: (layer, 0, col_v // 2 + j)),
                  anyspec, anyspec],
        out_specs=[out, out],
        out_shape=[jax.ShapeDtypeStruct(kbuf.shape, F32), jax.ShapeDtypeStruct(vbuf.shape, F32)],
        input_output_aliases={3: 0, 4: 1},
        compiler_params=_params("parallel", "arbitrary"),
        name="kv_proj",
    )(a, w, w, kbuf, vbuf)


def _swiglu_kernel(x_ref, wg_ref, wu_ref, o_ref):
    x = x_ref[...]
    g = _dot(x, wg_ref[...].astype(BF16))
    u = _dot(x, wu_ref[...].astype(BF16))
    o_ref[...] = (g * _sigmoid(g) * u).astype(o_ref.dtype)


def swiglu_up(x, w1, layer, *, tm, tn):
    m, k = x.shape
    f = w1.shape[-1] // 2
    assert m % tm == 0 and f % tn == 0
    nj = f // tn
    return pl.pallas_call(
        _swiglu_kernel,
        grid=(m // tm, nj),
        in_specs=[pl.BlockSpec((tm, k), lambda i, j: (i, 0)),
                  pl.BlockSpec((None, k, tn), lambda i, j: (layer, 0, j)),
                  pl.BlockSpec((None, k, tn), lambda i, j: (layer, 0, nj + j))],
        out_specs=pl.BlockSpec((tm, tn), lambda i, j: (i, j)),
        out_shape=jax.ShapeDtypeStruct((m, f), BF16),
        compiler_params=_params("parallel", "arbitrary"),
        name="swiglu_up",
    )(x, w1, w1)


def _mm_ln_kernel(a_ref, w_ref, r_ref, g_ref, b_ref, of_ref, ob_ref, *, alpha, nk):
    kk = pl.program_id(1)

    @pl.when(kk == 0)
    def _():
        of_ref[...] = jnp.zeros_like(of_ref)

    of_ref[...] += _dot(a_ref[...], w_ref[...].astype(BF16))

    @pl.when(kk == nk - 1)
    def _():
        r = alpha * r_ref[...] + of_ref[...]
        mu = jnp.mean(r, axis=-1, keepdims=True)
        c = r - mu
        var = jnp.mean(c * c, axis=-1, keepdims=True)
        y = c * lax.rsqrt(var + LN_EPS) * g_ref[...] + b_ref[...]
        of_ref[...] = y
        ob_ref[...] = y.astype(BF16)


def matmul_residual_ln(a, w, w_row0, resid, g, b, layer, *, alpha, tm, tk):
    m, k = a.shape
    n = w.shape[1]
    assert m % tm == 0 and k % tk == 0 and w_row0 % tk == 0
    nk = k // tk
    kb0 = w_row0 // tk
    row = pl.BlockSpec((tm, n), lambda i, kk: (i, 0))
    vec = pl.BlockSpec((None, 1, n), lambda i, kk: (layer, 0, 0))
    return pl.pallas_call(
        functools.partial(_mm_ln_kernel, alpha=alpha, nk=nk),
        grid=(m // tm, nk),
        in_specs=[pl.BlockSpec((tm, tk), lambda i, kk: (i, kk)),
                  pl.BlockSpec((tk, n), lambda i, kk: (kb0 + kk, 0)),
                  row, vec, vec],
        out_specs=[row, row],
        out_shape=[jax.ShapeDtypeStruct((m, n), F32), jax.ShapeDtypeStruct((m, n), BF16)],
        compiler_params=_params("parallel", "arbitrary"),
        name="proj_ln",
    )(a, w, resid, g, b)


def _seg_mm_ln_kernel(a_ref, w_ref, r_ref, g_ref, b_ref, of_ref, ob_ref, *, alpha, segments):
    y = None
    for a0, w0, width in segments:
        part = _dot(a_ref[:, a0:a0 + width], w_ref[w0:w0 + width, :].astype(BF16))
        y = part if y is None else y + part
    r = alpha * r_ref[...] + y
    mu = jnp.mean(r, axis=-1, keepdims=True)
    c = r - mu
    var = jnp.mean(c * c, axis=-1, keepdims=True)
    out = c * lax.rsqrt(var + LN_EPS) * g_ref[...] + b_ref[...]
    of_ref[...] = out
    ob_ref[...] = out.astype(BF16)


def segment_matmul_residual_ln(a, w, resid, g, b, layer, segments, *, alpha, tm):
    m, k = a.shape
    n = w.shape[2]
    assert m % tm == 0 and sum(s[2] for s in segments) == k == w.shape[1]
    row = pl.BlockSpec((tm, n), lambda i: (i, 0))
    vec = pl.BlockSpec((None, 1, n), lambda i: (layer, 0, 0))
    return pl.pallas_call(
        functools.partial(_seg_mm_ln_kernel, alpha=alpha, segments=tuple(segments)),
        grid=(m // tm,),
        in_specs=[pl.BlockSpec((tm, k), lambda i: (i, 0)),
                  pl.BlockSpec((None, k, n), lambda i: (layer, 0, 0)),
                  row, vec, vec],
        out_specs=[row, row],
        out_shape=[jax.ShapeDtypeStruct((m, n), F32), jax.ShapeDtypeStruct((m, n), BF16)],
        compiler_params=_params("parallel"),
        name="out_proj_ln",
    )(a, w, resid, g, b)


ROUTE_E1, ROUTE_E2, ROUTE_G1, ROUTE_G2, ROUTE_R1, ROUTE_R2 = range(6)


def _router_kernel(h_ref, r_ref, o_ref, cnt_ref, *, ne):
    step = pl.program_id(0)

    @pl.when(step == 0)
    def _():
        cnt_ref[...] = jnp.zeros_like(cnt_ref)

    logits = jnp.dot(h_ref[...], r_ref[...], preferred_element_type=F32,
                     precision=lax.Precision.HIGHEST)
    tm = logits.shape[0]
    lane = lax.broadcasted_iota(jnp.int32, logits.shape, 1)
    neg = jnp.float32(-jnp.inf)
    x = jnp.where(lane < ne, logits, neg)
    m1 = jnp.max(x, axis=-1, keepdims=True)
    i1 = jnp.min(jnp.where(x == m1, lane, ne), axis=-1, keepdims=True)
    x2 = jnp.where(lane == i1, neg, x)
    m2 = jnp.max(x2, axis=-1, keepdims=True)
    i2 = jnp.min(jnp.where(x2 == m2, lane, ne), axis=-1, keepdims=True)
    e2 = jnp.exp(m2 - m1)
    g1 = 1.0 / (1.0 + e2)
    g2 = e2 / (1.0 + e2)

    hit1 = lane == i1
    hit2 = lane == i2
    onehot = jnp.where(hit1, 1.0, 0.0) + jnp.where(hit2, 1.0, 0.0)
    t_i = lax.broadcasted_iota(jnp.int32, (tm, tm), 0)
    s_i = lax.broadcasted_iota(jnp.int32, (tm, tm), 1)
    before = jnp.where(s_i < t_i, 1.0, 0.0).astype(BF16)
    prior = _dot(before, onehot.astype(BF16)) + cnt_ref[...]
    r1 = jnp.sum(jnp.where(hit1, prior, 0.0), axis=-1, keepdims=True)
    r2 = jnp.sum(jnp.where(hit2, prior, 0.0), axis=-1, keepdims=True)
    cnt_ref[...] += jnp.sum(onehot, axis=0, keepdims=True)

    rec = jnp.zeros(logits.shape, F32)
    for ln, val in ((ROUTE_E1, i1.astype(F32)), (ROUTE_E2, i2.astype(F32)), (ROUTE_G1, g1),
                    (ROUTE_G2, g2), (ROUTE_R1, r1), (ROUTE_R2, r2)):
        rec = jnp.where(lane == ln, val, rec)
    o_ref[...] = rec


def router_top2(h, router_padded, layer, ne, *, tm):
    m, k = h.shape
    return pl.pallas_call(
        functools.partial(_router_kernel, ne=ne),
        grid=(m // tm,),
        in_specs=[pl.BlockSpec((tm, k), lambda i: (i, 0)),
                  pl.BlockSpec((None, k, HEAD_DIM), lambda i: (layer, 0, 0))],
        out_specs=[pl.BlockSpec((tm, HEAD_DIM), lambda i: (i, 0)),
                   pl.BlockSpec((1, HEAD_DIM), lambda i: (0, 0))],
        out_shape=[jax.ShapeDtypeStruct((m, HEAD_DIM), F32), jax.ShapeDtypeStruct((1, HEAD_DIM), F32)],
        compiler_params=_params("arbitrary"),
        name="router",
    )(h, router_padded)


MOE_HALF = 512
MOE_ITEM = 2 * MOE_HALF


def _for_item_rows(nrow, compute):
    lo = 0
    for n in (MOE_ITEM // 8, MOE_ITEM // 4, MOE_ITEM // 2, MOE_ITEM):
        pl.when(jnp.logical_and(nrow > lo, nrow <= n))(functools.partial(compute, n))
        lo = n


def _gather_rows_kernel(nrow_ref, src_ref, h_any, o_ref, buf, sem):
    half = pl.program_id(0)
    nrow = buf.shape[0]
    live = nrow_ref[half // 2] > (1 - half % 2) * MOE_HALF

    @pl.when(live)
    def _():
        def start(i, c):
            for u in range(2):
                r = 2 * i + u
                pltpu.make_async_copy(h_any.at[pl.ds(src_ref[0, r], 1)], buf.at[pl.ds(r, 1)],
                                      sem).start(priority=u)
            return c

        lax.fori_loop(0, nrow // 2, start, 0, unroll=4)

        def wait(r, c):
            pltpu.make_async_copy(h_any.at[pl.ds(0, 1)], buf.at[pl.ds(r, 1)], sem).wait()
            return c

        lax.fori_loop(0, nrow, wait, 0, unroll=8)
        o_ref[...] = buf[...].astype(o_ref.dtype)

    @pl.when(jnp.logical_not(live))
    def _():
        o_ref[...] = jnp.zeros_like(o_ref)


def gather_rows(h, src, item_rows):
    nhalf = src.shape[0]
    d = h.shape[1]
    grid_spec = pltpu.PrefetchScalarGridSpec(
        num_scalar_prefetch=1,
        grid=(nhalf,),
        in_specs=[pl.BlockSpec((None, 1, MOE_HALF), lambda i, nr: (i, 0, 0), memory_space=pltpu.SMEM),
                  pl.BlockSpec(memory_space=pl.ANY)],
        out_specs=pl.BlockSpec((MOE_HALF, d), lambda i, nr: (i, 0)),
        scratch_shapes=[pltpu.VMEM((MOE_HALF, d), F32), pltpu.SemaphoreType.DMA(())],
    )
    return pl.pallas_call(
        _gather_rows_kernel,
        grid_spec=grid_spec,
        out_shape=jax.ShapeDtypeStruct((nhalf * MOE_HALF, d), BF16),
        compiler_params=_params("arbitrary"),
        name="moe_gather",
    )(item_rows, src, h)


def _grouped_swiglu_kernel(ie_ref, nrow_ref, x_ref, wg_ref, wu_ref, o_ref):
    nrow = nrow_ref[pl.program_id(1)]

    def compute(n):
        x = x_ref[MOE_ITEM - n:, :]
        g = _dot(x, wg_ref[...].astype(BF16))
        u = _dot(x, wu_ref[...].astype(BF16))
        o_ref[MOE_ITEM - n:, :] = (g * _sigmoid(g) * u).astype(o_ref.dtype)
        if n < MOE_ITEM:
            o_ref[:MOE_ITEM - n, :] = jnp.zeros((MOE_ITEM - n, o_ref.shape[1]), o_ref.dtype)

    _for_item_rows(nrow, compute)

    @pl.when(nrow == 0)
    def _():
        o_ref[...] = jnp.zeros_like(o_ref)


def grouped_swiglu(xs, w1, layer, item_expert, item_rows, *, tn):
    rows, k = xs.shape
    nitem = rows // MOE_ITEM
    f = w1.shape[-1] // 2
    nj = f // tn
    grid_spec = pltpu.PrefetchScalarGridSpec(
        num_scalar_prefetch=2,
        grid=(nj, nitem),
        in_specs=[pl.BlockSpec((MOE_ITEM, k), lambda j, it, ie, nr: (jnp.where(nr[it] > 0, it, 0), 0)),
                  pl.BlockSpec((None, None, k, tn), lambda j, it, ie, nr: (layer, ie[it], 0, j)),
                  pl.BlockSpec((None, None, k, tn), lambda j, it, ie, nr: (layer, ie[it], 0, nj + j))],
        out_specs=pl.BlockSpec((MOE_ITEM, tn), lambda j, it, ie, nr: (it, j)),
    )
    return pl.pallas_call(
        _grouped_swiglu_kernel,
        grid_spec=grid_spec,
        out_shape=jax.ShapeDtypeStruct((rows, f), BF16),
        compiler_params=_params("arbitrary", "arbitrary"),
        name="moe_up",
    )(item_expert, item_rows, xs, w1, w1)


def _grouped_down_kernel(ie_ref, nrow_ref, a_ref, w_ref, o_ref):
    nrow = nrow_ref[pl.program_id(0)]

    @pl.when(pl.program_id(1) == 0)
    def _():
        o_ref[...] = jnp.zeros_like(o_ref)

    def compute(n):
        o_ref[MOE_ITEM - n:, :] += _dot(a_ref[MOE_ITEM - n:, :], w_ref[...].astype(BF16))

    _for_item_rows(nrow, compute)


def grouped_down(mid, w2, layer, item_expert, item_rows, *, tk):
    rows, f = mid.shape
    nitem = rows // MOE_ITEM
    d = w2.shape[-1]
    nk = f // tk

    def kidx(it, kk, nr):
        return jnp.where(nr[it] > 0, kk, nk - 1)

    grid_spec = pltpu.PrefetchScalarGridSpec(
        num_scalar_prefetch=2,
        grid=(nitem, nk),
        in_specs=[pl.BlockSpec((MOE_ITEM, tk), lambda it, kk, ie, nr: (it, kidx(it, kk, nr))),
                  pl.BlockSpec((None, None, tk, d), lambda it, kk, ie, nr: (layer, ie[it], kidx(it, kk, nr), 0))],
        out_specs=pl.BlockSpec((MOE_ITEM, d), lambda it, kk, ie, nr: (it, 0)),
    )
    return pl.pallas_call(
        _grouped_down_kernel,
        grid_spec=grid_spec,
        out_shape=jax.ShapeDtypeStruct((rows, d), F32),
        compiler_params=_params("arbitrary", "arbitrary"),
        name="moe_down",
    )(item_expert, item_rows, mid, w2)


def _combine_ln_kernel(dst_ref, y_any, route_ref, r_ref, g_ref, b_ref, of_ref, ob_ref, buf, sem, *, alpha):
    tm = r_ref.shape[0]

    def start(i, c):
        for slot in range(TOP_K):
            pltpu.make_async_copy(y_any.at[pl.ds(dst_ref[0, slot * tm + i], 1)],
                                  buf.at[slot, pl.ds(i, 1)], sem).start(priority=slot)
        return c

    lax.fori_loop(0, tm, start, 0, unroll=8)

    def wait(i, c):
        for slot in range(TOP_K):
            pltpu.make_async_copy(y_any.at[pl.ds(0, 1)], buf.at[slot, pl.ds(i, 1)], sem).wait()
        return c

    lax.fori_loop(0, tm, wait, 0, unroll=8)
    route = route_ref[...]
    g1 = route[:, ROUTE_G1:ROUTE_G1 + 1]
    g2 = route[:, ROUTE_G2:ROUTE_G2 + 1]
    r = alpha * r_ref[...] + (g1 * buf[0] + g2 * buf[1])
    mu = jnp.mean(r, axis=-1, keepdims=True)
    c = r - mu
    var = jnp.mean(c * c, axis=-1, keepdims=True)
    y = c * lax.rsqrt(var + LN_EPS) * g_ref[...] + b_ref[...]
    of_ref[...] = y
    ob_ref[...] = y.astype(BF16)


def combine_ln(ys, dst, route, resid, g, b, layer, *, alpha, tm):
    m, d = resid.shape
    row = pl.BlockSpec((tm, d), lambda i: (i, 0))
    vec = pl.BlockSpec((None, 1, d), lambda i: (layer, 0, 0))
    return pl.pallas_call(
        functools.partial(_combine_ln_kernel, alpha=alpha),
        grid=(m // tm,),
        in_specs=[pl.BlockSpec((None, 1, TOP_K * tm), lambda i: (i, 0, 0), memory_space=pltpu.SMEM),
                  pl.BlockSpec(memory_space=pl.ANY),
                  pl.BlockSpec((tm, HEAD_DIM), lambda i: (i, 0)),
                  row, vec, vec],
        out_specs=[row, row],
        out_shape=[jax.ShapeDtypeStruct((m, d), F32), jax.ShapeDtypeStruct((m, d), BF16)],
        scratch_shapes=[pltpu.VMEM((TOP_K, tm, d), F32), pltpu.SemaphoreType.DMA(())],
        compiler_params=_params("arbitrary"),
        name="moe_combine_ln",
    )(dst, ys, route, resid, g, b)


def routed_ffn(h, router_padded, w1, w2, layer, g, b, ln_layer, ne, *, alpha, tm_route, tm_comb, tf):
    m, d = h.shape
    route, counts = router_top2(h, router_padded, layer, ne, tm=tm_route)
    cnt = counts[0, :ne].astype(jnp.int32)
    nitem_e = (cnt + MOE_ITEM - 1) // MOE_ITEM
    item_end = jnp.cumsum(nitem_e)
    item_start = item_end - nitem_e
    row_off = item_end * MOE_ITEM - cnt
    n_items = item_end[-1]
    max_items = (TOP_K * m + MOE_ITEM - 1) // MOE_ITEM + ne
    item = jnp.arange(max_items, dtype=jnp.int32)
    item_expert = jnp.minimum(jnp.searchsorted(item_end, item, side="right"), ne - 1).astype(jnp.int32)
    first_rows = cnt - (nitem_e - 1) * MOE_ITEM
    item_rows = jnp.where(item == item_start[item_expert], first_rows[item_expert], MOE_ITEM)
    item_rows = jnp.where(item < n_items, item_rows, 0).astype(jnp.int32)
    item_expert = jnp.where(item < n_items, item_expert, item_expert[jnp.maximum(n_items - 1, 0)])
    e1 = route[:, ROUTE_E1].astype(jnp.int32)
    e2 = route[:, ROUTE_E2].astype(jnp.int32)
    dst1 = row_off[e1] + route[:, ROUTE_R1].astype(jnp.int32)
    dst2 = row_off[e2] + route[:, ROUTE_R2].astype(jnp.int32)
    tok = jnp.arange(m, dtype=jnp.int32)
    src = jnp.zeros((max_items * MOE_ITEM,), jnp.int32).at[jnp.concatenate([dst1, dst2])].set(
        jnp.concatenate([tok, tok]), unique_indices=True)
    dst = jnp.concatenate([dst1.reshape(m // tm_comb, 1, tm_comb), dst2.reshape(m // tm_comb, 1, tm_comb)], axis=2)

    xs = gather_rows(h, src.reshape(2 * max_items, 1, MOE_HALF), item_rows)
    mid = grouped_swiglu(xs, w1, layer, item_expert, item_rows, tn=tf)
    ys = grouped_down(mid, w2, layer, item_expert, item_rows, tk=tf)
    return combine_ln(ys, dst, route, h, g, b, ln_layer, alpha=alpha, tm=tm_comb)


def _hgrn_kernel(q_ref, z_ref, v_ref, g_ref, lb_ref, gn_ref, s0_ref, o_ref, sout_ref,
                 st_s, q_s, b_s, c_s, qt_s, kd_s, vb_s, ebl_s, upd_s, sb_s, *, heads, per_block_state):
    rows, width = q_ref.shape
    nblk = rows // HGRN_BLOCK
    step = pl.program_id(0)

    lb = lb_ref[...]
    log_lb = jnp.log(jnp.maximum(lb, LB_FLOOR))
    log_1m_lb = jnp.log1p(-lb)
    z = z_ref[...]
    ls = _log_sigmoid(z)
    lo = log_1m_lb + ls
    logf = jnp.maximum(log_lb, lo) + jnp.log1p(jnp.exp(-jnp.abs(log_lb - lo)))
    logk = log_1m_lb + (ls - z)
    qraw = q_ref[...]
    q = qraw * _sigmoid(qraw)

    r_i = lax.broadcasted_iota(jnp.int32, (rows, rows), 0)
    c_i = lax.broadcasted_iota(jnp.int32, (rows, rows), 1)
    same = (r_i >> HGRN_BLOCK_LOG2) == (c_i >> HGRN_BLOCK_LOG2)
    tri = jnp.where(same & (c_i <= r_i), 1.0, 0.0).astype(BF16)
    blk = jnp.where(same, 1.0, 0.0).astype(BF16)
    parts = _split3(logf)
    b = _dot(tri, parts[0]) + _dot(tri, parts[1]) + _dot(tri, parts[2])
    btot = _dot(blk, parts[0]) + _dot(blk, parts[1]) + _dot(blk, parts[2])
    c = b - logk
    q_s[...] = q
    b_s[...] = b
    c_s[...] = c
    qt_s[...] = (q * jnp.exp(b)).astype(BF16)
    kd_s[...] = jnp.exp(btot - c).astype(BF16)
    ebl_s[...] = jnp.exp(btot)
    vb_s[...] = v_ref[...].astype(BF16)

    def block_input(blk_i, carry):
        rs = pl.ds(pl.multiple_of(blk_i * HGRN_BLOCK, HGRN_BLOCK), HGRN_BLOCK)
        for h in range(heads):
            hc = slice(h * HEAD_DIM, (h + 1) * HEAD_DIM)
            upd_s[blk_i, h] = lax.dot_general(vb_s[rs, hc], kd_s[rs, hc], (((0,), (0,)), ((), ())),
                                              preferred_element_type=F32)
        return carry

    lax.fori_loop(0, nblk, block_input, 0, unroll=4)

    if not per_block_state:
        @pl.when(step == 0)
        def _():
            st_s[...] = jnp.zeros_like(st_s)

    def recur(blk_i, carry):
        r0 = pl.multiple_of(blk_i * HGRN_BLOCK, HGRN_BLOCK)
        for h in range(heads):
            hc = slice(h * HEAD_DIM, (h + 1) * HEAD_DIM)
            st = s0_ref[blk_i, h].T if per_block_state else st_s[h]
            sb_s[blk_i, h] = st.astype(BF16)
            st_new = st * ebl_s[pl.ds(r0, 1), hc] + upd_s[blk_i, h]
            if per_block_state:
                sout_ref[blk_i, h] = st_new.T
            else:
                st_s[h] = st_new
        return carry

    lax.fori_loop(0, nblk, recur, 0)

    if not per_block_state:
        @pl.when(step == pl.num_programs(0) - 1)
        def _():
            for h in range(heads):
                sout_ref[0, h] = st_s[h].T

    pos = lax.broadcasted_iota(jnp.int32, (HGRN_BLOCK, width), 0)
    ones = jnp.ones((HEAD_DIM, HEAD_DIM), BF16)
    gn = gn_ref[...]

    per_trip = 2

    def block_out(trip, carry):
        blks = [trip * per_trip + u for u in range(per_trip)]
        rss = [pl.ds(pl.multiple_of(bi * HGRN_BLOCK, HGRN_BLOCK), HGRN_BLOCK) for bi in blks]
        accs = [jnp.concatenate([_dot_nt(qt_s[rs, h * HEAD_DIM:(h + 1) * HEAD_DIM], sb_s[bi, h])
                                 for h in range(heads)], axis=1) for bi, rs in zip(blks, rss)]
        walls = []
        for rs in rss:
            qb, bb, cb = q_s[rs, :], b_s[rs, :], c_s[rs, :]
            ws = []
            for d in range(HGRN_BLOCK):
                w = qb * jnp.exp(bb - (cb if d == 0 else pltpu.roll(cb, d, 0)))
                if d > 0:
                    w = jnp.where(pos >= d, w, 0.0)
                ws.append(w.astype(BF16))
            walls.append(jnp.concatenate(ws, axis=0))
        atts = [jnp.concatenate([_dot(wall[:, h * HEAD_DIM:(h + 1) * HEAD_DIM], ones) for h in range(heads)], axis=1)
                for wall in walls]
        for rs, acc, att in zip(rss, accs, atts):
            vb = v_ref[rs, :]
            for d in range(HGRN_BLOCK):
                acc = acc + att[d * HGRN_BLOCK:(d + 1) * HGRN_BLOCK, :] * (vb if d == 0 else pltpu.roll(vb, d, 0))
            ms = jnp.concatenate(
                [jnp.broadcast_to(jnp.sum(jnp.square(acc[:, h * HEAD_DIM:(h + 1) * HEAD_DIM]), axis=1, keepdims=True),
                                  (HGRN_BLOCK, HEAD_DIM)) for h in range(heads)], axis=1) * (1.0 / HEAD_DIM)
            gate = g_ref[rs, :]
            o_ref[rs, :] = (acc * lax.rsqrt(ms + RMS_EPS) * gn * (gate * _sigmoid(gate))).astype(o_ref.dtype)
        return carry

    lax.fori_loop(0, nblk // per_trip, block_out, 0)


def _writes_into(kernel, n_in):
    def wrapped(*refs):
        return kernel(*refs[:n_in], *refs[n_in + 1:])
    return wrapped


def _dest_kwargs(dest, n_in):
    return {} if dest is None else {"input_output_aliases": {n_in: 0}}


def hgrn_mixer(p, row0, nrows, lb, gnorm, layer, s0, *, per_block_state, width, dest=None):
    heads = width // HEAD_DIM
    tile = HGRN_TILE
    assert nrows % tile == 0 and row0 % tile == 0
    rb0 = row0 // tile
    nt = nrows // tile
    if per_block_state:
        assert nt == 1
        nstate = s0.shape[0]
    else:
        nstate = 1
        s0 = jnp.zeros((1, heads, HEAD_DIM, HEAD_DIM), F32)

    def col(cb):
        return pl.BlockSpec((tile, width), lambda i: (rb0 + i, cb))

    vec = pl.BlockSpec((None, 1, width), lambda i: (layer, 0, 0))
    state = pl.BlockSpec((nstate, heads, HEAD_DIM, HEAD_DIM), lambda i: (0, 0, 0, 0))
    body = functools.partial(_hgrn_kernel, heads=heads, per_block_state=per_block_state)
    in_specs = [col(0), col(1), col(2), col(3), vec, vec, state]
    args = (p, p, p, p, lb, gnorm, s0)
    if dest is not None:
        body = _writes_into(body, len(in_specs))
        in_specs = in_specs + [pl.BlockSpec(memory_space=pl.ANY)]
        args = args + (dest,)
    return pl.pallas_call(
        body,
        grid=(nt,),
        in_specs=in_specs,
        out_specs=[pl.BlockSpec((tile, width), lambda i: (i, 0)), state],
        out_shape=[jax.ShapeDtypeStruct((nrows, width) if dest is None else dest.shape, BF16),
                   jax.ShapeDtypeStruct((nstate, heads, HEAD_DIM, HEAD_DIM), F32)],
        **_dest_kwargs(dest, len(args) - 1),
        scratch_shapes=[pltpu.VMEM((heads, HEAD_DIM, HEAD_DIM), F32),
                        pltpu.VMEM((tile, width), F32),
                        pltpu.VMEM((tile, width), F32),
                        pltpu.VMEM((tile, width), F32),
                        pltpu.VMEM((tile, width), BF16),
                        pltpu.VMEM((tile, width), BF16),
                        pltpu.VMEM((tile, width), BF16),
                        pltpu.VMEM((tile, width), F32),
                        pltpu.VMEM((tile // HGRN_BLOCK, heads, HEAD_DIM, HEAD_DIM), F32),
                        pltpu.VMEM((tile // HGRN_BLOCK, heads, HEAD_DIM, HEAD_DIM), BF16)],
        compiler_params=_params("arbitrary"),
        name="hgrn2",
    )(*args)


def _pool_tile(xp, pos, w_ref, scale):
    sums = {1: xp}
    s = xp
    w = 1
    while w < POOL_WINDOWS[-1]:
        s = s + pltpu.roll(s, w, 0)
        w *= 2
        sums[w] = s
    gd = HEAD_DIM
    outs = []
    for g, win in enumerate(POOL_WINDOWS):
        sl = slice(g * gd, (g + 1) * gd)
        cnt = jnp.clip(pos + 1, 1, win).astype(F32)
        d = sums[win][:, sl] / cnt - xp[:, sl]
        outs.append(_dot(d.astype(BF16), w_ref[g].astype(BF16)))
    return jnp.concatenate(outs, axis=1) * scale


def _pool_stream_kernel(u_ref, w_ref, sc_ref, o_ref, xp_s):
    step = pl.program_id(0)
    rows = u_ref.shape[0]

    @pl.when(step == 0)
    def _():
        xp_s[0:POOL_HALO, :] = jnp.zeros((POOL_HALO, xp_s.shape[1]), F32)

    xp_s[POOL_HALO:, :] = u_ref[...]
    xp = xp_s[...]
    pos = step * rows - POOL_HALO + lax.broadcasted_iota(jnp.int32, (rows + POOL_HALO, 1), 0)
    y = _pool_tile(xp, pos, w_ref, sc_ref[...])
    o_ref[...] = y[POOL_HALO:, :].astype(o_ref.dtype)
    xp_s[0:POOL_HALO, :] = xp[rows:, :]


def _pool_batch_kernel(u_ref, buf_ref, w_ref, sc_ref, o_ref, xp_s, *, start):
    nb, t, c = u_ref.shape
    seg = POOL_HALO + t
    for bi in range(nb):
        xp_s[bi * seg:bi * seg + POOL_HALO, :] = buf_ref[bi]
        xp_s[bi * seg + POOL_HALO:(bi + 1) * seg, :] = u_ref[bi]
    r = lax.broadcasted_iota(jnp.int32, (nb * seg, 1), 0) & (seg - 1)
    y = _pool_tile(xp_s[...], start - POOL_HALO + r, w_ref, sc_ref[...])
    for bi in range(nb):
        o_ref[bi] = y[bi * seg + POOL_HALO:(bi + 1) * seg, :].astype(o_ref.dtype)


def pool_stream(p, nrows, col_block, w_pool, pool_scale, layer, *, width, tile, dest=None, dest_col_block=0):
    assert nrows % tile == 0
    body = _pool_stream_kernel
    in_specs = [pl.BlockSpec((tile, width), lambda i: (i, col_block)),
                pl.BlockSpec((None,) + w_pool.shape[1:], lambda i: (layer, 0, 0, 0)),
                pl.BlockSpec((None, 1, width), lambda i: (layer, 0, 0))]
    args = (p, w_pool, pool_scale)
    if dest is not None:
        body = _writes_into(body, len(in_specs))
        in_specs = in_specs + [pl.BlockSpec(memory_space=pl.ANY)]
        args = args + (dest,)
    return pl.pallas_call(
        body,
        grid=(nrows // tile,),
        in_specs=in_specs,
        out_specs=pl.BlockSpec((tile, width), lambda i: (i, dest_col_block)),
        out_shape=jax.ShapeDtypeStruct((nrows, width) if dest is None else dest.shape, BF16),
        scratch_shapes=[pltpu.VMEM((tile + POOL_HALO, width), F32)],
        compiler_params=_params("arbitrary"),
        name="pool_stream",
        **_dest_kwargs(dest, len(args) - 1),
    )(*args)


def pool_batch(u, buf, w_pool, pool_scale, layer, *, start):
    nb, t, c = u.shape
    return pl.pallas_call(
        functools.partial(_pool_batch_kernel, start=start),
        grid=(1,),
        in_specs=[pl.BlockSpec((nb, t, c), lambda i: (0, 0, 0)),
                  pl.BlockSpec((nb, POOL_HALO, c), lambda i: (0, 0, 0)),
                  pl.BlockSpec((None,) + w_pool.shape[1:], lambda i: (layer, 0, 0, 0)),
                  pl.BlockSpec((None, 1, c), lambda i: (layer, 0, 0))],
        out_specs=pl.BlockSpec((nb, t, c), lambda i: (0, 0, 0)),
        out_shape=jax.ShapeDtypeStruct((nb, t, c), BF16),
        scratch_shapes=[pltpu.VMEM((nb * (POOL_HALO + t), c), F32)],
        compiler_params=_params("arbitrary"),
        name="pool_batch",
    )(u, buf, w_pool, pool_scale)


def _sb_blocks(qbs, kblks, vblks, masks, r_runs, upper, scale):
    zs = [_dot_nt(qb, kblk.astype(BF16)) * scale for qb, kblk in zip(qbs, kblks)]
    l1ms = []
    for z, mask in zip(zs, masks):
        l1m = _log_sigmoid(-z)
        l1ms.append(l1m if mask is None else jnp.where(mask, l1m, 0.0))
    parts = [_split3(l1m) for l1m in l1ms]
    laters = [_dot(p3[0], upper) + _dot(p3[1], upper) + _dot(p3[2], upper) for p3 in parts]
    avs = []
    for z, l1m, later, r_run, mask in zip(zs, l1ms, laters, r_runs, masks):
        a = jnp.exp(z + l1m + later + r_run)
        avs.append(a if mask is None else jnp.where(mask, a, 0.0))
    contribs = [_dot(a.astype(BF16), vblk.astype(BF16)) for a, vblk in zip(avs, vblks)]
    rowsums = [jnp.sum(l1m, axis=1, keepdims=True) for l1m in l1ms]
    return contribs, rowsums


def _strict_upper(tk):
    j = lax.broadcasted_iota(jnp.int32, (tk, tk), 0)
    s = lax.broadcasted_iota(jnp.int32, (tk, tk), 1)
    return jnp.where(j > s, 1.0, 0.0).astype(BF16)


def _sb_stream_kernel(q_ref, k_ref, v_ref, o_ref, acc_s, run_s, *, tk, scale):
    tq = q_ref.shape[0]
    nsub = tq // tk
    qi = pl.program_id(1)
    upper = _strict_upper(tk)
    qbs = [q_ref[s * tk:(s + 1) * tk, :].astype(BF16) for s in range(nsub)]
    row = lax.broadcasted_iota(jnp.int32, (tk, tk), 0)
    col = lax.broadcasted_iota(jnp.int32, (tk, tk), 1)

    diag0 = [pl.multiple_of((qi * nsub + s) * tk, tk) for s in range(nsub)]
    contribs, rowsums = _sb_blocks(qbs, [k_ref[pl.ds(k0, tk), :] for k0 in diag0],
                                   [v_ref[pl.ds(k0, tk), :] for k0 in diag0], [col < row] * nsub,
                                   [jnp.zeros((tk, 1), F32)] * nsub, upper, scale)
    live = None
    for s in range(nsub):
        acc_s[s * tk:(s + 1) * tk, :] = contribs[s]
        run_s[s * tk:(s + 1) * tk, :] = rowsums[s]
        m = jnp.max(rowsums[s])
        live = m if live is None else jnp.maximum(live, m)

    def cond(carry):
        i, live = carry
        return jnp.logical_and(i <= qi * nsub + nsub - 1, live > SB_DEAD_LOG)

    def body(carry):
        i, _ = carry
        kbs = [qi * nsub + s - i for s in range(nsub)]
        k0s = [pl.multiple_of(jnp.maximum(kb, 0) * tk, tk) for kb in kbs]
        r_runs = [run_s[s * tk:(s + 1) * tk, :] for s in range(nsub)]
        contribs, rowsums = _sb_blocks(qbs, [k_ref[pl.ds(k0, tk), :] for k0 in k0s],
                                       [v_ref[pl.ds(k0, tk), :] for k0 in k0s], [None] * nsub,
                                       r_runs, upper, scale)
        live = None
        for s in range(nsub):
            has_keys = kbs[s] >= 0
            rows = slice(s * tk, (s + 1) * tk)
            acc_s[rows, :] += jnp.where(has_keys, contribs[s], 0.0)
            r_new = jnp.where(has_keys, r_runs[s] + rowsums[s], 2.0 * SB_DEAD_LOG)
            run_s[rows, :] = r_new
            m = jnp.max(r_new)
            live = m if live is None else jnp.maximum(live, m)
        return i + 1, live

    lax.while_loop(cond, body, (jnp.int32(1), live))
    o_ref[...] = acc_s[...].astype(o_ref.dtype)


def sb_stream(p, col_q, kbuf, vbuf, layer, *, tq, tk, dest=None, dest_col_block=0):
    _, heads, nrows, _ = kbuf.shape
    assert nrows % tq == 0 and tq % tk == 0
    scale = HEAD_DIM ** -0.5
    kv = pl.BlockSpec((None, None, nrows, HEAD_DIM), lambda h, i: (layer, h, 0, 0))
    body = functools.partial(_sb_stream_kernel, tk=tk, scale=scale)
    in_specs = [pl.BlockSpec((tq, HEAD_DIM), lambda h, i: (i, col_q + h)), kv, kv]
    args = (p, kbuf, vbuf)
    if dest is not None:
        body = _writes_into(body, len(in_specs))
        in_specs = in_specs + [pl.BlockSpec(memory_space=pl.ANY)]
        args = args + (dest,)
    return pl.pallas_call(
        body,
        grid=(heads, nrows // tq),
        in_specs=in_specs,
        out_specs=pl.BlockSpec((tq, HEAD_DIM), lambda h, i: (i, dest_col_block + h)),
        out_shape=jax.ShapeDtypeStruct((nrows, heads * HEAD_DIM) if dest is None else dest.shape, BF16),
        scratch_shapes=[pltpu.VMEM((tq, HEAD_DIM), F32), pltpu.VMEM((tq, 1), F32)],
        compiler_params=_params("parallel", "arbitrary"),
        name="sb_stream",
        **_dest_kwargs(dest, len(args) - 1),
    )(*args)


def _head_cols(pair_refs, h):
    return pair_refs[h // 2][:, (h % 2) * HEAD_DIM:(h % 2 + 1) * HEAD_DIM]


def _sb_past_chunk(qbs, kp_ref, vp_ref, accs, r_runs, upper, tk, scale):
    heads = len(qbs)
    for j in reversed(range(kp_ref.shape[1] // tk)):
        contribs, rowsums = _sb_blocks(qbs, [kp_ref[h, j * tk:(j + 1) * tk, :] for h in range(heads)],
                                       [vp_ref[h, j * tk:(j + 1) * tk, :] for h in range(heads)],
                                       [None] * heads, r_runs, upper, scale)
        accs = [a + c for a, c in zip(accs, contribs)]
        r_runs = [r + s for r, s in zip(r_runs, rowsums)]
    return accs, r_runs


def _sb_decode_first_kernel(*refs, heads, tk, scale):
    npair = heads // 2
    q_refs = refs[0:npair]
    kn_ref, vn_ref, kp_ref, vp_ref, o_ref, acc_ref, run_ref, live_ref, kpad_s, vpad_s = refs[npair:]
    t = o_ref.shape[0]
    upper = _strict_upper(tk)
    kpad_s[...] = jnp.zeros_like(kpad_s)
    vpad_s[...] = jnp.zeros_like(vpad_s)
    row = lax.broadcasted_iota(jnp.int32, (t, tk), 0)
    col = lax.broadcasted_iota(jnp.int32, (t, tk), 1)
    qbs = [_head_cols(q_refs, h).astype(BF16) for h in range(heads)]
    for h in range(heads):
        kpad_s[h, 0:t, :] = kn_ref[h]
        vpad_s[h, 0:t, :] = vn_ref[h]
    accs, r_runs = _sb_blocks(qbs, [kpad_s[h] for h in range(heads)], [vpad_s[h] for h in range(heads)],
                              [col < row] * heads, [jnp.zeros((t, 1), F32)] * heads, upper, scale)
    accs, r_runs = _sb_past_chunk(qbs, kp_ref, vp_ref, accs, r_runs, upper, tk, scale)
    live = None
    for h in range(heads):
        o_ref[:, h * HEAD_DIM:(h + 1) * HEAD_DIM] = accs[h].astype(o_ref.dtype)
        acc_ref[h] = accs[h]
        run_ref[h] = jnp.broadcast_to(r_runs[h], (t, HEAD_DIM))
        m = jnp.max(r_runs[h])
        live = m if live is None else jnp.maximum(live, m)
    live_ref[...] = jnp.full(live_ref.shape, live, F32)


def _sb_decode_rest_kernel(*refs, heads, tk, scale):
    npair = heads // 2
    q_refs = refs[0:npair]
    acc_in, run_in, kp_ref, vp_ref, o_ref, acc_s, run_s, live_s = refs[npair:]
    step = pl.program_id(1)
    upper = _strict_upper(tk)

    @pl.when(step == 0)
    def _():
        acc_s[...] = acc_in[...]
        run_s[...] = run_in[...]
        live_s[0] = jnp.max(run_in[...])

    @pl.when(live_s[0] > SB_DEAD_LOG)
    def _():
        qbs = [_head_cols(q_refs, h).astype(BF16) for h in range(heads)]
        accs, r_runs = _sb_past_chunk(qbs, kp_ref, vp_ref, [acc_s[h] for h in range(heads)],
                                      [run_s[h][:, 0:1] for h in range(heads)], upper, tk, scale)
        live = None
        for h in range(heads):
            acc_s[h] = accs[h]
            run_s[h] = jnp.broadcast_to(r_runs[h], run_s.shape[1:])
            m = jnp.max(r_runs[h])
            live = m if live is None else jnp.maximum(live, m)
        live_s[0] = live

    @pl.when(step == pl.num_programs(1) - 1)
    def _():
        for h in range(heads):
            o_ref[:, h * HEAD_DIM:(h + 1) * HEAD_DIM] = acc_s[h].astype(o_ref.dtype)


def sb_decode(p, row0, nb, t, col_q, k_new, v_new, k_past, v_past, layer, *, heads, tk, tp_first, tp_rest):
    npast = k_past.shape[3]
    nrest = (npast - tp_first) // tp_rest
    assert row0 % t == 0 and t <= tk and heads % 2 == 0 and tp_first % tk == 0 and tp_rest % tk == 0
    assert npast % tp_first == 0 and nrest * tp_rest == npast - tp_first
    assert col_q % 2 == 0
    rb0 = row0 // t
    npair = heads // 2
    scale = HEAD_DIM ** -0.5
    width = heads * HEAD_DIM
    o_shape = jax.ShapeDtypeStruct((nb * t, width), BF16)
    state_shape = jax.ShapeDtypeStruct((nb, heads, t, HEAD_DIM), F32)

    def new(cb, nax):
        if nax == 1:
            return [pl.BlockSpec((t, 2 * HEAD_DIM), lambda b, j=j: (rb0 + b, cb // 2 + j)) for j in range(npair)]
        return [pl.BlockSpec((t, 2 * HEAD_DIM), lambda b, c, j=j: (rb0 + b, cb // 2 + j)) for j in range(npair)]

    fresh = pl.BlockSpec((None, heads, t, HEAD_DIM), lambda b: (layer, 0, b, 0))
    newest = pl.BlockSpec((None, None, heads, tp_first, HEAD_DIM),
                          lambda b: (layer, b, 0, npast // tp_first - 1, 0))
    o_first, acc, run, live = pl.pallas_call(
        functools.partial(_sb_decode_first_kernel, heads=heads, tk=tk, scale=scale),
        grid=(nb,),
        in_specs=new(col_q, 1) + [fresh, fresh, newest, newest],
        out_specs=[pl.BlockSpec((t, width), lambda b: (b, 0)),
                   pl.BlockSpec((None, heads, t, HEAD_DIM), lambda b: (b, 0, 0, 0)),
                   pl.BlockSpec((None, heads, t, HEAD_DIM), lambda b: (b, 0, 0, 0)),
                   pl.BlockSpec((None, 8, HEAD_DIM), lambda b: (b, 0, 0))],
        out_shape=[o_shape, state_shape, state_shape, jax.ShapeDtypeStruct((nb, 8, HEAD_DIM), F32)],
        scratch_shapes=[pltpu.VMEM((heads, tk, HEAD_DIM), F32), pltpu.VMEM((heads, tk, HEAD_DIM), F32)],
        compiler_params=_params("parallel"),
        name="sb_decode_first",
    )(*([p] * npair), k_new, v_new, k_past, v_past)
    if nrest == 0:
        return o_first

    def older():
        state = pl.BlockSpec((None, heads, t, HEAD_DIM), lambda b, c: (b, 0, 0, 0))
        chunk = pl.BlockSpec((None, None, heads, tp_rest, HEAD_DIM), lambda b, c: (layer, b, 0, nrest - 1 - c, 0))
        return pl.pallas_call(
            functools.partial(_sb_decode_rest_kernel, heads=heads, tk=tk, scale=scale),
            grid=(nb, nrest),
            in_specs=new(col_q, 2) + [state, state, chunk, chunk],
            out_specs=pl.BlockSpec((t, width), lambda b, c: (b, 0)),
            out_shape=o_shape,
            scratch_shapes=[pltpu.VMEM((heads, t, HEAD_DIM), F32), pltpu.VMEM((heads, t, HEAD_DIM), F32),
                            pltpu.SMEM((1,), F32)],
            compiler_params=_params("parallel", "arbitrary"),
            name="sb_decode_rest",
        )(*([p] * npair), acc, run, k_past, v_past)

    return lax.cond(jnp.max(live) > SB_DEAD_LOG, older, lambda: o_first)


def kernel(x_prompt, x_sample, cache_sb_k, cache_sb_v, state_hgrn, state_pool, w_in, lb_param, hgrn_norm,
           w_pool, pool_scale, w_out, ln1_g, ln1_b, ln2_g, ln2_b, ffn_w1, ffn_w2, moe_router, moe_w1, moe_w2):
    bp, seq, d_model = x_prompt.shape
    nb, dseq, _ = x_sample.shape
    depth = w_in.shape[0]
    past_len = cache_sb_k.shape[2]
    a_heads = state_hgrn.shape[2]
    c_heads = cache_sb_k.shape[3]
    a_width = a_heads * HEAD_DIM
    c_width = c_heads * HEAD_DIM
    b_width = state_pool.shape[3]
    ne = moe_router.shape[2]
    d_ff = ffn_w2.shape[1]
    assert bp == 1 and a_width == c_width
    alpha = (2 * depth) ** 0.25
    np_rows = bp * seq
    ns_rows = nb * dseq
    m = np_rows + ns_rows

    pool_cb = 4 * a_width // b_width
    cq = (4 * a_width + b_width) // HEAD_DIM
    ck = cq + c_heads
    cv = ck + c_heads
    k_col0 = ck * HEAD_DIM
    u_col0 = 4 * a_width

    pl_soft = jax.nn.softmax(lb_param.astype(F32), axis=0)
    lb_all = (jnp.cumsum(pl_soft, axis=0) - pl_soft[0])[:, None, :]
    gnorm = hgrn_norm.reshape(depth, 1, a_width)
    pscale = pool_scale.reshape(depth, 1, b_width)
    vec = lambda a: a.reshape(depth, 1, d_model)
    ln1g, ln1b, ln2g, ln2b = vec(ln1_g), vec(ln1_b), vec(ln2_g), vec(ln2_b)
    router_p = jnp.pad(moe_router, ((0, 0), (0, 0), (0, HEAD_DIM - ne)))
    k_cache = jnp.transpose(cache_sb_k, (0, 1, 3, 2, 4))
    v_cache = jnp.transpose(cache_sb_v, (0, 1, 3, 2, 4))
    pool_hist = jnp.pad(state_pool, ((0, 0), (0, 0), (POOL_HALO - state_pool.shape[2], 0), (0, 0)))
    ffn_w2_2 = ffn_w2.reshape(-1, d_model)

    x = jnp.concatenate([x_prompt.reshape(np_rows, d_model), x_sample.reshape(ns_rows, d_model)], axis=0)
    xb = x.astype(BF16)

    kp_buf = jnp.zeros((depth, c_heads, np_rows, HEAD_DIM), F32)
    vp_buf = jnp.zeros((depth, c_heads, np_rows, HEAD_DIM), F32)
    ks_buf = jnp.zeros((depth, c_heads, ns_rows, HEAD_DIM), F32)
    vs_buf = jnp.zeros((depth, c_heads, ns_rows, HEAD_DIM), F32)

    outs = {k: [] for k in ("ps", "pb", "ss", "sb")}
    for l in range(depth):
        p = matmul_layer(xb, w_in, l, k_col0, tm=2112, tn=256)
        kp_buf, vp_buf = kv_proj(xb, w_in, l, ck, cv, kp_buf, vp_buf, 0, tm=2048)
        ks_buf, vs_buf = kv_proj(xb, w_in, l, ck, cv, ks_buf, vs_buf, np_rows, tm=ns_rows)

        mix = jnp.zeros((m, d_model), BF16)
        mix, st_p = hgrn_mixer(p, 0, np_rows, lb_all, gnorm, l, None, per_block_state=False, width=a_width,
                               dest=mix)
        oa_s, st_s = hgrn_mixer(p, np_rows, ns_rows, lb_all, gnorm, l, state_hgrn[l],
                                per_block_state=True, width=a_width)
        mix = pool_stream(p, np_rows, pool_cb, w_pool, pscale, l, width=b_width, tile=512,
                          dest=mix, dest_col_block=(a_width + c_width) // b_width)
        u_s = p[np_rows:, u_col0:u_col0 + b_width].reshape(nb, dseq, b_width)
        ob_s = pool_batch(u_s, pool_hist[l], w_pool, pscale, l, start=past_len).reshape(ns_rows, b_width)
        mix = sb_stream(p, cq, kp_buf, vp_buf, l, tq=1024, tk=128, dest=mix, dest_col_block=a_heads)
        oc_s = sb_decode(p, np_rows, nb, dseq, cq, ks_buf, vs_buf, k_cache, v_cache, l,
                         heads=c_heads, tk=128, tp_first=256, tp_rest=768)
        mix = lax.dynamic_update_slice(mix, jnp.concatenate([oa_s, oc_s, ob_s], axis=1), (np_rows, 0))

        segments = ((0, 0, a_width), (a_width, a_width + b_width, c_width), (a_width + c_width, a_width, b_width))
        h, hb = segment_matmul_residual_ln(mix, w_out, x, ln1g, ln1b, l, segments, alpha=alpha, tm=264)

        if l % 2 == 0:
            mid = swiglu_up(hb, ffn_w1, l // 2, tm=1056, tn=512)
            x, xb = matmul_residual_ln(mid, ffn_w2_2, (l // 2) * d_ff, h, ln2g, ln2b, l,
                                       alpha=alpha, tm=704, tk=512)
        else:
            x, xb = routed_ffn(h, router_p, moe_w1, moe_w2, l // 2, ln2g, ln2b, l, ne, alpha=alpha,
                               tm_route=528, tm_comb=256, tf=512)

        outs["ps"].append(st_p)
        outs["pb"].append(p[np_rows - (POOL_HALO - 1):np_rows, u_col0:u_col0 + b_width][None])
        outs["ss"].append(st_s)
        outs["sb"].append(u_s[:, dseq - (POOL_HALO - 1):, :])

    stack = lambda k: jnp.stack(outs[k])
    prompt_kv = lambda a: jnp.transpose(a, (0, 2, 1, 3)).reshape(depth, bp, seq, c_heads, HEAD_DIM)
    sample_kv = lambda a: jnp.transpose(a, (0, 2, 1, 3)).reshape(depth, nb, dseq, c_heads, HEAD_DIM)
    return (x[:np_rows].reshape(bp, seq, d_model), x[np_rows:].reshape(nb, dseq, d_model),
            stack("ps"), stack("pb"), prompt_kv(kp_buf), prompt_kv(vp_buf),
            stack("ss"), stack("sb"), sample_kv(ks_buf), sample_kv(vs_buf))
```
